```python
import jax, jax.numpy as jnp
from jax import lax
import numpy as np

D_MODEL = 1024
BATCH = 2
SEQ = 8192
DEPTH = 4
DEC_BATCH = 128
DEC_SEQ = 1
PAST_LEN = 2048
PAGE_SIZE = 128

N_LAYERS_A = DEPTH // 2
N_LAYERS_B = DEPTH - N_LAYERS_A
HEADS_A = 8
DK_A = 128
DV_A = 128
KEY_W_A = HEADS_A * DK_A
VAL_W_A = HEADS_A * DV_A
QKV_W_A = 2 * KEY_W_A + VAL_W_A
PROJ_W_A = QKV_W_A + VAL_W_A + 2 * HEADS_A
CONV_W = 4
GDN_CHUNK = 64
BRANCH_WINDOWS = (128, 512, 2048)
BRANCH_DILATIONS = (1, 4, 16)
N_BRANCH = 3
HEADS_B = 8
KV_HEADS_B = 2
GQA_B = HEADS_B // KV_HEADS_B
HEAD_DIM_B = 64
Q_W_B = N_BRANCH * HEADS_B * HEAD_DIM_B
KV_W_B = N_BRANCH * 2 * KV_HEADS_B * HEAD_DIM_B
OUT_W_B = HEADS_B * HEAD_DIM_B
Q_BLOCK = 128
D_FF = 4 * D_MODEL
EPS = 1e-6

kernel_name = 'yoco_gdn_dilated_alibi_decoder_step'


def rmsnorm(x, gain):
    xf = x.astype(jnp.float32)
    y = xf * lax.rsqrt(jnp.mean(xf * xf, axis=-1, keepdims=True) + EPS)
    return (y * gain.astype(jnp.float32)).astype(x.dtype)


def l2norm(x):
    return x * lax.rsqrt(jnp.sum(x * x, axis=-1, keepdims=True) + EPS)


def sq_relu_mlp(h, w_up, w_down):
    u = jax.nn.relu(h @ w_up)
    return (u * u) @ w_down


def alibi_slopes():
    return jnp.exp2(-8.0 * jnp.arange(1, HEADS_B + 1, dtype=jnp.float32) / HEADS_B)


def gated_delta_rule(q, k, v, g, beta, s0):
    N, L, H, _ = q.shape
    C = GDN_CHUNK
    n_chunks = -(-L // C)
    pad = n_chunks * C - L

    def to_chunks(t):
        t = jnp.pad(t.astype(jnp.float32), [(0, 0), (0, pad)] + [(0, 0)] * (t.ndim - 2))
        t = t.reshape((N, n_chunks, C) + t.shape[2:])
        return t.transpose((1, 0, 3, 2) + tuple(range(4, t.ndim)))

    qc, kc, vc, gc, bc = (to_chunks(t) for t in (q, k, v, g, beta))
    G = jnp.cumsum(gc, axis=-1)
    causal = jnp.tril(jnp.ones((C, C), dtype=bool))
    strict = jnp.tril(jnp.ones((C, C), dtype=bool), -1)
    diff = G[..., :, None] - G[..., None, :]
    decay = jnp.where(causal, jnp.exp(jnp.where(causal, diff, 0.0)), 0.0)
    kb = kc * bc[..., None]
    lower = jnp.where(strict, jnp.einsum('...ik,...jk->...ij', kb, kc) * decay, 0.0)
    eye = jnp.broadcast_to(jnp.eye(C, dtype=jnp.float32), lower.shape)
    T = lax.linalg.triangular_solve(eye + lower, eye, left_side=True, lower=True, unit_diagonal=True)
    u = jnp.einsum('...ij,...jv->...iv', T, vc * bc[..., None])
    w = jnp.einsum('...ij,...jk->...ik', T, kb * jnp.exp(G)[..., None])
    attn = jnp.einsum('...ik,...jk->...ij', qc, kc) * decay
    q_dec = qc * jnp.exp(G)[..., None]
    G_last = G[..., -1]
    k_dec = kc * jnp.exp(G_last[..., None] - G)[..., None]
    decay_last = jnp.exp(G_last)

    def step(S, inp):
        u_c, w_c, qd_c, kd_c, a_c, dl_c = inp
        v_new = u_c - jnp.einsum('nhck,nhkv->nhcv', w_c, S)
        o_c = jnp.einsum('nhck,nhkv->nhcv', qd_c, S) + jnp.einsum('nhcj,nhjv->nhcv', a_c, v_new)
        S = S * dl_c[..., None, None] + jnp.einsum('nhck,nhcv->nhkv', kd_c, v_new)
        return S, o_c

    S_fin, o = lax.scan(step, s0.astype(jnp.float32), (u, w, q_dec, k_dec, attn, decay_last))
    o = o.transpose(1, 0, 3, 2, 4).reshape(N, n_chunks * C, H, DV_A)[:, :L]
    return o, S_fin


def gdn_mixer(h, s0, conv_buf, w_in, w_conv, a_log, dt_bias, g_norm, w_out):
    N, L, _ = h.shape
    proj = h @ w_in
    qkv = proj[..., :QKV_W_A]
    z = proj[..., QKV_W_A:QKV_W_A + VAL_W_A]
    b_raw = proj[..., QKV_W_A + VAL_W_A:QKV_W_A + VAL_W_A + HEADS_A]
    a_raw = proj[..., QKV_W_A + VAL_W_A + HEADS_A:]
    ext = jnp.concatenate([conv_buf.astype(qkv.dtype), qkv], axis=1)
    new_conv = ext[:, ext.shape[1] - (CONV_W - 1):]
    conv = sum(ext[:, j:j + L] * w_conv[j] for j in range(CONV_W))
    qkv_c = jax.nn.silu(conv.astype(jnp.float32))
    q = l2norm(qkv_c[..., :KEY_W_A].reshape(N, L, HEADS_A, DK_A)) * (DK_A ** -0.5)
    k = l2norm(qkv_c[..., KEY_W_A:2 * KEY_W_A].reshape(N, L, HEADS_A, DK_A))
    v = qkv_c[..., 2 * KEY_W_A:].reshape(N, L, HEADS_A, DV_A)
    beta = jax.nn.sigmoid(b_raw.astype(jnp.float32))
    g = -jnp.exp(a_log.astype(jnp.float32)) * jax.nn.softplus(a_raw.astype(jnp.float32) + dt_bias.astype(jnp.float32))
    o, s_new = gated_delta_rule(q, k, v, g, beta, s0)
    o = o * lax.rsqrt(jnp.mean(o * o, axis=-1, keepdims=True) + EPS) * g_norm.astype(jnp.float32)
    o = o * jax.nn.silu(z.astype(jnp.float32).reshape(N, L, HEADS_A, DV_A))
    out = o.reshape(N, L, VAL_W_A).astype(h.dtype) @ w_out
    return out, s_new.astype(s0.dtype), new_conv.astype(conv_buf.dtype)


def dilated_branch(q, k_ext, v_ext, q_start, window, dilation, slopes):
    N, Lq = q.shape[0], q.shape[1]
    qb = Q_BLOCK if Lq % Q_BLOCK == 0 else Lq
    nb = Lq // qb
    dist = jnp.arange(window // dilation + 1, dtype=jnp.int32) * dilation
    bias = -slopes[:, :, None] * dist.astype(jnp.float32)
    scale = HEAD_DIM_B ** -0.5

    def block(args):
        q_blk, start = args
        kidx = (start + jnp.arange(qb, dtype=jnp.int32))[:, None] - dist[None, :]
        valid = kidx >= 0
        kidx = jnp.maximum(kidx, 0)
        kg = jnp.take(k_ext, kidx, axis=1).astype(jnp.float32)
        vg = jnp.take(v_ext, kidx, axis=1).astype(jnp.float32)
        s = jnp.einsum('nqhgd,nqkhd->nqhgk', q_blk.astype(jnp.float32), kg) * scale + bias
        s = jnp.where(valid[None, :, None, None, :], s, -jnp.inf)
        m = jnp.max(s, axis=-1, keepdims=True)
        lse = m + jnp.log(jnp.sum(jnp.exp(s - m), axis=-1, keepdims=True))
        p = jnp.exp(s - lse)
        o = jnp.einsum('nqhgk,nqkhd->nqhgd', p, vg)
        return o, lse[..., 0]

    q_blocks = q.reshape((N, nb, qb) + q.shape[2:]).swapaxes(0, 1)
    starts = q_start + jnp.arange(nb, dtype=jnp.int32) * qb
    o, lse = lax.map(block, (q_blocks, starts))
    o = o.swapaxes(0, 1).reshape((N, Lq) + o.shape[3:])
    lse = lse.swapaxes(0, 1).reshape((N, Lq) + lse.shape[3:])
    return o, lse


def dilated_mixer(h, kv_exts, buf_lens, w_q, w_o):
    N, L, _ = h.shape
    q = (h @ w_q).reshape(N, L, N_BRANCH, KV_HEADS_B, GQA_B, HEAD_DIM_B)
    slopes = alibi_slopes().reshape(KV_HEADS_B, GQA_B)
    outs, lses = [], []
    for bi in range(N_BRANCH):
        o, lse = dilated_branch(q[:, :, bi], kv_exts[bi][:, :, 0], kv_exts[bi][:, :, 1], buf_lens[bi],
                                BRANCH_WINDOWS[bi], BRANCH_DILATIONS[bi], slopes)
        outs.append(o)
        lses.append(lse)
    wts = jax.nn.softmax(jnp.stack(lses), axis=0)
    out = jnp.sum(wts[..., None] * jnp.stack(outs), axis=0)
    return out.reshape(N, L, OUT_W_B).astype(h.dtype) @ w_o


def trunk(x, rec0, conv0, kv_bufs, w):
    N, L, _ = x.shape
    rec_new, conv_new, bufs_new, kv_exts = [], [], [], []
    buf_lens = [b.shape[1] for b in kv_bufs]
    for l in range(DEPTH):
        if l < N_LAYERS_A:
            mix, s_l, c_l = gdn_mixer(rmsnorm(x, w['norm_mix'][l]), rec0[l], conv0[l], w['w_in_a'][l],
                                      w['conv_a'][l], w['a_log'][l], w['dt_bias'][l], w['norm_o_a'][l],
                                      w['w_out_a'][l])
            rec_new.append(s_l)
            conv_new.append(c_l)
        else:
            if l == N_LAYERS_A:
                kv = (rmsnorm(x, w['norm_kv']) @ w['w_kv']).reshape(N, L, N_BRANCH, 2, KV_HEADS_B, HEAD_DIM_B)
                for bi in range(N_BRANCH):
                    e = jnp.concatenate([kv_bufs[bi].astype(kv.dtype), kv[:, :, bi]], axis=1)
                    kv_exts.append(e)
                    keep = min(BRANCH_WINDOWS[bi], e.shape[1])
                    bufs_new.append(e[:, e.shape[1] - keep:].astype(kv_bufs[bi].dtype))
            mix = dilated_mixer(rmsnorm(x, w['norm_mix'][l]), kv_exts, buf_lens,
                                w['w_q_b'][l - N_LAYERS_A], w['w_o_b'][l - N_LAYERS_A])
        x = x + mix
        x = x + sq_relu_mlp(rmsnorm(x, w['norm_mlp'][l]), w['w_up'][l], w['w_down'][l])
    y = rmsnorm(x, w['norm_final'])
    return y, jnp.stack(rec_new), jnp.stack(conv_new), bufs_new[0], bufs_new[1], bufs_new[2]


def setup_inputs(seed: int = 0) -> dict:
    key = jax.random.key(seed)
    ks = jax.random.split(key, 24)
    f32 = jnp.float32
    nrm = lambda k, shape, s: jax.random.normal(k, shape, f32) * s
    gain = lambda k, shape: 1.0 + 0.02 * jax.random.normal(k, shape, f32)
    dt = jnp.exp(jax.random.uniform(ks[5], (N_LAYERS_A, HEADS_A), f32, np.log(1e-3), np.log(1e-1)))
    return {
        'x_prompt': nrm(ks[0], (BATCH, SEQ, D_MODEL), 1.0),
        'x_sample': nrm(ks[1], (DEC_BATCH, DEC_SEQ, D_MODEL), 1.0),
        'state_a_rec': nrm(ks[2], (N_LAYERS_A, DEC_BATCH, HEADS_A, DK_A, DV_A), 0.1),
        'state_a_conv': nrm(ks[3], (N_LAYERS_A, DEC_BATCH, CONV_W - 1, QKV_W_A), 1.0),
        'cache_b0_kv': nrm(ks[4], (DEC_BATCH, min(BRANCH_WINDOWS[0], PAST_LEN), 2, KV_HEADS_B, HEAD_DIM_B), 1.0),
        'cache_b1_kv': nrm(ks[6], (DEC_BATCH, min(BRANCH_WINDOWS[1], PAST_LEN), 2, KV_HEADS_B, HEAD_DIM_B), 1.0),
        'cache_b2_kv': nrm(ks[7], (DEC_BATCH, min(BRANCH_WINDOWS[2], PAST_LEN), 2, KV_HEADS_B, HEAD_DIM_B), 1.0),
        'norm_mix': gain(ks[8], (DEPTH, D_MODEL)),
        'norm_mlp': gain(ks[9], (DEPTH, D_MODEL)),
        'w_in_a': nrm(ks[10], (N_LAYERS_A, D_MODEL, PROJ_W_A), D_MODEL ** -0.5),
        'conv_a': nrm(ks[11], (N_LAYERS_A, CONV_W, QKV_W_A), CONV_W ** -0.5),
        'a_log': jnp.log(jax.random.uniform(ks[12], (N_LAYERS_A, HEADS_A), f32, 1.0, 16.0)),
        'dt_bias': dt + jnp.log(-jnp.expm1(-dt)),
        'norm_o_a': gain(ks[13], (N_LAYERS_A, DV_A)),
        'w_out_a': nrm(ks[14], (N_LAYERS_A, VAL_W_A, D_MODEL), VAL_W_A ** -0.5),
        'norm_kv': gain(ks[15], (D_MODEL,)),
        'w_kv': nrm(ks[16], (D_MODEL, KV_W_B), D_MODEL ** -0.5),
        'w_q_b': nrm(ks[17], (N_LAYERS_B, D_MODEL, Q_W_B), D_MODEL ** -0.5),
        'w_o_b': nrm(ks[18], (N_LAYERS_B, OUT_W_B, D_MODEL), OUT_W_B ** -0.5),
        'w_up': nrm(ks[19], (DEPTH, D_MODEL, D_FF), D_MODEL ** -0.5),
        'w_down': nrm(ks[20], (DEPTH, D_FF, D_MODEL), D_FF ** -0.5),
        'norm_final': gain(ks[21], (D_MODEL,)),
    }


def reference(x_prompt, x_sample, state_a_rec, state_a_conv, cache_b0_kv, cache_b1_kv, cache_b2_kv,
              norm_mix, norm_mlp, w_in_a, conv_a, a_log, dt_bias, norm_o_a, w_out_a, norm_kv, w_kv,
              w_q_b, w_o_b, w_up, w_down, norm_final):
    w = {'norm_mix': norm_mix, 'norm_mlp': norm_mlp, 'w_in_a': w_in_a, 'conv_a': conv_a, 'a_log': a_log,
         'dt_bias': dt_bias, 'norm_o_a': norm_o_a, 'w_out_a': w_out_a, 'norm_kv': norm_kv, 'w_kv': w_kv,
         'w_q_b': w_q_b, 'w_o_b': w_o_b, 'w_up': w_up, 'w_down': w_down, 'norm_final': norm_final}
    p_rec0 = jnp.zeros((N_LAYERS_A, BATCH, HEADS_A, DK_A, DV_A), state_a_rec.dtype)
    p_conv0 = jnp.zeros((N_LAYERS_A, BATCH, CONV_W - 1, QKV_W_A), state_a_conv.dtype)
    p_bufs0 = [jnp.zeros((BATCH, 0, 2, KV_HEADS_B, HEAD_DIM_B), c.dtype) for c in (cache_b0_kv, cache_b1_kv, cache_b2_kv)]
    y_prompt, p_rec, p_conv, p_kv0, p_kv1, p_kv2 = trunk(x_prompt, p_rec0, p_conv0, p_bufs0, w)
    y_sample, s_rec, s_conv, s_kv0, s_kv1, s_kv2 = trunk(
        x_sample, state_a_rec, state_a_conv, [cache_b0_kv, cache_b1_kv, cache_b2_kv], w)
    return (y_prompt, y_sample, p_rec, p_conv, p_kv0, p_kv1, p_kv2, s_rec, s_conv, s_kv0, s_kv1, s_kv2)
```

```python
import functools

import jax
import jax.numpy as jnp
from jax import lax
from jax.experimental import pallas as pl
from jax.experimental.pallas import tpu as pltpu

D_MODEL = 1024
HEADS_A = 8
DK_A = 128
DV_A = 128
KEY_W_A = HEADS_A * DK_A
VAL_W_A = HEADS_A * DV_A
QKV_W_A = 2 * KEY_W_A + VAL_W_A
QKVZ_W_A = QKV_W_A + VAL_W_A
CONV_W = 4
N_LAYERS_A = 2
N_LAYERS_B = 2
BRANCH_WINDOWS = (128, 512, 2048)
BRANCH_DILATIONS = (1, 4, 16)
N_BRANCH = 3
HEADS_B = 8
KV_HEADS_B = 2
GQA_B = HEADS_B // KV_HEADS_B
HEAD_DIM_B = 64
Q_W_BRANCH = HEADS_B * HEAD_DIM_B
KV_W_BRANCH = 2 * KV_HEADS_B * HEAD_DIM_B
ALIBI_SLOPES = tuple(2.0 ** (-8.0 * h / HEADS_B) for h in range(1, HEADS_B + 1))
EPS = 1e-6

LANES = 128
SUBLANES = 8
VMEM_LIMIT_BYTES = 56 * 1024 * 1024
GDN_CHUNK = 128
ATT_BLOCK = 128

F32 = jnp.float32
BF16 = jnp.bfloat16


def _params(*sem):
    return pltpu.CompilerParams(dimension_semantics=sem, vmem_limit_bytes=VMEM_LIMIT_BYTES)


def _rms(x, gain):
    return x * lax.rsqrt(jnp.mean(x * x, axis=-1, keepdims=True) + EPS) * gain


def _sigmoid(x):
    return 1.0 / (1.0 + jnp.exp(-x))


def _softplus(x):
    return jnp.maximum(x, 0.0) + jnp.log(1.0 + jnp.exp(-jnp.abs(x)))


def _dot(a, b):
    return jnp.dot(a.astype(BF16), b.astype(BF16), preferred_element_type=F32)


def _dot_nt(a, b):
    return lax.dot_general(a.astype(BF16), b.astype(BF16), (((1,), (1,)), ((), ())),
                           preferred_element_type=F32)


def _dot_tn(a, b):
    return lax.dot_general(a.astype(BF16), b.astype(BF16), (((0,), (0,)), ((), ())),
                           preferred_element_type=F32)


def _dot_f32(a, b):
    return jnp.dot(a, b, precision=lax.Precision.HIGHEST, preferred_element_type=F32)


def _norm_matmul_kernel(x_ref, g_ref, w_ref, o_ref, xn_ref):
    @pl.when(pl.program_id(1) == 0)
    def _():
        xn_ref[...] = _rms(x_ref[...], g_ref[...]).astype(BF16)

    o_ref[...] = jnp.dot(xn_ref[...], w_ref[...], preferred_element_type=F32).astype(o_ref.dtype)


def _norm_matmul_split(x, gain, w, tn):
    t, d = x.shape
    f = w.shape[1]
    tm = min(512, t)
    nj = f // tn
    return pl.pallas_call(
        _norm_matmul_kernel,
        grid=(t // tm, nj),
        in_specs=[pl.BlockSpec((tm, d), lambda i, j: (i, 0)),
                  pl.BlockSpec((1, d), lambda i, j: (0, 0)),
                  pl.BlockSpec((d, tn), lambda i, j: (0, j))],
        out_specs=pl.BlockSpec((None, tm, tn), lambda i, j: (j, i, 0)),
        out_shape=jax.ShapeDtypeStruct((nj, t, tn), F32),
        scratch_shapes=[pltpu.VMEM((tm, d), BF16)],
        compiler_params=_params("parallel", "arbitrary"),
        name="norm_matmul_split",
    )(x, gain.reshape(1, d), w.astype(BF16))


def _inproj_kernel(x_ref, g_ref, w_ref, wg_ref, o_ref, og_ref, xn_ref):
    @pl.when(pl.program_id(1) == 0)
    def _():
        xn = _rms(x_ref[...], g_ref[...]).astype(BF16)
        xn_ref[...] = xn
        og_ref[...] = jnp.dot(xn, wg_ref[...], preferred_element_type=F32)

    o_ref[...] = jnp.dot(xn_ref[...], w_ref[...], preferred_element_type=F32)


def _inproj(x, gain, w_in):
    t, d = x.shape
    tm = min(512, t)
    tn = 1024
    w_main = w_in[:, :QKVZ_W_A].astype(BF16)
    w_gate = jnp.pad(w_in[:, QKVZ_W_A:], ((0, 0), (0, LANES - 2 * HEADS_A))).astype(BF16)
    return pl.pallas_call(
        _inproj_kernel,
        grid=(t // tm, QKVZ_W_A // tn),
        in_specs=[pl.BlockSpec((tm, d), lambda i, j: (i, 0)),
                  pl.BlockSpec((1, d), lambda i, j: (0, 0)),
                  pl.BlockSpec((d, tn), lambda i, j: (0, j)),
                  pl.BlockSpec((d, LANES), lambda i, j: (0, 0))],
        out_specs=[pl.BlockSpec((tm, tn), lambda i, j: (i, j)),
                   pl.BlockSpec((tm, LANES), lambda i, j: (i, 0))],
        out_shape=[jax.ShapeDtypeStruct((t, QKVZ_W_A), F32),
                   jax.ShapeDtypeStruct((t, LANES), F32)],
        scratch_shapes=[pltpu.VMEM((tm, d), BF16)],
        compiler_params=_params("parallel", "arbitrary"),
        name="gdn_inproj",
    )(x, gain.reshape(1, d), w_main, w_gate)


def _mlp_kernel(*refs, n_mix, final_norm):
    x_ref = refs[0]
    mix_refs = refs[1:1 + n_mix]
    wo_ref, g_ref, wup_ref, wdn_ref, gf_ref, y_ref, x1_ref, xn_ref, acc_ref = refs[1 + n_mix:]
    f = pl.program_id(1)

    @pl.when(f == 0)
    def _():
        if n_mix == 1:
            mixed = mix_refs[0][...]
        else:
            outs = [r[...] for r in mix_refs[:N_BRANCH]]
            lses = [r[...] for r in mix_refs[N_BRANCH:]]
            m = functools.reduce(jnp.maximum, lses)
            ws = [jnp.exp(l - m) for l in lses]
            mixed = sum(w * o for w, o in zip(ws, outs)) / sum(ws)
        x1 = x_ref[...] + _dot(mixed, wo_ref[...])
        x1_ref[...] = x1
        xn_ref[...] = _rms(x1, g_ref[...]).astype(BF16)
        acc_ref[...] = jnp.zeros_like(acc_ref)

    u = jnp.maximum(jnp.dot(xn_ref[...], wup_ref[...], preferred_element_type=F32), 0.0)
    acc_ref[...] += _dot(u * u, wdn_ref[...])

    @pl.when(f == pl.num_programs(1) - 1)
    def _():
        y = x1_ref[...] + acc_ref[...]
        if final_norm:
            y = _rms(y, gf_ref[...])
        y_ref[...] = y


def _mlp_block(x, mix, w_o, gain, w_up, w_down, gain_final, final_norm):
    t, d = x.shape
    k = w_o.shape[0]
    dff = w_up.shape[1]
    tm = min(512, t)
    tf = 512
    n_mix = len(mix)
    row = lambda i, f: (i, 0)
    const = lambda i, f: (0, 0)
    return pl.pallas_call(
        functools.partial(_mlp_kernel, n_mix=n_mix, final_norm=final_norm),
        grid=(t // tm, dff // tf),
        in_specs=([pl.BlockSpec((tm, d), row)]
                  + [pl.BlockSpec((tm, k), row)] * n_mix
                  + [pl.BlockSpec((k, d), const),
                     pl.BlockSpec((1, d), const),
                     pl.BlockSpec((d, tf), lambda i, f: (0, f)),
                     pl.BlockSpec((tf, d), lambda i, f: (f, 0)),
                     pl.BlockSpec((1, d), const)]),
        out_specs=pl.BlockSpec((tm, d), row),
        out_shape=jax.ShapeDtypeStruct((t, d), F32),
        scratch_shapes=[pltpu.VMEM((tm, d), F32), pltpu.VMEM((tm, d), BF16), pltpu.VMEM((tm, d), F32)],
        compiler_params=_params("parallel", "arbitrary"),
        name="mix_out_mlp",
    )(x, *mix, w_o.astype(BF16), gain.reshape(1, d), w_up.astype(BF16), w_down.astype(BF16),
      gain_final.reshape(1, d))


def _unit_lower_inverse(a, c):
    row = lax.broadcasted_iota(jnp.int32, (c, c), 0)
    col = lax.broadcasted_iota(jnp.int32, (c, c), 1)
    lower = row > col
    eye = (row == col).astype(F32)
    inv = eye - jnp.where(lower & ((row >> 1) == (col >> 1)), a, 0.0)
    shift = 1
    while (1 << shift) < c:
        in_pair = (row >> (shift + 1)) == (col >> (shift + 1))
        off_block = (row >> shift) != (col >> shift)
        e = jnp.where(lower & in_pair & off_block, a, 0.0)
        inv = inv - _dot(_dot(inv, e), inv)
        shift += 1
    return inv


def _gdn_kernel(qkv_ref, z_ref, ba_ref, bat_ref, cw_ref, s0_ref, c0_ref, alr_ref, dtr_ref, alc_ref, dtc_ref,
                gn_ref, o_ref, s_ref, ext_ref, *, chunk, seq_len):
    c = chunk
    ci = pl.program_id(1)
    halo = SUBLANES

    @pl.when(ci == 0)
    def _():
        s_ref[...] = s0_ref[...]
        ext_ref[pl.ds(0, halo), :] = c0_ref[...]

    ext_ref[pl.ds(halo, c), :] = qkv_ref[...]
    conv = ext_ref[pl.ds(halo - (CONV_W - 1), c), :] * cw_ref[0:1, :]
    for j in range(1, CONV_W):
        conv = conv + ext_ref[pl.ds(halo - (CONV_W - 1) + j, c), :] * cw_ref[j:j + 1, :]
    ext_ref[pl.ds(0, halo), :] = ext_ref[pl.ds(c, halo), :]
    act = conv * _sigmoid(conv)

    ba = ba_ref[...]
    bat = bat_ref[...]
    beta_cols = _sigmoid(ba)
    g_cols = -jnp.exp(alr_ref[...]) * _softplus(ba + dtr_ref[...])
    g_rows = -jnp.exp(alc_ref[...]) * _softplus(bat + dtc_ref[...])
    row = lax.broadcasted_iota(jnp.int32, (c, c), 0)
    col = lax.broadcasted_iota(jnp.int32, (c, c), 1)
    causal = row >= col
    strict = row > col
    if seq_len % c:
        tok_c = ci * c + lax.broadcasted_iota(jnp.int32, (c, 1), 0)
        tok_r = ci * c + lax.broadcasted_iota(jnp.int32, (1, c), 1)
        live_c = tok_c < seq_len
        beta_cols = jnp.where(live_c, beta_cols, 0.0)
        g_cols = jnp.where(live_c, g_cols, 0.0)
        g_rows = jnp.where(tok_r < seq_len, g_rows, 0.0)
    else:
        live_c = None
    cum_cols = _dot_f32(causal.astype(F32), g_cols)
    cum_rows = _dot_f32(g_rows, (row <= col).astype(F32))

    for h in range(HEADS_A):
        lo = h * DK_A
        q = act[:, lo:lo + DK_A]
        k = act[:, KEY_W_A + lo:KEY_W_A + lo + DK_A]
        v = act[:, 2 * KEY_W_A + lo:2 * KEY_W_A + lo + DV_A]
        q = q * lax.rsqrt(jnp.sum(q * q, axis=-1, keepdims=True) + EPS) * (DK_A ** -0.5)
        k = k * lax.rsqrt(jnp.sum(k * k, axis=-1, keepdims=True) + EPS)
        if live_c is not None:
            k = jnp.where(live_c, k, 0.0)
        beta = beta_cols[:, h:h + 1]
        gc = cum_cols[:, HEADS_A + h:HEADS_A + h + 1]
        gr = cum_rows[HEADS_A + h:HEADS_A + h + 1, :]
        decay = jnp.where(causal, jnp.exp(jnp.where(causal, gc - gr, 0.0)), 0.0)
        kb = k * beta
        a = jnp.where(strict, _dot_nt(kb, k) * decay, 0.0)
        t_inv = _unit_lower_inverse(a, c)
        e_gc = jnp.exp(gc)
        u = _dot(t_inv, v * beta)
        w = _dot(t_inv, kb * e_gc)
        attn = _dot_nt(q, k) * decay
        s = s_ref[h]
        v_new = u - _dot(w, s)
        o = _dot(q * e_gc, s) + _dot(attn, v_new)
        g_last = gc[c - 1:c, :]
        s_ref[h] = s * jnp.exp(g_last) + _dot_tn(k * jnp.exp(g_last - gc), v_new)
        o = _rms(o, gn_ref[...])
        z = z_ref[:, lo:lo + DV_A]
        o_ref[:, lo:lo + DV_A] = (o * z * _sigmoid(z)).astype(o_ref.dtype)


def _gdn(qkvz, ba, n, seq_len, chunk, s0, conv0, w_conv, a_log, dt_bias, g_norm):
    nc = -(-seq_len // chunk)
    t = n * nc * chunk
    halo = SUBLANES
    bat = ba[:, :2 * HEADS_A].reshape(n * nc, chunk, 2 * HEADS_A).transpose(0, 2, 1)
    c0 = jnp.pad(conv0, ((0, 0), (halo - (CONV_W - 1), 0), (0, 0)))
    pad_row = lambda p: jnp.pad(p.reshape(1, HEADS_A), ((0, 0), (HEADS_A, LANES - 2 * HEADS_A)))
    pad_col = lambda p: jnp.pad(p.reshape(HEADS_A, 1), ((HEADS_A, 0), (0, 0)))
    blk = lambda i, j: (i * nc + j, 0)
    const = lambda i, j: (0, 0)
    return pl.pallas_call(
        functools.partial(_gdn_kernel, chunk=chunk, seq_len=seq_len),
        grid=(n, nc),
        in_specs=[pl.BlockSpec((chunk, QKV_W_A), blk),
                  pl.BlockSpec((chunk, VAL_W_A), lambda i, j: (i * nc + j, QKV_W_A // VAL_W_A)),
                  pl.BlockSpec((chunk, LANES), blk),
                  pl.BlockSpec((None, 2 * HEADS_A, chunk), lambda i, j: (i * nc + j, 0, 0)),
                  pl.BlockSpec((CONV_W, QKV_W_A), const),
                  pl.BlockSpec((None, HEADS_A, DK_A, DV_A), lambda i, j: (i, 0, 0, 0)),
                  pl.BlockSpec((None, halo, QKV_W_A), lambda i, j: (i, 0, 0)),
                  pl.BlockSpec((1, LANES), const),
                  pl.BlockSpec((1, LANES), const),
                  pl.BlockSpec((2 * HEADS_A, 1), const),
                  pl.BlockSpec((2 * HEADS_A, 1), const),
                  pl.BlockSpec((1, DV_A), const)],
        out_specs=[pl.BlockSpec((chunk, VAL_W_A), blk),
                   pl.BlockSpec((None, HEADS_A, DK_A, DV_A), lambda i, j: (i, 0, 0, 0))],
        out_shape=[jax.ShapeDtypeStruct((t, VAL_W_A), BF16),
                   jax.ShapeDtypeStruct((n, HEADS_A, DK_A, DV_A), F32)],
        scratch_shapes=[pltpu.VMEM((halo + chunk, QKV_W_A), F32)],
        compiler_params=_params("parallel", "arbitrary"),
        name="gdn_delta_rule",
    )(qkvz, qkvz, ba, bat, w_conv, s0, c0, pad_row(a_log), pad_row(dt_bias), pad_col(a_log), pad_col(dt_bias),
      g_norm.reshape(1, DV_A))


def _attn_prompt_kernel(q_ref, kvc_ref, kvp_ref, o_ref, lse_ref, *, dilation):
    blk = ATT_BLOCK
    a = pl.program_id(2)
    qi = lax.broadcasted_iota(jnp.int32, (blk, 2 * blk), 0)
    kj = lax.broadcasted_iota(jnp.int32, (blk, 2 * blk), 1)
    steps = blk + qi - kj
    valid = (steps >= 0) & (steps <= blk) & ((kj >= blk) | (a > 0))
    dist = (steps * dilation).astype(F32)
    kvc = kvc_ref[...]
    kvp = kvp_ref[...]
    scale = HEAD_DIM_B ** -0.5
    for kvh in range(KV_HEADS_B):
        klo = kvh * HEAD_DIM_B
        vlo = (KV_HEADS_B + kvh) * HEAD_DIM_B
        k_cat = jnp.concatenate([kvp[:, klo:klo + HEAD_DIM_B], kvc[:, klo:klo + HEAD_DIM_B]], axis=0)
        v_cat = jnp.concatenate([kvp[:, vlo:vlo + HEAD_DIM_B], kvc[:, vlo:vlo + HEAD_DIM_B]], axis=0)
        for g in range(GQA_B):
            h = kvh * GQA_B + g
            lo = h * HEAD_DIM_B
            s = _dot_nt(q_ref[:, lo:lo + HEAD_DIM_B] * scale, k_cat) - ALIBI_SLOPES[h] * dist
            s = jnp.where(valid, s, -jnp.inf)
            m = jnp.max(s, axis=-1, keepdims=True)
            p = jnp.exp(s - m)
            l = jnp.sum(p, axis=-1, keepdims=True)
            o_ref[:, lo:lo + HEAD_DIM_B] = _dot(p, v_cat) / l
            lse_ref[:, lo:lo + HEAD_DIM_B] = jnp.broadcast_to(m + jnp.log(l), (blk, HEAD_DIM_B))


def _attn_prompt_branch(q, kv, n, seq_len, dilation):
    ls = seq_len // dilation
    nq = ls // ATT_BLOCK
    qv = q.reshape(n, ls, dilation * Q_W_BRANCH)
    kvv = kv.reshape(n, ls, dilation * KV_W_BRANCH)
    cur = lambda i, r, a: (i, a, r)
    prev = lambda i, r, a: (i, jnp.maximum(a - 1, 0), r)
    o, lse = pl.pallas_call(
        functools.partial(_attn_prompt_kernel, dilation=dilation),
        grid=(n, dilation, nq),
        in_specs=[pl.BlockSpec((None, ATT_BLOCK, Q_W_BRANCH), cur),
                  pl.BlockSpec((None, ATT_BLOCK, KV_W_BRANCH), cur),
                  pl.BlockSpec((None, ATT_BLOCK, KV_W_BRANCH), prev)],
        out_specs=[pl.BlockSpec((None, ATT_BLOCK, Q_W_BRANCH), cur)] * 2,
        out_shape=[jax.ShapeDtypeStruct(qv.shape, F32)] * 2,
        compiler_params=_params("parallel", "parallel", "arbitrary"),
        name=f"attn_prompt_d{dilation}",
    )(qv, kvv, kvv)
    return o.reshape(n * seq_len, Q_W_BRANCH), lse.reshape(n * seq_len, Q_W_BRANCH)


def _attn_sample_kernel(q_ref, kvn_ref, c0_ref, c1_ref, c2_ref, o_ref):
    nk = ATT_BLOCK
    idx = lax.broadcasted_iota(jnp.int32, (nk, 1), 0)
    scale = HEAD_DIM_B ** -0.5
    for h in range(HEADS_B):
        kvh = h // GQA_B
        klo = kvh * HEAD_DIM_B
        vlo = (KV_HEADS_B + kvh) * HEAD_DIM_B
        lo = h * HEAD_DIM_B
        outs, lses = [], []
        for b, c_ref in enumerate((c0_ref, c1_ref, c2_ref)):
            dist = ((nk - idx) * BRANCH_DILATIONS[b]).astype(F32)
            q = q_ref[b, :, lo:lo + HEAD_DIM_B] * scale
            k_new = kvn_ref[b, :, klo:klo + HEAD_DIM_B]
            v_new = kvn_ref[b, :, vlo:vlo + HEAD_DIM_B]
            k_old = c_ref[:, klo:klo + HEAD_DIM_B]
            v_old = c_ref[:, vlo:vlo + HEAD_DIM_B]
            s_old = jnp.sum(k_old * q, axis=-1, keepdims=True) - ALIBI_SLOPES[h] * dist
            s_new = jnp.sum(k_new * q, axis=-1, keepdims=True)
            m = jnp.maximum(jnp.max(s_old, axis=0, keepdims=True), s_new)
            p_old = jnp.exp(s_old - m)
            p_new = jnp.exp(s_new - m)
            l = jnp.sum(p_old, axis=0, keepdims=True) + p_new
            o = (jnp.sum(p_old * v_old, axis=0, keepdims=True) + p_new * v_new) / l
            outs.append(o)
            lses.append(m + jnp.log(l))
        m = functools.reduce(jnp.maximum, lses)
        ws = [jnp.exp(l - m) for l in lses]
        o_ref[:, lo:lo + HEAD_DIM_B] = (sum(w * o for w, o in zip(ws, outs)) / sum(ws)).astype(o_ref.dtype)


def _attn_sample(q, kv_new, caches):
    n = q.shape[1]
    views = [c.reshape(n, BRANCH_WINDOWS[b] // BRANCH_DILATIONS[b], BRANCH_DILATIONS[b] * KV_W_BRANCH)
             for b, c in enumerate(caches)]
    out = pl.pallas_call(
        _attn_sample_kernel,
        grid=(n,),
        in_specs=[pl.BlockSpec((N_BRANCH, None, 1, Q_W_BRANCH), lambda i: (0, i, 0, 0)),
                  pl.BlockSpec((N_BRANCH, None, 1, KV_W_BRANCH), lambda i: (0, i, 0, 0))]
                 + [pl.BlockSpec((None, ATT_BLOCK, KV_W_BRANCH), lambda i: (i, 0, 0))] * N_BRANCH,
        out_specs=pl.BlockSpec((None, 1, Q_W_BRANCH), lambda i: (i, 0, 0)),
        out_shape=jax.ShapeDtypeStruct((n, 1, Q_W_BRANCH), BF16),
        compiler_params=_params("parallel"),
        name="attn_sample",
    )(q.reshape(N_BRANCH, n, 1, Q_W_BRANCH), kv_new.reshape(N_BRANCH, n, 1, KV_W_BRANCH), *views)
    return out.reshape(n, Q_W_BRANCH)


def _trunk(x, rec0, conv0, kv_bufs, w):
    n, seq_len, d = x.shape
    x = x.reshape(n * seq_len, d)
    chunk = GDN_CHUNK if seq_len >= GDN_CHUNK else SUBLANES
    lp = -(-seq_len // chunk) * chunk
    rec_new, conv_new = [], []
    for l in range(N_LAYERS_A):
        qkvz, ba = _inproj(x, w['norm_mix'][l], w['w_in_a'][l])
        qkv_seq = qkvz.reshape(n, seq_len, QKVZ_W_A)[:, :, :QKV_W_A]
        conv_new.append(jnp.concatenate([conv0[l], qkv_seq], axis=1)[:, seq_len:])
        if lp != seq_len:
            pad = lambda a: jnp.pad(a.reshape(n, seq_len, -1), ((0, 0), (0, lp - seq_len), (0, 0))).reshape(n * lp, -1)
            qkvz, ba = pad(qkvz), pad(ba)
        o, s_new = _gdn(qkvz, ba, n, seq_len, chunk, rec0[l], conv0[l], w['conv_a'][l], w['a_log'][l],
                        w['dt_bias'][l], w['norm_o_a'][l])
        if lp != seq_len:
            o = o.reshape(n, lp, VAL_W_A)[:, :seq_len].reshape(n * seq_len, VAL_W_A)
        rec_new.append(s_new)
        x = _mlp_block(x, [o], w['w_out_a'][l], w['norm_mlp'][l], w['w_up'][l], w['w_down'][l],
                       w['norm_final'], False)

    kv = _norm_matmul_split(x, w['norm_kv'], w['w_kv'], KV_W_BRANCH)
    bufs_new = []
    for b in range(N_BRANCH):
        new = kv[b].reshape(n, seq_len, 2, KV_HEADS_B, HEAD_DIM_B)
        ext = new if kv_bufs is None else jnp.concatenate([kv_bufs[b], new], axis=1)
        keep = min(BRANCH_WINDOWS[b], ext.shape[1])
        bufs_new.append(ext[:, ext.shape[1] - keep:])

    for l in range(N_LAYERS_A, N_LAYERS_A + N_LAYERS_B):
        lb = l - N_LAYERS_A
        q = _norm_matmul_split(x, w['norm_mix'][l], w['w_q_b'][lb], Q_W_BRANCH)
        if kv_bufs is None:
            parts = [_attn_prompt_branch(q[b], kv[b], n, seq_len, BRANCH_DILATIONS[b]) for b in range(N_BRANCH)]
            mix = [p[0] for p in parts] + [p[1] for p in parts]
        else:
            mix = [_attn_sample(q, kv, kv_bufs)]
        x = _mlp_block(x, mix, w['w_o_b'][lb], w['norm_mlp'][l], w['w_up'][l], w['w_down'][l],
                       w['norm_final'], l == N_LAYERS_A + N_LAYERS_B - 1)
    return (x.reshape(n, seq_len, d), jnp.stack(rec_new), jnp.stack(conv_new), *bufs_new)


def kernel(x_prompt, x_sample, state_a_rec, state_a_conv, cache_b0_kv, cache_b1_kv, cache_b2_kv, norm_mix, norm_mlp, w_in_a, conv_a, a_log, dt_bias, norm_o_a, w_out_a, norm_kv, w_kv, w_q_b, w_o_b, w_up, w_down, norm_final):
    w = {'norm_mix': norm_mix, 'norm_mlp': norm_mlp, 'w_in_a': w_in_a, 'conv_a': conv_a, 'a_log': a_log,
         'dt_bias': dt_bias, 'norm_o_a': norm_o_a, 'w_out_a': w_out_a, 'norm_kv': norm_kv, 'w_kv': w_kv,
         'w_q_b': w_q_b, 'w_o_b': w_o_b, 'w_up': w_up, 'w_down': w_down, 'norm_final': norm_final}
    n_p = x_prompt.shape[0]
    p_rec0 = jnp.zeros((N_LAYERS_A, n_p, HEADS_A, DK_A, DV_A), state_a_rec.dtype)
    p_conv0 = jnp.zeros((N_LAYERS_A, n_p, CONV_W - 1, QKV_W_A), state_a_conv.dtype)
    prompt = _trunk(x_prompt, p_rec0, p_conv0, None, w)
    sample = _trunk(x_sample, state_a_rec, state_a_conv, [cache_b0_kv, cache_b1_kv, cache_b2_kv], w)
    return (prompt[0], sample[0], *prompt[1:], *sample[1:])
```

```python
import functools

import jax
import jax.numpy as jnp
from jax import lax
from jax.experimental import pallas as pl
from jax.experimental.pallas import tpu as pltpu

D_MODEL = 1024
HEADS_A = 8
DK_A = 128
DV_A = 128
KEY_W_A = HEADS_A * DK_A
VAL_W_A = HEADS_A * DV_A
QKV_W_A = 2 * KEY_W_A + VAL_W_A
QKVZ_W_A = QKV_W_A + VAL_W_A
CONV_W = 4
N_LAYERS_A = 2
N_LAYERS_B = 2
BRANCH_WINDOWS = (128, 512, 2048)
BRANCH_DILATIONS = (1, 4, 16)
N_BRANCH = 3
HEADS_B = 8
KV_HEADS_B = 2
GQA_B = HEADS_B // KV_HEADS_B
HEAD_DIM_B = 64
Q_W_BRANCH = HEADS_B * HEAD_DIM_B
KV_W_BRANCH = 2 * KV_HEADS_B * HEAD_DIM_B
ALIBI_SLOPES = tuple(2.0 ** (-8.0 * h / HEADS_B) for h in range(1, HEADS_B + 1))
EPS = 1e-6

LANES = 128
SUBLANES = 8
VMEM_LIMIT_BYTES = 56 * 1024 * 1024
GDN_CHUNK = 128
ATT_BLOCK = 128

F32 = jnp.float32
BF16 = jnp.bfloat16


def _params(*sem):
    return pltpu.CompilerParams(dimension_semantics=sem, vmem_limit_bytes=VMEM_LIMIT_BYTES)


def _rms(x, gain):
    return x * lax.rsqrt(jnp.mean(x * x, axis=-1, keepdims=True) + EPS) * gain


def _sigmoid(x):
    return 1.0 / (1.0 + jnp.exp(-x))


def _softplus(x):
    return jnp.maximum(x, 0.0) + jnp.log(1.0 + jnp.exp(-jnp.abs(x)))


def _dot(a, b):
    return jnp.dot(a.astype(BF16), b.astype(BF16), preferred_element_type=F32)


def _dot_nt(a, b):
    return lax.dot_general(a.astype(BF16), b.astype(BF16), (((1,), (1,)), ((), ())),
                           preferred_element_type=F32)


def _dot_tn(a, b):
    return lax.dot_general(a.astype(BF16), b.astype(BF16), (((0,), (0,)), ((), ())),
                           preferred_element_type=F32)


def _dot_f32(a, b):
    return jnp.dot(a, b, precision=lax.Precision.HIGHEST, preferred_element_type=F32)


def _norm_matmul_kernel(x_ref, g_ref, w_ref, o_ref, xn_ref):
    @pl.when(pl.program_id(1) == 0)
    def _():
        xn_ref[...] = _rms(x_ref[...], g_ref[...]).astype(BF16)

    o_ref[...] = jnp.dot(xn_ref[...], w_ref[...], preferred_element_type=F32).astype(o_ref.dtype)


def _norm_matmul(x, gain, w, tn, split):
    t, d = x.shape
    f = w.shape[1]
    tm = min(512, t)
    nj = f // tn
    if split:
        out_spec = pl.BlockSpec((None, tm, tn), lambda i, j: (j, i, 0))
        out_shape = jax.ShapeDtypeStruct((nj, t, tn), F32)
    else:
        out_spec = pl.BlockSpec((tm, tn), lambda i, j: (i, j))
        out_shape = jax.ShapeDtypeStruct((t, f), F32)
    return pl.pallas_call(
        _norm_matmul_kernel,
        grid=(t // tm, nj),
        in_specs=[pl.BlockSpec((tm, d), lambda i, j: (i, 0)),
                  pl.BlockSpec((1, d), lambda i, j: (0, 0)),
                  pl.BlockSpec((d, tn), lambda i, j: (0, j))],
        out_specs=out_spec,
        out_shape=out_shape,
        scratch_shapes=[pltpu.VMEM((tm, d), BF16)],
        compiler_params=_params("parallel", "arbitrary"),
        name="norm_matmul",
    )(x, gain.reshape(1, d), w.astype(BF16))


def _inproj_kernel(x_ref, g_ref, w_ref, wg_ref, o_ref, og_ref, xn_ref):
    @pl.when(pl.program_id(1) == 0)
    def _():
        xn = _rms(x_ref[...], g_ref[...]).astype(BF16)
        xn_ref[...] = xn
        og_ref[...] = jnp.dot(xn, wg_ref[...], preferred_element_type=F32)

    o_ref[...] = jnp.dot(xn_ref[...], w_ref[...], preferred_element_type=F32)


def _inproj(x, gain, w_in):
    t, d = x.shape
    tm = min(512, t)
    tn = 1024
    w_main = w_in[:, :QKVZ_W_A].astype(BF16)
    w_gate = jnp.pad(w_in[:, QKVZ_W_A:], ((0, 0), (0, LANES - 2 * HEADS_A))).astype(BF16)
    return pl.pallas_call(
        _inproj_kernel,
        grid=(t // tm, QKVZ_W_A // tn),
        in_specs=[pl.BlockSpec((tm, d), lambda i, j: (i, 0)),
                  pl.BlockSpec((1, d), lambda i, j: (0, 0)),
                  pl.BlockSpec((d, tn), lambda i, j: (0, j)),
                  pl.BlockSpec((d, LANES), lambda i, j: (0, 0))],
        out_specs=[pl.BlockSpec((tm, tn), lambda i, j: (i, j)),
                   pl.BlockSpec((tm, LANES), lambda i, j: (i, 0))],
        out_shape=[jax.ShapeDtypeStruct((t, QKVZ_W_A), F32),
                   jax.ShapeDtypeStruct((t, LANES), F32)],
        scratch_shapes=[pltpu.VMEM((tm, d), BF16)],
        compiler_params=_params("parallel", "arbitrary"),
        name="gdn_inproj",
    )(x, gain.reshape(1, d), w_main, w_gate)


def _mlp_kernel(*refs, n_mix, final_norm):
    x_ref = refs[0]
    mix_refs = refs[1:1 + n_mix]
    wo_ref, g_ref, wup_ref, wdn_ref, gf_ref, y_ref, x1_ref, xn_ref, acc_ref = refs[1 + n_mix:]
    f = pl.program_id(1)

    @pl.when(f == 0)
    def _():
        if n_mix == 1:
            mixed = mix_refs[0][...]
        else:
            outs = [r[...] for r in mix_refs[:N_BRANCH]]
            lses = [r[...] for r in mix_refs[N_BRANCH:]]
            m = functools.reduce(jnp.maximum, lses)
            ws = [jnp.exp(l - m) for l in lses]
            mixed = sum(w * o for w, o in zip(ws, outs)) / sum(ws)
        x1 = x_ref[...] + _dot(mixed, wo_ref[...])
        x1_ref[...] = x1
        xn_ref[...] = _rms(x1, g_ref[...]).astype(BF16)
        acc_ref[...] = jnp.zeros_like(acc_ref)

    u = jnp.maximum(jnp.dot(xn_ref[...], wup_ref[...], preferred_element_type=F32), 0.0)
    acc_ref[...] += _dot(u * u, wdn_ref[...])

    @pl.when(f == pl.num_programs(1) - 1)
    def _():
        y = x1_ref[...] + acc_ref[...]
        if final_norm:
            y = _rms(y, gf_ref[...])
        y_ref[...] = y


def _mlp_block(x, mix, w_o, gain, w_up, w_down, gain_final, final_norm):
    t, d = x.shape
    k = w_o.shape[0]
    dff = w_up.shape[1]
    tm = min(512, t)
    tf = 512
    n_mix = len(mix)
    row = lambda i, f: (i, 0)
    const = lambda i, f: (0, 0)
    return pl.pallas_call(
        functools.partial(_mlp_kernel, n_mix=n_mix, final_norm=final_norm),
        grid=(t // tm, dff // tf),
        in_specs=([pl.BlockSpec((tm, d), row)]
                  + [pl.BlockSpec((tm, k), row)] * n_mix
                  + [pl.BlockSpec((k, d), const),
                     pl.BlockSpec((1, d), const),
                     pl.BlockSpec((d, tf), lambda i, f: (0, f)),
                     pl.BlockSpec((tf, d), lambda i, f: (f, 0)),
                     pl.BlockSpec((1, d), const)]),
        out_specs=pl.BlockSpec((tm, d), row),
        out_shape=jax.ShapeDtypeStruct((t, d), F32),
        scratch_shapes=[pltpu.VMEM((tm, d), F32), pltpu.VMEM((tm, d), BF16), pltpu.VMEM((tm, d), F32)],
        compiler_params=_params("parallel", "arbitrary"),
        name="mix_out_mlp",
    )(x, *mix, w_o.astype(BF16), gain.reshape(1, d), w_up.astype(BF16), w_down.astype(BF16),
      gain_final.reshape(1, d))


def _inverse_masks(c):
    row = lax.broadcasted_iota(jnp.int32, (c, c), 0)
    col = lax.broadcasted_iota(jnp.int32, (c, c), 1)
    masks = []
    shift = 0
    while (1 << shift) < c:
        same_pair = (row >> (shift + 1)) == (col >> (shift + 1))
        other_half = (row >> shift) != (col >> shift)
        masks.append((row > col) & same_pair & other_half)
        shift += 1
    return masks


def _unit_lower_inverses(mats, c):
    masks = _inverse_masks(c)
    row = lax.broadcasted_iota(jnp.int32, (c, c), 0)
    col = lax.broadcasted_iota(jnp.int32, (c, c), 1)
    eye = (row == col).astype(F32)
    invs = [eye - jnp.where(masks[0], a, 0.0) for a in mats]
    for mask in masks[1:]:
        xs = [_dot(inv, jnp.where(mask, a, 0.0)) for inv, a in zip(invs, mats)]
        invs = [inv - _dot(x, inv) for inv, x in zip(invs, xs)]
    return invs


def _gdn_kernel(qkv_ref, z_ref, ba_ref, bat_ref, cw_ref, s0_ref, c0_ref, alr_ref, dtr_ref, alc_ref, dtc_ref,
                gn_ref, o_ref, s_ref, ext_ref, *, chunk):
    c = chunk
    ci = pl.program_id(1)
    halo = SUBLANES

    @pl.when(ci == 0)
    def _():
        s_ref[...] = s0_ref[...]
        ext_ref[pl.ds(0, halo), :] = c0_ref[...]

    ext_ref[pl.ds(halo, c), :] = qkv_ref[...]
    conv = ext_ref[pl.ds(halo - (CONV_W - 1), c), :] * cw_ref[0:1, :]
    for j in range(1, CONV_W):
        conv = conv + ext_ref[pl.ds(halo - (CONV_W - 1) + j, c), :] * cw_ref[j:j + 1, :]
    ext_ref[pl.ds(0, halo), :] = ext_ref[pl.ds(c, halo), :]
    act = conv * _sigmoid(conv)

    ba = ba_ref[...]
    bat = bat_ref[...]
    beta_cols = _sigmoid(ba)
    g_cols = -jnp.exp(alr_ref[...]) * _softplus(ba + dtr_ref[...])
    g_rows = -jnp.exp(alc_ref[...]) * _softplus(bat + dtc_ref[...])
    row = lax.broadcasted_iota(jnp.int32, (c, c), 0)
    col = lax.broadcasted_iota(jnp.int32, (c, c), 1)
    causal = row >= col
    strict = row > col
    cum_cols = _dot_f32(causal.astype(F32), g_cols)
    cum_rows = _dot_f32(g_rows, (row <= col).astype(F32))

    heads = range(HEADS_A)
    qs, ks, vs, betas, gcs, decays = [], [], [], [], [], []
    for h in heads:
        lo = h * DK_A
        q = act[:, lo:lo + DK_A]
        k = act[:, KEY_W_A + lo:KEY_W_A + lo + DK_A]
        qs.append(q * lax.rsqrt(jnp.sum(q * q, axis=-1, keepdims=True) + EPS) * (DK_A ** -0.5))
        ks.append(k * lax.rsqrt(jnp.sum(k * k, axis=-1, keepdims=True) + EPS))
        vs.append(act[:, 2 * KEY_W_A + lo:2 * KEY_W_A + lo + DV_A])
        betas.append(beta_cols[:, h:h + 1])
        gc = cum_cols[:, HEADS_A + h:HEADS_A + h + 1]
        gr = cum_rows[HEADS_A + h:HEADS_A + h + 1, :]
        gcs.append(gc)
        decays.append(jnp.where(causal, jnp.exp(jnp.where(causal, gc - gr, 0.0)), 0.0))
    kbs = [k * b for k, b in zip(ks, betas)]
    kq = [_dot_nt(jnp.concatenate([kb, q], axis=0), k) for kb, q, k in zip(kbs, qs, ks)]
    mats = [jnp.where(strict, x[:c] * d, 0.0) for x, d in zip(kq, decays)]
    attns = [x[c:] * d for x, d in zip(kq, decays)]
    t_invs = _unit_lower_inverses(mats, c)
    e_gcs = [jnp.exp(gc) for gc in gcs]
    uws = [_dot(t, jnp.concatenate([v * b, kb * e], axis=1))
           for t, v, b, kb, e in zip(t_invs, vs, betas, kbs, e_gcs)]
    states = [s_ref[h] for h in heads]
    wq_s = [_dot(jnp.concatenate([uw[:, DV_A:], q * e], axis=0), s)
            for uw, q, e, s in zip(uws, qs, e_gcs, states)]
    v_news = [uw[:, :DV_A] - x[:c] for uw, x in zip(uws, wq_s)]
    outs = [x[c:] + _dot(attn, vn) for x, attn, vn in zip(wq_s, attns, v_news)]
    for h in heads:
        g_last = gcs[h][c - 1:c, :]
        s_ref[h] = states[h] * jnp.exp(g_last) + _dot_tn(ks[h] * jnp.exp(g_last - gcs[h]), v_news[h])
    for h in heads:
        lo = h * DV_A
        o = _rms(outs[h], gn_ref[...])
        z = z_ref[:, lo:lo + DV_A]
        o_ref[:, lo:lo + DV_A] = (o * z * _sigmoid(z)).astype(o_ref.dtype)


def _gdn(qkvz, ba, n, seq_len, chunk, s0, conv0, w_conv, a_log, dt_bias, g_norm):
    nc = seq_len // chunk
    t = n * seq_len
    halo = SUBLANES
    bat = ba[:, :2 * HEADS_A].reshape(n * nc, chunk, 2 * HEADS_A).transpose(0, 2, 1)
    c0 = jnp.pad(conv0, ((0, 0), (halo - (CONV_W - 1), 0), (0, 0)))
    pad_row = lambda p: jnp.pad(p.reshape(1, HEADS_A), ((0, 0), (HEADS_A, LANES - 2 * HEADS_A)))
    pad_col = lambda p: jnp.pad(p.reshape(HEADS_A, 1), ((HEADS_A, 0), (0, 0)))
    blk = lambda i, j: (i * nc + j, 0)
    const = lambda i, j: (0, 0)
    return pl.pallas_call(
        functools.partial(_gdn_kernel, chunk=chunk),
        grid=(n, nc),
        in_specs=[pl.BlockSpec((chunk, QKV_W_A), blk),
                  pl.BlockSpec((chunk, VAL_W_A), lambda i, j: (i * nc + j, QKV_W_A // VAL_W_A)),
                  pl.BlockSpec((chunk, LANES), blk),
                  pl.BlockSpec((None, 2 * HEADS_A, chunk), lambda i, j: (i * nc + j, 0, 0)),
                  pl.BlockSpec((CONV_W, QKV_W_A), const),
                  pl.BlockSpec((None, HEADS_A, DK_A, DV_A), lambda i, j: (i, 0, 0, 0)),
                  pl.BlockSpec((None, halo, QKV_W_A), lambda i, j: (i, 0, 0)),
                  pl.BlockSpec((1, LANES), const),
                  pl.BlockSpec((1, LANES), const),
                  pl.BlockSpec((2 * HEADS_A, 1), const),
                  pl.BlockSpec((2 * HEADS_A, 1), const),
                  pl.BlockSpec((1, DV_A), const)],
        out_specs=[pl.BlockSpec((chunk, VAL_W_A), blk),
                   pl.BlockSpec((None, HEADS_A, DK_A, DV_A), lambda i, j: (i, 0, 0, 0))],
        out_shape=[jax.ShapeDtypeStruct((t, VAL_W_A), BF16),
                   jax.ShapeDtypeStruct((n, HEADS_A, DK_A, DV_A), F32)],
        scratch_shapes=[pltpu.VMEM((halo + chunk, QKV_W_A), F32)],
        compiler_params=_params("parallel", "arbitrary"),
        name="gdn_delta_rule",
    )(qkvz, qkvz, ba, bat, w_conv, s0, c0, pad_row(a_log), pad_row(dt_bias), pad_col(a_log), pad_col(dt_bias),
      g_norm.reshape(1, DV_A))


def _gdn_step_kernel(qkv_ref, z_ref, ba_ref, st_ref, cw_ref, s0_ref, alr_ref, dtr_ref, gn_ref, o_ref, s_ref):
    nb = qkv_ref.shape[0]
    conv = qkv_ref[...] * cw_ref[CONV_W - 1:CONV_W, :]
    for j in range(CONV_W - 1):
        conv = conv + st_ref[j] * cw_ref[j:j + 1, :]
    act = conv * _sigmoid(conv)
    ba = ba_ref[...]
    beta_cols = _sigmoid(ba)
    decay_cols = jnp.exp(-jnp.exp(alr_ref[...]) * _softplus(ba + dtr_ref[...]))
    fill = jnp.zeros((DK_A - nb, DK_A), F32)
    for h in range(HEADS_A):
        lo = h * DK_A
        q = act[:, lo:lo + DK_A]
        k = act[:, KEY_W_A + lo:KEY_W_A + lo + DK_A]
        v = act[:, 2 * KEY_W_A + lo:2 * KEY_W_A + lo + DV_A]
        q = q * lax.rsqrt(jnp.sum(q * q, axis=-1, keepdims=True) + EPS) * (DK_A ** -0.5)
        k = k * lax.rsqrt(jnp.sum(k * k, axis=-1, keepdims=True) + EPS)
        q_t = jnp.concatenate([q, fill], axis=0).T
        k_t = jnp.concatenate([k, fill], axis=0).T
        z = z_ref[:, lo:lo + DV_A]
        gate = z * _sigmoid(z)
        seqs = range(nb)
        k_cols = [k_t[:, i:i + 1] for i in seqs]
        decays = [decay_cols[i:i + 1, HEADS_A + h:HEADS_A + h + 1] for i in seqs]
        k_s = [jnp.sum(k_cols[i] * s0_ref[i, h], axis=0, keepdims=True) for i in seqs]
        v_new = [beta_cols[i:i + 1, h:h + 1] * (v[i:i + 1, :] - decays[i] * k_s[i]) for i in seqs]
        s_new = [decays[i] * s0_ref[i, h] + k_cols[i] * v_new[i] for i in seqs]
        for i in seqs:
            s_ref[i, h] = s_new[i]
        outs = [jnp.sum(q_t[:, i:i + 1] * s_new[i], axis=0, keepdims=True) for i in seqs]
        for i in seqs:
            o_ref[i:i + 1, lo:lo + DV_A] = _rms(outs[i], gn_ref[...]) * gate[i:i + 1, :]


def _gdn_step(qkvz, ba, s0, conv0, w_conv, a_log, dt_bias, g_norm):
    n = qkvz.shape[0]
    nb = SUBLANES
    pad_row = lambda p: jnp.pad(p.reshape(1, HEADS_A), ((0, 0), (HEADS_A, LANES - 2 * HEADS_A)))
    blk = lambda i: (i, 0)
    const = lambda i: (0, 0)
    return pl.pallas_call(
        _gdn_step_kernel,
        grid=(n // nb,),
        in_specs=[pl.BlockSpec((nb, QKV_W_A), blk),
                  pl.BlockSpec((nb, VAL_W_A), lambda i: (i, QKV_W_A // VAL_W_A)),
                  pl.BlockSpec((nb, LANES), blk),
                  pl.BlockSpec((CONV_W - 1, nb, QKV_W_A), lambda i: (0, i, 0)),
                  pl.BlockSpec((CONV_W, QKV_W_A), const),
                  pl.BlockSpec((nb, HEADS_A, DK_A, DV_A), lambda i: (i, 0, 0, 0)),
                  pl.BlockSpec((1, LANES), const),
                  pl.BlockSpec((1, LANES), const),
                  pl.BlockSpec((1, DV_A), const)],
        out_specs=[pl.BlockSpec((nb, VAL_W_A), blk),
                   pl.BlockSpec((nb, HEADS_A, DK_A, DV_A), lambda i: (i, 0, 0, 0))],
        out_shape=[jax.ShapeDtypeStruct((n, VAL_W_A), F32),
                   jax.ShapeDtypeStruct((n, HEADS_A, DK_A, DV_A), F32)],
        compiler_params=_params("parallel"),
        name="gdn_step",
    )(qkvz, qkvz, ba, conv0.transpose(1, 0, 2), w_conv, s0, pad_row(a_log), pad_row(dt_bias),
      g_norm.reshape(1, DV_A))


def _attn_prompt_kernel(q_ref, kvc_ref, kvp_ref, o_ref, lse_ref, *, dilation):
    blk = ATT_BLOCK
    a = pl.program_id(2)
    qi = lax.broadcasted_iota(jnp.int32, (blk, 2 * blk), 0)
    kj = lax.broadcasted_iota(jnp.int32, (blk, 2 * blk), 1)
    steps = blk + qi - kj
    valid = (steps >= 0) & (steps <= blk) & ((kj >= blk) | (a > 0))
    dist = (steps * dilation).astype(F32)
    kvc = kvc_ref[...]
    kvp = kvp_ref[...]
    scale = HEAD_DIM_B ** -0.5
    for kvh in range(KV_HEADS_B):
        klo = kvh * HEAD_DIM_B
        vlo = (KV_HEADS_B + kvh) * HEAD_DIM_B
        k_cat = jnp.concatenate([kvp[:, klo:klo + HEAD_DIM_B], kvc[:, klo:klo + HEAD_DIM_B]], axis=0)
        v_cat = jnp.concatenate([kvp[:, vlo:vlo + HEAD_DIM_B], kvc[:, vlo:vlo + HEAD_DIM_B]], axis=0)
        for g in range(GQA_B):
            h = kvh * GQA_B + g
            lo = h * HEAD_DIM_B
            s = _dot_nt(q_ref[:, lo:lo + HEAD_DIM_B] * scale, k_cat) - ALIBI_SLOPES[h] * dist
            s = jnp.where(valid, s, -jnp.inf)
            m = jnp.max(s, axis=-1, keepdims=True)
            p = jnp.exp(s - m)
            l = jnp.sum(p, axis=-1, keepdims=True)
            o_ref[:, lo:lo + HEAD_DIM_B] = _dot(p, v_cat) / l
            lse_ref[:, lo:lo + HEAD_DIM_B] = jnp.broadcast_to(m + jnp.log(l), (blk, HEAD_DIM_B))


def _attn_prompt_branch(q, kv, n, seq_len, dilation):
    ls = seq_len // dilation
    nq = ls // ATT_BLOCK
    qv = q.reshape(n, ls, dilation * Q_W_BRANCH)
    kvv = kv.reshape(n, ls, dilation * KV_W_BRANCH)
    cur = lambda i, r, a: (i, a, r)
    prev = lambda i, r, a: (i, jnp.maximum(a - 1, 0), r)
    o, lse = pl.pallas_call(
        functools.partial(_attn_prompt_kernel, dilation=dilation),
        grid=(n, dilation, nq),
        in_specs=[pl.BlockSpec((None, ATT_BLOCK, Q_W_BRANCH), cur),
                  pl.BlockSpec((None, ATT_BLOCK, KV_W_BRANCH), cur),
                  pl.BlockSpec((None, ATT_BLOCK, KV_W_BRANCH), prev)],
        out_specs=[pl.BlockSpec((None, ATT_BLOCK, Q_W_BRANCH), cur)] * 2,
        out_shape=[jax.ShapeDtypeStruct(qv.shape, F32)] * 2,
        compiler_params=_params("parallel", "parallel", "arbitrary"),
        name=f"attn_prompt_d{dilation}",
    )(qv, kvv, kvv)
    return o.reshape(n * seq_len, Q_W_BRANCH), lse.reshape(n * seq_len, Q_W_BRANCH)


def _attn_sample_kernel(q_ref, kvn_ref, c0_ref, c1_ref, c2_ref, o_ref):
    nk = ATT_BLOCK
    nb = q_ref.shape[0]
    kv_w = KV_HEADS_B * HEAD_DIM_B
    head = lax.broadcasted_iota(jnp.int32, (HEADS_B, 1), 0)
    slopes = jnp.exp2(-8.0 * (head + 1).astype(F32) / HEADS_B)
    key = lax.broadcasted_iota(jnp.int32, (1, nk), 1)
    lane = lax.broadcasted_iota(jnp.int32, (HEADS_B, kv_w), 1)
    own_half = (lane // HEAD_DIM_B) == (head // GQA_B)
    scale = HEAD_DIM_B ** -0.5
    c_refs = (c0_ref, c1_ref, c2_ref)
    pairs = [(i, b) for i in range(nb) for b in range(N_BRANCH)]
    dists = [((nk - key) * d).astype(F32) for d in BRANCH_DILATIONS]
    qs = [q_ref[i, b * HEADS_B:(b + 1) * HEADS_B, :] * scale for i, b in pairs]
    s_old = [_dot_nt(q, c_refs[b][i, :, :kv_w]) - slopes * dists[b] for q, (i, b) in zip(qs, pairs)]
    s_new = [jnp.sum(q * kvn_ref[b, i:i + 1, :kv_w], axis=-1, keepdims=True) for q, (i, b) in zip(qs, pairs)]
    ms = [jnp.maximum(jnp.max(so, axis=-1, keepdims=True), sn) for so, sn in zip(s_old, s_new)]
    p_old = [jnp.exp(so - m) for so, m in zip(s_old, ms)]
    p_new = [jnp.exp(sn - m) for sn, m in zip(s_new, ms)]
    ls = [jnp.sum(po, axis=-1, keepdims=True) + pn for po, pn in zip(p_old, p_new)]
    outs = [(_dot(po, c_refs[b][i, :, kv_w:]) + pn * kvn_ref[b, i:i + 1, kv_w:]) / l
            for po, pn, l, (i, b) in zip(p_old, p_new, ls, pairs)]
    lses = [m + jnp.log(l) for m, l in zip(ms, ls)]
    for i in range(nb):
        sl = slice(i * N_BRANCH, (i + 1) * N_BRANCH)
        m = functools.reduce(jnp.maximum, lses[sl])
        ws = [jnp.exp(l - m) for l in lses[sl]]
        merged = sum(w * o for w, o in zip(ws, outs[sl])) / sum(ws)
        o_ref[i] = jnp.where(own_half, merged, 0.0)


def _spread_heads(w, axis):
    w = jnp.moveaxis(w, axis, -1)
    lead = w.shape[:-1]
    w = w.reshape(lead + (-1, KV_HEADS_B, GQA_B, HEAD_DIM_B))
    halves = [jnp.pad(w[..., g, :, :], [(0, 0)] * (len(lead) + 2) + [(g * HEAD_DIM_B, (KV_HEADS_B - 1 - g) * HEAD_DIM_B)])
              for g in range(KV_HEADS_B)]
    out = jnp.stack(halves, axis=-3)
    return jnp.moveaxis(out.reshape(lead + (-1,)), -1, axis)


def _attn_sample(q, kv_new, caches):
    n = q.shape[0]
    nb = SUBLANES
    rows = N_BRANCH * HEADS_B
    views = [c.reshape(n, BRANCH_WINDOWS[b] // BRANCH_DILATIONS[b], BRANCH_DILATIONS[b] * KV_W_BRANCH)
             for b, c in enumerate(caches)]
    out = pl.pallas_call(
        _attn_sample_kernel,
        grid=(n // nb,),
        in_specs=[pl.BlockSpec((nb, rows, LANES), lambda i: (i, 0, 0)),
                  pl.BlockSpec((N_BRANCH, nb, KV_W_BRANCH), lambda i: (0, i, 0))]
                 + [pl.BlockSpec((nb, ATT_BLOCK, KV_W_BRANCH), lambda i: (i, 0, 0))] * N_BRANCH,
        out_specs=pl.BlockSpec((nb, HEADS_B, LANES), lambda i: (i, 0, 0)),
        out_shape=jax.ShapeDtypeStruct((n, HEADS_B, LANES), F32),
        compiler_params=_params("parallel"),
        name="attn_sample",
    )(q.reshape(n, rows, LANES), kv_new, *views)
    return out.reshape(n, HEADS_B * LANES)


def _trunk(x, rec0, conv0, kv_bufs, w):
    n, seq_len, d = x.shape
    fresh = kv_bufs is None
    assert seq_len == 1 or (fresh and seq_len % (ATT_BLOCK * max(BRANCH_DILATIONS)) == 0)
    x = x.reshape(n * seq_len, d)
    rec_new, conv_new = [], []
    for l in range(N_LAYERS_A):
        qkvz, ba = _inproj(x, w['norm_mix'][l], w['w_in_a'][l])
        qkv_seq = qkvz.reshape(n, seq_len, QKVZ_W_A)[:, :, :QKV_W_A]
        conv_new.append(jnp.concatenate([conv0[l], qkv_seq], axis=1)[:, seq_len:])
        if fresh:
            o, s_new = _gdn(qkvz, ba, n, seq_len, GDN_CHUNK, rec0[l], conv0[l], w['conv_a'][l], w['a_log'][l],
                            w['dt_bias'][l], w['norm_o_a'][l])
        else:
            o, s_new = _gdn_step(qkvz, ba, rec0[l], conv0[l], w['conv_a'][l], w['a_log'][l], w['dt_bias'][l],
                                 w['norm_o_a'][l])
        rec_new.append(s_new)
        x = _mlp_block(x, [o], w['w_out_a'][l], w['norm_mlp'][l], w['w_up'][l], w['w_down'][l],
                       w['norm_final'], False)

    kv = _norm_matmul(x, w['norm_kv'], w['w_kv'], KV_W_BRANCH, True)
    bufs_new = []
    for b in range(N_BRANCH):
        new = kv[b].reshape(n, seq_len, 2, KV_HEADS_B, HEAD_DIM_B)
        ext = new if fresh else jnp.concatenate([kv_bufs[b], new], axis=1)
        keep = min(BRANCH_WINDOWS[b], ext.shape[1])
        bufs_new.append(ext[:, ext.shape[1] - keep:])

    for l in range(N_LAYERS_A, N_LAYERS_A + N_LAYERS_B):
        lb = l - N_LAYERS_A
        if fresh:
            q = _norm_matmul(x, w['norm_mix'][l], w['w_q_b'][lb], Q_W_BRANCH, True)
            parts = [_attn_prompt_branch(q[b], kv[b], n, seq_len, BRANCH_DILATIONS[b]) for b in range(N_BRANCH)]
            mix = [p[0] for p in parts] + [p[1] for p in parts]
            w_o = w['w_o_b'][lb]
        else:
            q = _norm_matmul(x, w['norm_mix'][l], _spread_heads(w['w_q_b'][lb], 1), 1024, False)
            mix = [_attn_sample(q, kv, kv_bufs)]
            w_o = _spread_heads(w['w_o_b'][lb], 0)
        x = _mlp_block(x, mix, w_o, w['norm_mlp'][l], w['w_up'][l], w['w_down'][l],
                       w['norm_final'], l == N_LAYERS_A + N_LAYERS_B - 1)
    return (x.reshape(n, seq_len, d), jnp.stack(rec_new), jnp.stack(conv_new), *bufs_new)


def kernel(x_prompt, x_sample, state_a_rec, state_a_conv, cache_b0_kv, cache_b1_kv, cache_b2_kv, norm_mix, norm_mlp, w_in_a, conv_a, a_log, dt_bias, norm_o_a, w_out_a, norm_kv, w_kv, w_q_b, w_o_b, w_up, w_down, norm_final):
    w = {'norm_mix': norm_mix, 'norm_mlp': norm_mlp, 'w_in_a': w_in_a, 'conv_a': conv_a, 'a_log': a_log,
         'dt_bias': dt_bias, 'norm_o_a': norm_o_a, 'w_out_a': w_out_a, 'norm_kv': norm_kv, 'w_kv': w_kv,
         'w_q_b': w_q_b, 'w_o_b': w_o_b, 'w_up': w_up, 'w_down': w_down, 'norm_final': norm_final}
    n_p = x_prompt.shape[0]
    p_rec0 = jnp.zeros((N_LAYERS_A, n_p, HEADS_A, DK_A, DV_A), state_a_rec.dtype)
    p_conv0 = jnp.zeros((N_LAYERS_A, n_p, CONV_W - 1, QKV_W_A), state_a_conv.dtype)
    prompt = _trunk(x_prompt, p_rec0, p_conv0, None, w)
    sample = _trunk(x_sample, state_a_rec, state_a_conv, [cache_b0_kv, cache_b1_kv, cache_b2_kv], w)
    return (prompt[0], sample[0], *prompt[1:], *sample[1:])
```

```python
import functools

import jax
import jax.numpy as jnp
from jax import lax
from jax.experimental import pallas as pl
from jax.experimental.pallas import tpu as pltpu

D_MODEL = 1024
HEADS_A = 8
DK_A = 128
DV_A = 128
KEY_W_A = HEADS_A * DK_A
VAL_W_A = HEADS_A * DV_A
QKV_W_A = 2 * KEY_W_A + VAL_W_A
QKVZ_W_A = QKV_W_A + VAL_W_A
CONV_W = 4
N_LAYERS_A = 2
N_LAYERS_B = 2
BRANCH_WINDOWS = (128, 512, 2048)
BRANCH_DILATIONS = (1, 4, 16)
N_BRANCH = 3
HEADS_B = 8
KV_HEADS_B = 2
GQA_B = HEADS_B // KV_HEADS_B
HEAD_DIM_B = 64
Q_W_BRANCH = HEADS_B * HEAD_DIM_B
KV_W_BRANCH = 2 * KV_HEADS_B * HEAD_DIM_B
ALIBI_SLOPES = tuple(2.0 ** (-8.0 * h / HEADS_B) for h in range(1, HEADS_B + 1))
EPS = 1e-6

LANES = 128
SUBLANES = 8
VMEM_LIMIT_BYTES = 56 * 1024 * 1024
GDN_CHUNK = 128
ATT_BLOCK = 128

F32 = jnp.float32
BF16 = jnp.bfloat16


def _params(*sem):
    return pltpu.CompilerParams(dimension_semantics=sem, vmem_limit_bytes=VMEM_LIMIT_BYTES)


def _rms(x, gain):
    return x * lax.rsqrt(jnp.mean(x * x, axis=-1, keepdims=True) + EPS) * gain


def _sigmoid(x):
    return 1.0 / (1.0 + jnp.exp(-x))


def _softplus(x):
    return jnp.maximum(x, 0.0) + jnp.log(1.0 + jnp.exp(-jnp.abs(x)))


def _dot(a, b):
    return jnp.dot(a.astype(BF16), b.astype(BF16), preferred_element_type=F32)


def _dot_nt(a, b):
    return lax.dot_general(a.astype(BF16), b.astype(BF16), (((1,), (1,)), ((), ())),
                           preferred_element_type=F32)


def _dot_tn(a, b):
    return lax.dot_general(a.astype(BF16), b.astype(BF16), (((0,), (0,)), ((), ())),
                           preferred_element_type=F32)


def _dot_f32(a, b):
    return jnp.dot(a, b, precision=lax.Precision.HIGHEST, preferred_element_type=F32)


def _norm_matmul_kernel(x_ref, g_ref, w_ref, *rest):
    *o_refs, xn_ref = rest

    @pl.when(pl.program_id(1) == 0)
    def _():
        xn_ref[...] = _rms(x_ref[...], g_ref[...]).astype(BF16)

    y = jnp.dot(xn_ref[...], w_ref[...], preferred_element_type=F32)
    for o_ref in o_refs:
        o_ref[...] = y.astype(o_ref.dtype)


def _norm_matmul(x, gain, w, tn, split, dtypes):
    t, d = x.shape
    f = w.shape[1]
    tm = min(512, t)
    nj = f // tn
    if split:
        out_spec = pl.BlockSpec((None, tm, tn), lambda i, j: (j, i, 0))
        shape = (nj, t, tn)
    else:
        out_spec = pl.BlockSpec((tm, tn), lambda i, j: (i, j))
        shape = (t, f)
    return pl.pallas_call(
        _norm_matmul_kernel,
        grid=(t // tm, nj),
        in_specs=[pl.BlockSpec((tm, d), lambda i, j: (i, 0)),
                  pl.BlockSpec((1, d), lambda i, j: (0, 0)),
                  pl.BlockSpec((d, tn), lambda i, j: (0, j))],
        out_specs=[out_spec] * len(dtypes),
        out_shape=[jax.ShapeDtypeStruct(shape, dt) for dt in dtypes],
        scratch_shapes=[pltpu.VMEM((tm, d), BF16)],
        compiler_params=_params("parallel", "arbitrary"),
        name="norm_matmul",
    )(x, gain.reshape(1, d), w.astype(BF16))


def _inproj_kernel(x_ref, g_ref, w_ref, wg_ref, o_ref, og_ref, xn_ref):
    @pl.when(pl.program_id(1) == 0)
    def _():
        xn = _rms(x_ref[...], g_ref[...]).astype(BF16)
        xn_ref[...] = xn
        og_ref[...] = jnp.dot(xn, wg_ref[...], preferred_element_type=F32)

    o_ref[...] = jnp.dot(xn_ref[...], w_ref[...], preferred_element_type=F32)


def _inproj(x, gain, w_in):
    t, d = x.shape
    tm = min(512, t)
    tn = 1024
    w_main = w_in[:, :QKVZ_W_A].astype(BF16)
    w_gate = jnp.pad(w_in[:, QKVZ_W_A:], ((0, 0), (0, LANES - 2 * HEADS_A))).astype(BF16)
    return pl.pallas_call(
        _inproj_kernel,
        grid=(t // tm, QKVZ_W_A // tn),
        in_specs=[pl.BlockSpec((tm, d), lambda i, j: (i, 0)),
                  pl.BlockSpec((1, d), lambda i, j: (0, 0)),
                  pl.BlockSpec((d, tn), lambda i, j: (0, j)),
                  pl.BlockSpec((d, LANES), lambda i, j: (0, 0))],
        out_specs=[pl.BlockSpec((tm, tn), lambda i, j: (i, j)),
                   pl.BlockSpec((tm, LANES), lambda i, j: (i, 0))],
        out_shape=[jax.ShapeDtypeStruct((t, QKVZ_W_A), F32),
                   jax.ShapeDtypeStruct((t, LANES), F32)],
        scratch_shapes=[pltpu.VMEM((tm, d), BF16)],
        compiler_params=_params("parallel", "arbitrary"),
        name="gdn_inproj",
    )(x, gain.reshape(1, d), w_main, w_gate)


def _mlp_kernel(*refs, n_mix, final_norm):
    x_ref = refs[0]
    mix_refs = refs[1:1 + n_mix]
    wo_ref, g_ref, wup_ref, wdn_ref, gf_ref, y_ref, x1_ref, xn_ref, acc_ref = refs[1 + n_mix:]
    f = pl.program_id(1)

    @pl.when(f == 0)
    def _():
        if n_mix == 1:
            mixed = mix_refs[0][...]
        else:
            lses = [r[...] for r in mix_refs[N_BRANCH:]]
            m = functools.reduce(jnp.maximum, lses)
            ws = [jnp.exp(l - m) for l in lses]
            total = sum(ws)
            low = lax.broadcasted_iota(jnp.int32, (1, 2 * HEAD_DIM_B), 1) < HEAD_DIM_B
            mixed = 0.0
            for w, o_ref_b in zip(ws, mix_refs[:N_BRANCH]):
                w = w / total
                per_lane = jnp.concatenate([jnp.where(low, w[:, 2 * p:2 * p + 1], w[:, 2 * p + 1:2 * p + 2])
                                            for p in range(HEADS_B // 2)], axis=1)
                mixed = mixed + per_lane * o_ref_b[...].astype(F32)
        x1 = x_ref[...] + _dot(mixed, wo_ref[...])
        x1_ref[...] = x1
        xn_ref[...] = _rms(x1, g_ref[...]).astype(BF16)
        acc_ref[...] = jnp.zeros_like(acc_ref)

    u = jnp.maximum(jnp.dot(xn_ref[...], wup_ref[...], preferred_element_type=F32), 0.0)
    acc_ref[...] += _dot(u * u, wdn_ref[...])

    @pl.when(f == pl.num_programs(1) - 1)
    def _():
        y = x1_ref[...] + acc_ref[...]
        if final_norm:
            y = _rms(y, gf_ref[...])
        y_ref[...] = y


def _mlp_block(x, mix, w_o, gain, w_up, w_down, gain_final, final_norm):
    t, d = x.shape
    k = w_o.shape[0]
    dff = w_up.shape[1]
    tm = min(512, t)
    tf = 512
    n_mix = len(mix)
    row = lambda i, f: (i, 0)
    const = lambda i, f: (0, 0)
    return pl.pallas_call(
        functools.partial(_mlp_kernel, n_mix=n_mix, final_norm=final_norm),
        grid=(t // tm, dff // tf),
        in_specs=([pl.BlockSpec((tm, d), row)]
                  + [pl.BlockSpec((tm, a.shape[1]), row) for a in mix]
                  + [pl.BlockSpec((k, d), const),
                     pl.BlockSpec((1, d), const),
                     pl.BlockSpec((d, tf), lambda i, f: (0, f)),
                     pl.BlockSpec((tf, d), lambda i, f: (f, 0)),
                     pl.BlockSpec((1, d), const)]),
        out_specs=pl.BlockSpec((tm, d), row),
        out_shape=jax.ShapeDtypeStruct((t, d), F32),
        scratch_shapes=[pltpu.VMEM((tm, d), F32), pltpu.VMEM((tm, d), BF16), pltpu.VMEM((tm, d), F32)],
        compiler_params=_params("parallel", "arbitrary"),
        name="mix_out_mlp",
    )(x, *mix, w_o.astype(BF16), gain.reshape(1, d), w_up.astype(BF16), w_down.astype(BF16),
      gain_final.reshape(1, d))


def _inverse_masks(c):
    row = lax.broadcasted_iota(jnp.int32, (c, c), 0)
    col = lax.broadcasted_iota(jnp.int32, (c, c), 1)
    masks = []
    shift = 0
    while (1 << shift) < c:
        same_pair = (row >> (shift + 1)) == (col >> (shift + 1))
        other_half = (row >> shift) != (col >> shift)
        masks.append((row > col) & same_pair & other_half)
        shift += 1
    return masks


def _unit_lower_inverses(mats, c):
    masks = _inverse_masks(c)
    row = lax.broadcasted_iota(jnp.int32, (c, c), 0)
    col = lax.broadcasted_iota(jnp.int32, (c, c), 1)
    eye = (row == col).astype(F32)
    invs = [eye - jnp.where(masks[0], a, 0.0) for a in mats]
    for mask in masks[1:]:
        xs = [_dot(inv, jnp.where(mask, a, 0.0)) for inv, a in zip(invs, mats)]
        invs = [inv - _dot(x, inv) for inv, x in zip(invs, xs)]
    return invs


def _gdn_kernel(qkv_ref, z_ref, ba_ref, bat_ref, cw_ref, s0_ref, c0_ref, alr_ref, dtr_ref, alc_ref, dtc_ref,
                gn_ref, o_ref, s_ref, ext_ref, *, chunk):
    c = chunk
    ci = pl.program_id(1)
    halo = SUBLANES

    @pl.when(ci == 0)
    def _():
        s_ref[...] = s0_ref[...]
        ext_ref[pl.ds(0, halo), :] = c0_ref[...]

    ext_ref[pl.ds(halo, c), :] = qkv_ref[...]
    conv = ext_ref[pl.ds(halo - (CONV_W - 1), c), :] * cw_ref[0:1, :]
    for j in range(1, CONV_W):
        conv = conv + ext_ref[pl.ds(halo - (CONV_W - 1) + j, c), :] * cw_ref[j:j + 1, :]
    ext_ref[pl.ds(0, halo), :] = ext_ref[pl.ds(c, halo), :]
    act = conv * _sigmoid(conv)

    ba = ba_ref[...]
    bat = bat_ref[...]
    beta_cols = _sigmoid(ba)
    g_cols = -jnp.exp(alr_ref[...]) * _softplus(ba + dtr_ref[...])
    g_rows = -jnp.exp(alc_ref[...]) * _softplus(bat + dtc_ref[...])
    row = lax.broadcasted_iota(jnp.int32, (c, c), 0)
    col = lax.broadcasted_iota(jnp.int32, (c, c), 1)
    causal = row >= col
    strict = row > col
    cum_cols = _dot_f32(causal.astype(F32), g_cols)
    cum_rows = _dot_f32(g_rows, (row <= col).astype(F32))

    heads = range(HEADS_A)
    qs, ks, vs, betas, gcs, decays = [], [], [], [], [], []
    for h in heads:
        lo = h * DK_A
        q = act[:, lo:lo + DK_A]
        k = act[:, KEY_W_A + lo:KEY_W_A + lo + DK_A]
        qs.append(q * lax.rsqrt(jnp.sum(q * q, axis=-1, keepdims=True) + EPS) * (DK_A ** -0.5))
        ks.append(k * lax.rsqrt(jnp.sum(k * k, axis=-1, keepdims=True) + EPS))
        vs.append(act[:, 2 * KEY_W_A + lo:2 * KEY_W_A + lo + DV_A])
        betas.append(beta_cols[:, h:h + 1])
        gc = cum_cols[:, HEADS_A + h:HEADS_A + h + 1]
        gr = cum_rows[HEADS_A + h:HEADS_A + h + 1, :]
        gcs.append(gc)
        decays.append(jnp.where(causal, jnp.exp(jnp.where(causal, gc - gr, 0.0)), 0.0))
    kbs = [k * b for k, b in zip(ks, betas)]
    kq = [_dot_nt(jnp.concatenate([kb, q], axis=0), k) for kb, q, k in zip(kbs, qs, ks)]
    mats = [jnp.where(strict, x[:c] * d, 0.0) for x, d in zip(kq, decays)]
    attns = [x[c:] * d for x, d in zip(kq, decays)]
    t_invs = _unit_lower_inverses(mats, c)
    e_gcs = [jnp.exp(gc) for gc in gcs]
    uws = [_dot(t, jnp.concatenate([v * b, kb * e], axis=1))
           for t, v, b, kb, e in zip(t_invs, vs, betas, kbs, e_gcs)]
    states = [s_ref[h] for h in heads]
    wq_s = [_dot(jnp.concatenate([uw[:, DV_A:], q * e], axis=0), s)
            for uw, q, e, s in zip(uws, qs, e_gcs, states)]
    v_news = [uw[:, :DV_A] - x[:c] for uw, x in zip(uws, wq_s)]
    outs = [x[c:] + _dot(attn, vn) for x, attn, vn in zip(wq_s, attns, v_news)]
    for h in heads:
        g_last = gcs[h][c - 1:c, :]
        s_ref[h] = states[h] * jnp.exp(g_last) + _dot_tn(ks[h] * jnp.exp(g_last - gcs[h]), v_news[h])
    for h in heads:
        lo = h * DV_A
        o = _rms(outs[h], gn_ref[...])
        z = z_ref[:, lo:lo + DV_A]
        o_ref[:, lo:lo + DV_A] = (o * z * _sigmoid(z)).astype(o_ref.dtype)


def _gdn(qkvz, ba, n, seq_len, chunk, s0, conv0, w_conv, a_log, dt_bias, g_norm):
    nc = seq_len // chunk
    t = n * seq_len
    halo = SUBLANES
    bat = ba[:, :2 * HEADS_A].reshape(n * nc, chunk, 2 * HEADS_A).transpose(0, 2, 1)
    c0 = jnp.pad(conv0, ((0, 0), (halo - (CONV_W - 1), 0), (0, 0)))
    pad_row = lambda p: jnp.pad(p.reshape(1, HEADS_A), ((0, 0), (HEADS_A, LANES - 2 * HEADS_A)))
    pad_col = lambda p: jnp.pad(p.reshape(HEADS_A, 1), ((HEADS_A, 0), (0, 0)))
    blk = lambda i, j: (i * nc + j, 0)
    const = lambda i, j: (0, 0)
    return pl.pallas_call(
        functools.partial(_gdn_kernel, chunk=chunk),
        grid=(n, nc),
        in_specs=[pl.BlockSpec((chunk, QKV_W_A), blk),
                  pl.BlockSpec((chunk, VAL_W_A), lambda i, j: (i * nc + j, QKV_W_A // VAL_W_A)),
                  pl.BlockSpec((chunk, LANES), blk),
                  pl.BlockSpec((None, 2 * HEADS_A, chunk), lambda i, j: (i * nc + j, 0, 0)),
                  pl.BlockSpec((CONV_W, QKV_W_A), const),
                  pl.BlockSpec((None, HEADS_A, DK_A, DV_A), lambda i, j: (i, 0, 0, 0)),
                  pl.BlockSpec((None, halo, QKV_W_A), lambda i, j: (i, 0, 0)),
                  pl.BlockSpec((1, LANES), const),
                  pl.BlockSpec((1, LANES), const),
                  pl.BlockSpec((2 * HEADS_A, 1), const),
                  pl.BlockSpec((2 * HEADS_A, 1), const),
                  pl.BlockSpec((1, DV_A), const)],
        out_specs=[pl.BlockSpec((chunk, VAL_W_A), blk),
                   pl.BlockSpec((None, HEADS_A, DK_A, DV_A), lambda i, j: (i, 0, 0, 0))],
        out_shape=[jax.ShapeDtypeStruct((t, VAL_W_A), BF16),
                   jax.ShapeDtypeStruct((n, HEADS_A, DK_A, DV_A), F32)],
        scratch_shapes=[pltpu.VMEM((halo + chunk, QKV_W_A), F32)],
        compiler_params=_params("parallel", "arbitrary"),
        name="gdn_delta_rule",
    )(qkvz, qkvz, ba, bat, w_conv, s0, c0, pad_row(a_log), pad_row(dt_bias), pad_col(a_log), pad_col(dt_bias),
      g_norm.reshape(1, DV_A))


def _gdn_step_kernel(qkv_ref, z_ref, ba_ref, st_ref, cw_ref, s0_ref, alr_ref, dtr_ref, gn_ref, o_ref, s_ref):
    nb = qkv_ref.shape[0]
    conv = qkv_ref[...] * cw_ref[CONV_W - 1:CONV_W, :]
    for j in range(CONV_W - 1):
        conv = conv + st_ref[j] * cw_ref[j:j + 1, :]
    act = conv * _sigmoid(conv)
    ba = ba_ref[...]
    beta_cols = _sigmoid(ba)
    decay_cols = jnp.exp(-jnp.exp(alr_ref[...]) * _softplus(ba + dtr_ref[...]))
    fill = jnp.zeros((DK_A - nb, DK_A), F32)
    for h in range(HEADS_A):
        lo = h * DK_A
        q = act[:, lo:lo + DK_A]
        k = act[:, KEY_W_A + lo:KEY_W_A + lo + DK_A]
        v = act[:, 2 * KEY_W_A + lo:2 * KEY_W_A + lo + DV_A]
        q = q * lax.rsqrt(jnp.sum(q * q, axis=-1, keepdims=True) + EPS) * (DK_A ** -0.5)
        k = k * lax.rsqrt(jnp.sum(k * k, axis=-1, keepdims=True) + EPS)
        q_t = jnp.concatenate([q, fill], axis=0).T
        k_t = jnp.concatenate([k, fill], axis=0).T
        z = z_ref[:, lo:lo + DV_A]
        gate = z * _sigmoid(z)
        seqs = range(nb)
        k_cols = [k_t[:, i:i + 1] for i in seqs]
        decays = [decay_cols[i:i + 1, HEADS_A + h:HEADS_A + h + 1] for i in seqs]
        k_s = [jnp.sum(k_cols[i] * s0_ref[i, h], axis=0, keepdims=True) for i in seqs]
        v_new = [beta_cols[i:i + 1, h:h + 1] * (v[i:i + 1, :] - decays[i] * k_s[i]) for i in seqs]
        s_new = [decays[i] * s0_ref[i, h] + k_cols[i] * v_new[i] for i in seqs]
        for i in seqs:
            s_ref[i, h] = s_new[i]
        outs = [jnp.sum(q_t[:, i:i + 1] * s_new[i], axis=0, keepdims=True) for i in seqs]
        for i in seqs:
            o_ref[i:i + 1, lo:lo + DV_A] = _rms(outs[i], gn_ref[...]) * gate[i:i + 1, :]


def _gdn_step(qkvz, ba, s0_layers, layer, conv0, w_conv, a_log, dt_bias, g_norm):
    n = qkvz.shape[0]
    nb = SUBLANES
    pad_row = lambda p: jnp.pad(p.reshape(1, HEADS_A), ((0, 0), (HEADS_A, LANES - 2 * HEADS_A)))
    blk = lambda i: (i, 0)
    const = lambda i: (0, 0)
    return pl.pallas_call(
        _gdn_step_kernel,
        grid=(n // nb,),
        in_specs=[pl.BlockSpec((nb, QKV_W_A), blk),
                  pl.BlockSpec((nb, VAL_W_A), lambda i: (i, QKV_W_A // VAL_W_A)),
                  pl.BlockSpec((nb, LANES), blk),
                  pl.BlockSpec((CONV_W - 1, nb, QKV_W_A), lambda i: (0, i, 0)),
                  pl.BlockSpec((CONV_W, QKV_W_A), const),
                  pl.BlockSpec((None, nb, HEADS_A, DK_A, DV_A), lambda i: (layer, i, 0, 0, 0)),
                  pl.BlockSpec((1, LANES), const),
                  pl.BlockSpec((1, LANES), const),
                  pl.BlockSpec((1, DV_A), const)],
        out_specs=[pl.BlockSpec((nb, VAL_W_A), blk),
                   pl.BlockSpec((nb, HEADS_A, DK_A, DV_A), lambda i: (i, 0, 0, 0))],
        out_shape=[jax.ShapeDtypeStruct((n, VAL_W_A), F32),
                   jax.ShapeDtypeStruct((n, HEADS_A, DK_A, DV_A), F32)],
        compiler_params=_params("parallel"),
        name="gdn_step",
    )(qkvz, qkvz, ba, conv0.transpose(1, 0, 2), w_conv, s0_layers, pad_row(a_log), pad_row(dt_bias),
      g_norm.reshape(1, DV_A))


def _attn_prompt_kernel(q_ref, kvc_ref, kvp_ref, o_ref, lse_ref, *, dilation, sub_blocks):
    blk = ATT_BLOCK
    hd = HEAD_DIM_B
    a = pl.program_id(2)
    qi = lax.broadcasted_iota(jnp.int32, (blk, 2 * blk), 0)
    kj = lax.broadcasted_iota(jnp.int32, (blk, 2 * blk), 1)
    steps = blk + qi - kj
    in_window = (steps >= 0) & (steps <= blk)
    started = in_window & ((kj >= blk) | (a > 0))
    dist = (steps * dilation).astype(F32)
    low = lax.broadcasted_iota(jnp.int32, (1, 2 * hd), 1) < hd
    kv_all = jnp.concatenate([kvp_ref[...], kvc_ref[...]], axis=0)
    padded = []
    for kvh in range(KV_HEADS_B):
        own = low if kvh == 0 else ~low
        both = []
        for x in (kv_all[:, :2 * hd], kv_all[:, 2 * hd:]):
            kept = jnp.where(own, x, jnp.zeros_like(x))
            moved = pltpu.roll(kept, hd, axis=1)
            both += [kept, moved] if kvh == 0 else [moved, kept]
        padded.append(both)
    scale = hd ** -0.5
    for j in range(sub_blocks):
        valid = started if j == 0 else in_window
        rows = slice(j * blk, (j + 1) * blk)
        keys = slice(j * blk, (j + 2) * blk)
        for kvh in range(KV_HEADS_B):
            k_lo, k_hi, v_lo, v_hi = (x[keys] for x in padded[kvh])
            heads = [kvh * GQA_B + g for g in range(GQA_B)]
            qs = [q_ref[rows, (h // 2) * 2 * hd:(h // 2 + 1) * 2 * hd] for h in heads]
            ss = [_dot_nt(q, k_hi if h % 2 else k_lo) * scale - ALIBI_SLOPES[h] * dist for q, h in zip(qs, heads)]
            ss = [jnp.where(valid, s, -jnp.inf) for s in ss]
            ms = [jnp.max(s, axis=-1, keepdims=True) for s in ss]
            ps = [jnp.exp(s - m) for s, m in zip(ss, ms)]
            ls = [jnp.sum(p, axis=-1, keepdims=True) for p in ps]
            for g in range(0, GQA_B, 2):
                pair = (heads[g] // 2) * 2 * hd
                o = _dot(ps[g], v_lo) + _dot(ps[g + 1], v_hi)
                o_ref[rows, pair:pair + 2 * hd] = (o / jnp.where(low, ls[g], ls[g + 1])).astype(o_ref.dtype)
            for g, h in enumerate(heads):
                lse_ref[rows, h:h + 1] = ms[g] + jnp.log(ls[g])


def _attn_prompt_branch(q, kv, n, seq_len, dilation):
    ls = seq_len // dilation
    sub_blocks = min(4, ls // ATT_BLOCK)
    qb = sub_blocks * ATT_BLOCK
    qv = q.reshape(n, ls, dilation * Q_W_BRANCH)
    kvv = kv.reshape(n, ls, dilation * KV_W_BRANCH)
    cur = lambda i, r, a: (i, a, r)
    prev = lambda i, r, a: (i, jnp.maximum(a * sub_blocks - 1, 0), r)
    o, lse = pl.pallas_call(
        functools.partial(_attn_prompt_kernel, dilation=dilation, sub_blocks=sub_blocks),
        grid=(n, dilation, ls // qb),
        in_specs=[pl.BlockSpec((None, qb, Q_W_BRANCH), cur),
                  pl.BlockSpec((None, qb, KV_W_BRANCH), cur),
                  pl.BlockSpec((None, ATT_BLOCK, KV_W_BRANCH), prev)],
        out_specs=[pl.BlockSpec((None, qb, Q_W_BRANCH), cur),
                   pl.BlockSpec((None, None, qb, HEADS_B), lambda i, r, a: (i, r, a, 0))],
        out_shape=[jax.ShapeDtypeStruct(qv.shape, BF16),
                   jax.ShapeDtypeStruct((n, dilation, ls, HEADS_B), F32)],
        compiler_params=_params("parallel", "parallel", "arbitrary"),
        name=f"attn_prompt_d{dilation}",
    )(qv, kvv, kvv)
    lse = lse.transpose(0, 2, 1, 3).reshape(n * seq_len, HEADS_B)
    return o.reshape(n * seq_len, Q_W_BRANCH), lse


def _cache_update_kernel(c_ref, kvn_ref, o_ref, g_ref, *, window, dilation):
    nb = c_ref.shape[0]
    nk = ATT_BLOCK
    last = lax.broadcasted_iota(jnp.int32, (1, window), 1) == window - 1
    new_t = jnp.concatenate([kvn_ref[...], jnp.zeros((LANES - nb, KV_W_BRANCH), F32)], axis=0).T
    if dilation > 1:
        src = lax.broadcasted_iota(jnp.int32, (window, nk), 0)
        dst = lax.broadcasted_iota(jnp.int32, (window, nk), 1)
        pick = (src == dst * dilation).astype(BF16)
    for i in range(nb):
        x = c_ref[i].reshape(KV_W_BRANCH, window)
        shifted = jnp.where(last, new_t[:, i:i + 1], pltpu.roll(x, window - 1, axis=1))
        o_ref[i] = shifted.reshape(o_ref.shape[1:])
        seen = x.astype(BF16) if dilation == 1 else _dot(x, pick).astype(BF16)
        g_ref[i] = seen.reshape(g_ref.shape[1:])


def _cache_update(cache_t, kv_new, window, dilation):
    n = cache_t.shape[0]
    nb = max(1, min(SUBLANES, 4096 // window))
    blk = (nb, 2, KV_HEADS_B, HEAD_DIM_B, window)
    seen_blk = (nb, 2, KV_HEADS_B * HEAD_DIM_B, ATT_BLOCK)
    return pl.pallas_call(
        functools.partial(_cache_update_kernel, window=window, dilation=dilation),
        grid=(n // nb,),
        in_specs=[pl.BlockSpec(blk, lambda i: (i, 0, 0, 0, 0)),
                  pl.BlockSpec((None, nb, KV_W_BRANCH), lambda i: (i, 0, 0))],
        out_specs=[pl.BlockSpec(blk, lambda i: (i, 0, 0, 0, 0)),
                   pl.BlockSpec(seen_blk, lambda i: (i, 0, 0, 0))],
        out_shape=[jax.ShapeDtypeStruct(cache_t.shape, F32),
                   jax.ShapeDtypeStruct((n,) + seen_blk[1:], BF16)],
        compiler_params=_params("parallel"),
        name=f"cache_update_w{window}",
    )(cache_t, kv_new.reshape(n // nb, nb, KV_W_BRANCH))


def _attn_sample_kernel(q_ref, kvn_ref, g0_ref, g1_ref, g2_ref, o_ref):
    nk = ATT_BLOCK
    nb = q_ref.shape[0]
    kv_w = KV_HEADS_B * HEAD_DIM_B
    head = lax.broadcasted_iota(jnp.int32, (HEADS_B, 1), 0)
    slopes = jnp.exp2(-8.0 * (head + 1).astype(F32) / HEADS_B)
    key = lax.broadcasted_iota(jnp.int32, (1, nk), 1)
    lane = lax.broadcasted_iota(jnp.int32, (HEADS_B, kv_w), 1)
    own_half = (lane // HEAD_DIM_B) == (head // GQA_B)
    scale = HEAD_DIM_B ** -0.5
    g_refs = (g0_ref, g1_ref, g2_ref)
    pairs = [(i, b) for i in range(nb) for b in range(N_BRANCH)]
    dists = [((nk - key) * d).astype(F32) for d in BRANCH_DILATIONS]
    qs = [q_ref[i, b * HEADS_B:(b + 1) * HEADS_B, :] * scale for i, b in pairs]
    s_old = [_dot(q, g_refs[b][i, 0]) - slopes * dists[b] for q, (i, b) in zip(qs, pairs)]
    s_new = [jnp.sum(q * kvn_ref[b, i:i + 1, :kv_w], axis=-1, keepdims=True) for q, (i, b) in zip(qs, pairs)]
    ms = [jnp.maximum(jnp.max(so, axis=-1, keepdims=True), sn) for so, sn in zip(s_old, s_new)]
    p_old = [jnp.exp(so - m) for so, m in zip(s_old, ms)]
    p_new = [jnp.exp(sn - m) for sn, m in zip(s_new, ms)]
    ls = [jnp.sum(po, axis=-1, keepdims=True) + pn for po, pn in zip(p_old, p_new)]
    outs = [(_dot_nt(po, g_refs[b][i, 1]) + pn * kvn_ref[b, i:i + 1, kv_w:]) / l
            for po, pn, l, (i, b) in zip(p_old, p_new, ls, pairs)]
    lses = [m + jnp.log(l) for m, l in zip(ms, ls)]
    for i in range(nb):
        sl = slice(i * N_BRANCH, (i + 1) * N_BRANCH)
        m = functools.reduce(jnp.maximum, lses[sl])
        ws = [jnp.exp(l - m) for l in lses[sl]]
        merged = sum(w * o for w, o in zip(ws, outs[sl])) / sum(ws)
        o_ref[i] = jnp.where(own_half, merged, 0.0)


def _spread_heads(w, axis):
    w = jnp.moveaxis(w, axis, -1)
    lead = w.shape[:-1]
    w = w.reshape(lead + (-1, KV_HEADS_B, GQA_B, HEAD_DIM_B))
    halves = [jnp.pad(w[..., g, :, :], [(0, 0)] * (len(lead) + 2) + [(g * HEAD_DIM_B, (KV_HEADS_B - 1 - g) * HEAD_DIM_B)])
              for g in range(KV_HEADS_B)]
    out = jnp.stack(halves, axis=-3)
    return jnp.moveaxis(out.reshape(lead + (-1,)), -1, axis)


def _attn_sample(q, kv_new, seen):
    n = q.shape[0]
    nb = SUBLANES
    rows = N_BRANCH * HEADS_B
    seen_blk = (nb, 2, KV_HEADS_B * HEAD_DIM_B, ATT_BLOCK)
    out = pl.pallas_call(
        _attn_sample_kernel,
        grid=(n // nb,),
        in_specs=[pl.BlockSpec((nb, rows, LANES), lambda i: (i, 0, 0)),
                  pl.BlockSpec((N_BRANCH, nb, KV_W_BRANCH), lambda i: (0, i, 0))]
                 + [pl.BlockSpec(seen_blk, lambda i: (i, 0, 0, 0))] * N_BRANCH,
        out_specs=pl.BlockSpec((nb, HEADS_B, LANES), lambda i: (i, 0, 0)),
        out_shape=jax.ShapeDtypeStruct((n, HEADS_B, LANES), F32),
        compiler_params=_params("parallel"),
        name="attn_sample",
    )(q.reshape(n, rows, LANES), kv_new, *seen)
    return out.reshape(n, HEADS_B * LANES)


def _trunk(x, rec0, conv0, kv_bufs, w):
    n, seq_len, d = x.shape
    fresh = kv_bufs is None
    assert seq_len == 1 or (fresh and seq_len % (ATT_BLOCK * max(BRANCH_DILATIONS)) == 0)
    x = x.reshape(n * seq_len, d)
    rec_new, conv_new = [], []
    for l in range(N_LAYERS_A):
        qkvz, ba = _inproj(x, w['norm_mix'][l], w['w_in_a'][l])
        qkv_seq = qkvz.reshape(n, seq_len, QKVZ_W_A)[:, :, :QKV_W_A]
        conv_new.append(jnp.concatenate([conv0[l], qkv_seq], axis=1)[:, seq_len:])
        if fresh:
            o, s_new = _gdn(qkvz, ba, n, seq_len, GDN_CHUNK, rec0[l], conv0[l], w['conv_a'][l], w['a_log'][l],
                            w['dt_bias'][l], w['norm_o_a'][l])
        else:
            o, s_new = _gdn_step(qkvz, ba, rec0, l, conv0[l], w['conv_a'][l], w['a_log'][l], w['dt_bias'][l],
                                 w['norm_o_a'][l])
        rec_new.append(s_new)
        x = _mlp_block(x, [o], w['w_out_a'][l], w['norm_mlp'][l], w['w_up'][l], w['w_down'][l],
                       w['norm_final'], False)

    kv_outs = _norm_matmul(x, w['norm_kv'], w['w_kv'], KV_W_BRANCH, True, (F32, BF16) if fresh else (F32,))
    kv, kv_lo = kv_outs[0], kv_outs[-1]
    bufs_new, seen = [], []
    for b in range(N_BRANCH):
        if fresh:
            new = kv[b].reshape(n, seq_len, 2, KV_HEADS_B, HEAD_DIM_B)
            bufs_new.append(new[:, seq_len - min(BRANCH_WINDOWS[b], seq_len):])
        else:
            assert kv_bufs[b].shape[1] == BRANCH_WINDOWS[b]
            shifted, seen_b = _cache_update(jnp.transpose(kv_bufs[b], (0, 2, 3, 4, 1)), kv[b],
                                            BRANCH_WINDOWS[b], BRANCH_DILATIONS[b])
            bufs_new.append(jnp.transpose(shifted, (0, 4, 1, 2, 3)))
            seen.append(seen_b)

    for l in range(N_LAYERS_A, N_LAYERS_A + N_LAYERS_B):
        lb = l - N_LAYERS_A
        if fresh:
            q, = _norm_matmul(x, w['norm_mix'][l], w['w_q_b'][lb], Q_W_BRANCH, True, (BF16,))
            parts = [_attn_prompt_branch(q[b], kv_lo[b], n, seq_len, BRANCH_DILATIONS[b]) for b in range(N_BRANCH)]
            mix = [p[0] for p in parts] + [p[1] for p in parts]
            w_o = w['w_o_b'][lb]
        else:
            q, = _norm_matmul(x, w['norm_mix'][l], _spread_heads(w['w_q_b'][lb], 1), 1024, False, (F32,))
            mix = [_attn_sample(q, kv, seen)]
            w_o = _spread_heads(w['w_o_b'][lb], 0)
        x = _mlp_block(x, mix, w_o, w['norm_mlp'][l], w['w_up'][l], w['w_down'][l],
                       w['norm_final'], l == N_LAYERS_A + N_LAYERS_B - 1)
    return (x.reshape(n, seq_len, d), jnp.stack(rec_new), jnp.stack(conv_new), *bufs_new)


def kernel(x_prompt, x_sample, state_a_rec, state_a_conv, cache_b0_kv, cache_b1_kv, cache_b2_kv, norm_mix, norm_mlp, w_in_a, conv_a, a_log, dt_bias, norm_o_a, w_out_a, norm_kv, w_kv, w_q_b, w_o_b, w_up, w_down, norm_final):
    w = {'norm_mix': norm_mix, 'norm_mlp': norm_mlp, 'w_in_a': w_in_a, 'conv_a': conv_a, 'a_log': a_log,
         'dt_bias': dt_bias, 'norm_o_a': norm_o_a, 'w_out_a': w_out_a, 'norm_kv': norm_kv, 'w_kv': w_kv,
         'w_q_b': w_q_b, 'w_o_b': w_o_b, 'w_up': w_up, 'w_down': w_down, 'norm_final': norm_final}
    n_p = x_prompt.shape[0]
    p_rec0 = jnp.zeros((N_LAYERS_A, n_p, HEADS_A, DK_A, DV_A), state_a_rec.dtype)
    p_conv0 = jnp.zeros((N_LAYERS_A, n_p, CONV_W - 1, QKV_W_A), state_a_conv.dtype)
    prompt = _trunk(x_prompt, p_rec0, p_conv0, None, w)
    sample = _trunk(x_sample, state_a_rec, state_a_conv, [cache_b0_kv, cache_b1_kv, cache_b2_kv], w)
    return (prompt[0], sample[0], *prompt[1:], *sample[1:])
```

```python
import functools

import jax
import jax.numpy as jnp
from jax import lax
from jax.experimental import pallas as pl
from jax.experimental.pallas import tpu as pltpu

D_MODEL = 1024
HEADS_A = 8
DK_A = 128
DV_A = 128
KEY_W_A = HEADS_A * DK_A
VAL_W_A = HEADS_A * DV_A
QKV_W_A = 2 * KEY_W_A + VAL_W_A
QKVZ_W_A = QKV_W_A + VAL_W_A
CONV_W = 4
N_LAYERS_A = 2
N_LAYERS_B = 2
BRANCH_WINDOWS = (128, 512, 2048)
BRANCH_DILATIONS = (1, 4, 16)
N_BRANCH = 3
HEADS_B = 8
KV_HEADS_B = 2
GQA_B = HEADS_B // KV_HEADS_B
HEAD_DIM_B = 64
Q_W_BRANCH = HEADS_B * HEAD_DIM_B
KV_W_BRANCH = 2 * KV_HEADS_B * HEAD_DIM_B
ALIBI_SLOPES = tuple(2.0 ** (-8.0 * h / HEADS_B) for h in range(1, HEADS_B + 1))
EPS = 1e-6

LANES = 128
SUBLANES = 8
VMEM_LIMIT_BYTES = 56 * 1024 * 1024
ROW_TILE = 1024
GDN_CHUNK = 128
ATT_BLOCK = 128

F32 = jnp.float32
BF16 = jnp.bfloat16


def _params(*sem):
    return pltpu.CompilerParams(dimension_semantics=sem, vmem_limit_bytes=VMEM_LIMIT_BYTES)


def _rms(x, gain):
    return x * lax.rsqrt(jnp.mean(x * x, axis=-1, keepdims=True) + EPS) * gain


def _sigmoid(x):
    return 1.0 / (1.0 + jnp.exp(-x))


def _softplus(x):
    return jnp.maximum(x, 0.0) + jnp.log(1.0 + jnp.exp(-jnp.abs(x)))


def _dot(a, b):
    return jnp.dot(a.astype(BF16), b.astype(BF16), preferred_element_type=F32)


def _dot_nt(a, b):
    return lax.dot_general(a.astype(BF16), b.astype(BF16), (((1,), (1,)), ((), ())),
                           preferred_element_type=F32)


def _dot_tn(a, b):
    return lax.dot_general(a.astype(BF16), b.astype(BF16), (((0,), (0,)), ((), ())),
                           preferred_element_type=F32)


def _dot_f32(a, b):
    return jnp.dot(a, b, precision=lax.Precision.HIGHEST, preferred_element_type=F32)


def _norm_matmul_kernel(x_ref, g_ref, w_ref, *rest):
    *o_refs, xn_ref = rest

    @pl.when(pl.program_id(1) == 0)
    def _():
        xn_ref[...] = _rms(x_ref[...], g_ref[...]).astype(BF16)

    y = jnp.dot(xn_ref[...], w_ref[...], preferred_element_type=F32)
    for o_ref in o_refs:
        o_ref[...] = y.astype(o_ref.dtype)


def _norm_matmul(x, gain, w, tn, split, dtypes):
    t, d = x.shape
    f = w.shape[1]
    tm = min(ROW_TILE, t)
    nj = f // tn
    if split:
        out_spec = pl.BlockSpec((None, tm, tn), lambda i, j: (j, i, 0))
        shape = (nj, t, tn)
    else:
        out_spec = pl.BlockSpec((tm, tn), lambda i, j: (i, j))
        shape = (t, f)
    return pl.pallas_call(
        _norm_matmul_kernel,
        grid=(t // tm, nj),
        in_specs=[pl.BlockSpec((tm, d), lambda i, j: (i, 0)),
                  pl.BlockSpec((1, d), lambda i, j: (0, 0)),
                  pl.BlockSpec((d, tn), lambda i, j: (0, j))],
        out_specs=[out_spec] * len(dtypes),
        out_shape=[jax.ShapeDtypeStruct(shape, dt) for dt in dtypes],
        scratch_shapes=[pltpu.VMEM((tm, d), BF16)],
        compiler_params=_params("parallel", "arbitrary"),
        name="norm_matmul",
    )(x, gain.reshape(1, d), w.astype(BF16))


def _inproj_kernel(x_ref, g_ref, w_ref, wg_ref, o_ref, og_ref, xn_ref):
    @pl.when(pl.program_id(1) == 0)
    def _():
        xn = _rms(x_ref[...], g_ref[...]).astype(BF16)
        xn_ref[...] = xn
        og_ref[...] = jnp.dot(xn, wg_ref[...], preferred_element_type=F32)

    o_ref[...] = jnp.dot(xn_ref[...], w_ref[...], preferred_element_type=F32)


def _inproj(x, gain, w_in):
    t, d = x.shape
    tm = min(ROW_TILE, t)
    tn = 1024
    w_main = w_in[:, :QKVZ_W_A].astype(BF16)
    w_gate = jnp.pad(w_in[:, QKVZ_W_A:], ((0, 0), (0, LANES - 2 * HEADS_A))).astype(BF16)
    return pl.pallas_call(
        _inproj_kernel,
        grid=(t // tm, QKVZ_W_A // tn),
        in_specs=[pl.BlockSpec((tm, d), lambda i, j: (i, 0)),
                  pl.BlockSpec((1, d), lambda i, j: (0, 0)),
                  pl.BlockSpec((d, tn), lambda i, j: (0, j)),
                  pl.BlockSpec((d, LANES), lambda i, j: (0, 0))],
        out_specs=[pl.BlockSpec((tm, tn), lambda i, j: (i, j)),
                   pl.BlockSpec((tm, LANES), lambda i, j: (i, 0))],
        out_shape=[jax.ShapeDtypeStruct((t, QKVZ_W_A), F32),
                   jax.ShapeDtypeStruct((t, LANES), F32)],
        scratch_shapes=[pltpu.VMEM((tm, d), BF16)],
        compiler_params=_params("parallel", "arbitrary"),
        name="gdn_inproj",
    )(x, gain.reshape(1, d), w_main, w_gate)


def _mlp_kernel(*refs, n_mix, final_norm):
    x_ref = refs[0]
    mix_refs = refs[1:1 + n_mix]
    wo_ref, g_ref, wup_ref, wdn_ref, gf_ref, y_ref, x1_ref, xn_ref, acc_ref = refs[1 + n_mix:]
    f = pl.program_id(1)

    @pl.when(f == 0)
    def _():
        if n_mix == 1:
            mixed = mix_refs[0][...]
        else:
            lses = [r[...] for r in mix_refs[N_BRANCH:]]
            m = functools.reduce(jnp.maximum, lses)
            ws = [jnp.exp(l - m) for l in lses]
            total = sum(ws)
            low = lax.broadcasted_iota(jnp.int32, (1, 2 * HEAD_DIM_B), 1) < HEAD_DIM_B
            mixed = 0.0
            for w, o_ref_b in zip(ws, mix_refs[:N_BRANCH]):
                w = w / total
                per_lane = jnp.concatenate([jnp.where(low, w[:, 2 * p:2 * p + 1], w[:, 2 * p + 1:2 * p + 2])
                                            for p in range(HEADS_B // 2)], axis=1)
                mixed = mixed + per_lane * o_ref_b[...].astype(F32)
        x1 = x_ref[...] + _dot(mixed, wo_ref[...])
        x1_ref[...] = x1
        xn_ref[...] = _rms(x1, g_ref[...]).astype(BF16)
        acc_ref[...] = jnp.zeros_like(acc_ref)

    u = jnp.maximum(jnp.dot(xn_ref[...], wup_ref[...], preferred_element_type=F32), 0.0)
    acc_ref[...] += _dot(u * u, wdn_ref[...])

    @pl.when(f == pl.num_programs(1) - 1)
    def _():
        y = x1_ref[...] + acc_ref[...]
        if final_norm:
            y = _rms(y, gf_ref[...])
        y_ref[...] = y


def _mlp_block(x, mix, w_o, gain, w_up, w_down, gain_final, final_norm):
    t, d = x.shape
    k = w_o.shape[0]
    dff = w_up.shape[1]
    tm = min(ROW_TILE, t)
    tf = 512
    n_mix = len(mix)
    row = lambda i, f: (i, 0)
    const = lambda i, f: (0, 0)
    return pl.pallas_call(
        functools.partial(_mlp_kernel, n_mix=n_mix, final_norm=final_norm),
        grid=(t // tm, dff // tf),
        in_specs=([pl.BlockSpec((tm, d), row)]
                  + [pl.BlockSpec((tm, a.shape[1]), row) for a in mix]
                  + [pl.BlockSpec((k, d), const),
                     pl.BlockSpec((1, d), const),
                     pl.BlockSpec((d, tf), lambda i, f: (0, f)),
                     pl.BlockSpec((tf, d), lambda i, f: (f, 0)),
                     pl.BlockSpec((1, d), const)]),
        out_specs=pl.BlockSpec((tm, d), row),
        out_shape=jax.ShapeDtypeStruct((t, d), F32),
        scratch_shapes=[pltpu.VMEM((tm, d), F32), pltpu.VMEM((tm, d), BF16), pltpu.VMEM((tm, d), F32)],
        compiler_params=_params("parallel", "arbitrary"),
        name="mix_out_mlp",
    )(x, *mix, w_o.astype(BF16), gain.reshape(1, d), w_up.astype(BF16), w_down.astype(BF16),
      gain_final.reshape(1, d))


def _inverse_masks(c):
    row = lax.broadcasted_iota(jnp.int32, (c, c), 0)
    col = lax.broadcasted_iota(jnp.int32, (c, c), 1)
    masks = []
    shift = 0
    while (1 << shift) < c:
        same_pair = (row >> (shift + 1)) == (col >> (shift + 1))
        other_half = (row >> shift) != (col >> shift)
        masks.append((row > col) & same_pair & other_half)
        shift += 1
    return masks


def _unit_lower_inverses(mats, c):
    masks = _inverse_masks(c)
    row = lax.broadcasted_iota(jnp.int32, (c, c), 0)
    col = lax.broadcasted_iota(jnp.int32, (c, c), 1)
    eye = (row == col).astype(F32)
    invs = [eye - jnp.where(masks[0], a, 0.0) for a in mats]
    for mask in masks[1:]:
        xs = [_dot(inv, jnp.where(mask, a, 0.0)) for inv, a in zip(invs, mats)]
        invs = [inv - _dot(x, inv) for inv, x in zip(invs, xs)]
    return invs


def _gdn_kernel(qkv_ref, z_ref, ba_ref, bat_ref, cw_ref, s0_ref, c0_ref, alr_ref, dtr_ref, alc_ref, dtc_ref,
                gn_ref, o_ref, s_ref, ext_ref, *, chunk):
    c = chunk
    ci = pl.program_id(1)
    halo = SUBLANES

    @pl.when(ci == 0)
    def _():
        s_ref[...] = s0_ref[...]
        ext_ref[...] = c0_ref[...]

    x = qkv_ref[...]
    ext = jnp.concatenate([ext_ref[...], x], axis=0)
    conv = x * cw_ref[CONV_W - 1:CONV_W, :]
    for back in range(1, CONV_W):
        conv = conv + pltpu.roll(ext, back, axis=0)[halo:] * cw_ref[CONV_W - 1 - back:CONV_W - back, :]
    ext_ref[...] = x[c - halo:]
    act = conv * _sigmoid(conv)

    ba = ba_ref[...]
    bat = bat_ref[...]
    beta_cols = _sigmoid(ba)
    g_cols = -jnp.exp(alr_ref[...]) * _softplus(ba + dtr_ref[...])
    g_rows = -jnp.exp(alc_ref[...]) * _softplus(bat + dtc_ref[...])
    row = lax.broadcasted_iota(jnp.int32, (c, c), 0)
    col = lax.broadcasted_iota(jnp.int32, (c, c), 1)
    causal = row >= col
    strict = row > col
    cum_cols = _dot_f32(causal.astype(F32), g_cols)
    cum_rows = _dot_f32(g_rows, (row <= col).astype(F32))

    heads = range(HEADS_A)
    qs, ks, vs, betas, gcs, decays = [], [], [], [], [], []
    for h in heads:
        lo = h * DK_A
        q = act[:, lo:lo + DK_A]
        k = act[:, KEY_W_A + lo:KEY_W_A + lo + DK_A]
        qs.append(q * lax.rsqrt(jnp.sum(q * q, axis=-1, keepdims=True) + EPS) * (DK_A ** -0.5))
        ks.append(k * lax.rsqrt(jnp.sum(k * k, axis=-1, keepdims=True) + EPS))
        vs.append(act[:, 2 * KEY_W_A + lo:2 * KEY_W_A + lo + DV_A])
        betas.append(beta_cols[:, h:h + 1])
        gc = cum_cols[:, HEADS_A + h:HEADS_A + h + 1]
        gr = cum_rows[HEADS_A + h:HEADS_A + h + 1, :]
        gcs.append(gc)
        decays.append(jnp.where(causal, jnp.exp(jnp.where(causal, gc - gr, 0.0)), 0.0))
    kbs = [k * b for k, b in zip(ks, betas)]
    kq = [_dot_nt(jnp.concatenate([kb, q], axis=0), k) for kb, q, k in zip(kbs, qs, ks)]
    mats = [jnp.where(strict, x[:c] * d, 0.0) for x, d in zip(kq, decays)]
    attns = [x[c:] * d for x, d in zip(kq, decays)]
    t_invs = _unit_lower_inverses(mats, c)
    e_gcs = [jnp.exp(gc) for gc in gcs]
    uws = [_dot(t, jnp.concatenate([v * b, kb * e], axis=1))
           for t, v, b, kb, e in zip(t_invs, vs, betas, kbs, e_gcs)]
    states = [s_ref[h] for h in heads]
    wq_s = [_dot(jnp.concatenate([uw[:, DV_A:], q * e], axis=0), s)
            for uw, q, e, s in zip(uws, qs, e_gcs, states)]
    v_news = [uw[:, :DV_A] - x[:c] for uw, x in zip(uws, wq_s)]
    outs = [x[c:] + _dot(attn, vn) for x, attn, vn in zip(wq_s, attns, v_news)]
    for h in heads:
        g_last = gcs[h][c - 1:c, :]
        s_ref[h] = states[h] * jnp.exp(g_last) + _dot_tn(ks[h] * jnp.exp(g_last - gcs[h]), v_news[h])
    for h in heads:
        lo = h * DV_A
        o = _rms(outs[h], gn_ref[...])
        z = z_ref[:, lo:lo + DV_A]
        o_ref[:, lo:lo + DV_A] = (o * z * _sigmoid(z)).astype(o_ref.dtype)


def _gdn(qkvz, ba, n, seq_len, chunk, s0, conv0, w_conv, a_log, dt_bias, g_norm):
    nc = seq_len // chunk
    t = n * seq_len
    halo = SUBLANES
    bat = ba[:, :2 * HEADS_A].reshape(n * nc, chunk, 2 * HEADS_A).transpose(0, 2, 1)
    c0 = jnp.pad(conv0, ((0, 0), (halo - (CONV_W - 1), 0), (0, 0)))
    pad_row = lambda p: jnp.pad(p.reshape(1, HEADS_A), ((0, 0), (HEADS_A, LANES - 2 * HEADS_A)))
    pad_col = lambda p: jnp.pad(p.reshape(HEADS_A, 1), ((HEADS_A, 0), (0, 0)))
    blk = lambda i, j: (i * nc + j, 0)
    const = lambda i, j: (0, 0)
    return pl.pallas_call(
        functools.partial(_gdn_kernel, chunk=chunk),
        grid=(n, nc),
        in_specs=[pl.BlockSpec((chunk, QKV_W_A), blk),
                  pl.BlockSpec((chunk, VAL_W_A), lambda i, j: (i * nc + j, QKV_W_A // VAL_W_A)),
                  pl.BlockSpec((chunk, LANES), blk),
                  pl.BlockSpec((None, 2 * HEADS_A, chunk), lambda i, j: (i * nc + j, 0, 0)),
                  pl.BlockSpec((CONV_W, QKV_W_A), const),
                  pl.BlockSpec((None, HEADS_A, DK_A, DV_A), lambda i, j: (i, 0, 0, 0)),
                  pl.BlockSpec((None, halo, QKV_W_A), lambda i, j: (i, 0, 0)),
                  pl.BlockSpec((1, LANES), const),
                  pl.BlockSpec((1, LANES), const),
                  pl.BlockSpec((2 * HEADS_A, 1), const),
                  pl.BlockSpec((2 * HEADS_A, 1), const),
                  pl.BlockSpec((1, DV_A), const)],
        out_specs=[pl.BlockSpec((chunk, VAL_W_A), blk),
                   pl.BlockSpec((None, HEADS_A, DK_A, DV_A), lambda i, j: (i, 0, 0, 0))],
        out_shape=[jax.ShapeDtypeStruct((t, VAL_W_A), BF16),
                   jax.ShapeDtypeStruct((n, HEADS_A, DK_A, DV_A), F32)],
        scratch_shapes=[pltpu.VMEM((halo, QKV_W_A), F32)],
        compiler_params=_params("parallel", "arbitrary"),
        name="gdn_delta_rule",
    )(qkvz, qkvz, ba, bat, w_conv, s0, c0, pad_row(a_log), pad_row(dt_bias), pad_col(a_log), pad_col(dt_bias),
      g_norm.reshape(1, DV_A))


def _gdn_step_kernel(qkv_ref, z_ref, ba_ref, st_ref, cw_ref, s0_ref, alr_ref, dtr_ref, gn_ref, o_ref, s_ref):
    nb = qkv_ref.shape[0]
    conv = qkv_ref[...] * cw_ref[CONV_W - 1:CONV_W, :]
    for j in range(CONV_W - 1):
        conv = conv + st_ref[j] * cw_ref[j:j + 1, :]
    act = conv * _sigmoid(conv)
    ba = ba_ref[...]
    beta_cols = _sigmoid(ba)
    decay_cols = jnp.exp(-jnp.exp(alr_ref[...]) * _softplus(ba + dtr_ref[...]))
    fill = jnp.zeros((DK_A - nb, DK_A), F32)
    for h in range(HEADS_A):
        lo = h * DK_A
        q = act[:, lo:lo + DK_A]
        k = act[:, KEY_W_A + lo:KEY_W_A + lo + DK_A]
        v = act[:, 2 * KEY_W_A + lo:2 * KEY_W_A + lo + DV_A]
        q = q * lax.rsqrt(jnp.sum(q * q, axis=-1, keepdims=True) + EPS) * (DK_A ** -0.5)
        k = k * lax.rsqrt(jnp.sum(k * k, axis=-1, keepdims=True) + EPS)
        q_t = jnp.concatenate([q, fill], axis=0).T
        k_t = jnp.concatenate([k, fill], axis=0).T
        z = z_ref[:, lo:lo + DV_A]
        gate = z * _sigmoid(z)
        seqs = range(nb)
        k_cols = [k_t[:, i:i + 1] for i in seqs]
        decays = [decay_cols[i:i + 1, HEADS_A + h:HEADS_A + h + 1] for i in seqs]
        k_s = [jnp.sum(k_cols[i] * s0_ref[i, h], axis=0, keepdims=True) for i in seqs]
        v_new = [beta_cols[i:i + 1, h:h + 1] * (v[i:i + 1, :] - decays[i] * k_s[i]) for i in seqs]
        s_new = [decays[i] * s0_ref[i, h] + k_cols[i] * v_new[i] for i in seqs]
        for i in seqs:
            s_ref[i, h] = s_new[i]
        outs = [jnp.sum(q_t[:, i:i + 1] * s_new[i], axis=0, keepdims=True) for i in seqs]
        for i in seqs:
            o_ref[i:i + 1, lo:lo + DV_A] = _rms(outs[i], gn_ref[...]) * gate[i:i + 1, :]


def _gdn_step(qkvz, ba, s0_layers, layer, conv0, w_conv, a_log, dt_bias, g_norm):
    n = qkvz.shape[0]
    nb = SUBLANES
    pad_row = lambda p: jnp.pad(p.reshape(1, HEADS_A), ((0, 0), (HEADS_A, LANES - 2 * HEADS_A)))
    blk = lambda i: (i, 0)
    const = lambda i: (0, 0)
    return pl.pallas_call(
        _gdn_step_kernel,
        grid=(n // nb,),
        in_specs=[pl.BlockSpec((nb, QKV_W_A), blk),
                  pl.BlockSpec((nb, VAL_W_A), lambda i: (i, QKV_W_A // VAL_W_A)),
                  pl.BlockSpec((nb, LANES), blk),
                  pl.BlockSpec((CONV_W - 1, nb, QKV_W_A), lambda i: (0, i, 0)),
                  pl.BlockSpec((CONV_W, QKV_W_A), const),
                  pl.BlockSpec((None, nb, HEADS_A, DK_A, DV_A), lambda i: (layer, i, 0, 0, 0)),
                  pl.BlockSpec((1, LANES), const),
                  pl.BlockSpec((1, LANES), const),
                  pl.BlockSpec((1, DV_A), const)],
        out_specs=[pl.BlockSpec((nb, VAL_W_A), blk),
                   pl.BlockSpec((nb, HEADS_A, DK_A, DV_A), lambda i: (i, 0, 0, 0))],
        out_shape=[jax.ShapeDtypeStruct((n, VAL_W_A), F32),
                   jax.ShapeDtypeStruct((n, HEADS_A, DK_A, DV_A), F32)],
        compiler_params=_params("parallel"),
        name="gdn_step",
    )(qkvz, qkvz, ba, conv0.transpose(1, 0, 2), w_conv, s0_layers, pad_row(a_log), pad_row(dt_bias),
      g_norm.reshape(1, DV_A))


def _attn_prompt_kernel(q_ref, kvc_ref, kvp_ref, o_ref, lse_ref, *, dilation, sub_blocks):
    blk = ATT_BLOCK
    hd = HEAD_DIM_B
    a = pl.program_id(2)
    qi = lax.broadcasted_iota(jnp.int32, (blk, 2 * blk), 0)
    kj = lax.broadcasted_iota(jnp.int32, (blk, 2 * blk), 1)
    steps = blk + qi - kj
    in_window = (steps >= 0) & (steps <= blk)
    started = in_window & ((kj >= blk) | (a > 0))
    dist = (steps * dilation).astype(F32)
    low = lax.broadcasted_iota(jnp.int32, (1, 2 * hd), 1) < hd
    kv_all = jnp.concatenate([kvp_ref[...], kvc_ref[...]], axis=0)
    padded = []
    for kvh in range(KV_HEADS_B):
        own = low if kvh == 0 else ~low
        both = []
        for x in (kv_all[:, :2 * hd], kv_all[:, 2 * hd:]):
            kept = jnp.where(own, x, jnp.zeros_like(x))
            moved = pltpu.roll(kept, hd, axis=1)
            both += [kept, moved] if kvh == 0 else [moved, kept]
        padded.append(both)
    scale = hd ** -0.5
    for j in range(sub_blocks):
        valid = started if j == 0 else in_window
        rows = slice(j * blk, (j + 1) * blk)
        keys = slice(j * blk, (j + 2) * blk)
        for kvh in range(KV_HEADS_B):
            k_lo, k_hi, v_lo, v_hi = (x[keys] for x in padded[kvh])
            heads = [kvh * GQA_B + g for g in range(GQA_B)]
            qs = [q_ref[rows, (h // 2) * 2 * hd:(h // 2 + 1) * 2 * hd] for h in heads]
            ss = [_dot_nt(q, k_hi if h % 2 else k_lo) * scale - ALIBI_SLOPES[h] * dist for q, h in zip(qs, heads)]
            ss = [jnp.where(valid, s, -jnp.inf) for s in ss]
            ms = [jnp.max(s, axis=-1, keepdims=True) for s in ss]
            ps = [jnp.exp(s - m) for s, m in zip(ss, ms)]
            ls = [jnp.sum(p, axis=-1, keepdims=True) for p in ps]
            for g in range(0, GQA_B, 2):
                pair = (heads[g] // 2) * 2 * hd
                o = _dot(ps[g], v_lo) + _dot(ps[g + 1], v_hi)
                o_ref[rows, pair:pair + 2 * hd] = (o / jnp.where(low, ls[g], ls[g + 1])).astype(o_ref.dtype)
            for g, h in enumerate(heads):
                lse_ref[rows, h:h + 1] = ms[g] + jnp.log(ls[g])


def _attn_prompt_branch(q, kv, n, seq_len, dilation):
    ls = seq_len // dilation
    sub_blocks = min(4, ls // ATT_BLOCK)
    qb = sub_blocks * ATT_BLOCK
    qv = q.reshape(n, ls, dilation * Q_W_BRANCH)
    kvv = kv.reshape(n, ls, dilation * KV_W_BRANCH)
    cur = lambda i, r, a: (i, a, r)
    prev = lambda i, r, a: (i, jnp.maximum(a * sub_blocks - 1, 0), r)
    o, lse = pl.pallas_call(
        functools.partial(_attn_prompt_kernel, dilation=dilation, sub_blocks=sub_blocks),
        grid=(n, dilation, ls // qb),
        in_specs=[pl.BlockSpec((None, qb, Q_W_BRANCH), cur),
                  pl.BlockSpec((None, qb, KV_W_BRANCH), cur),
                  pl.BlockSpec((None, ATT_BLOCK, KV_W_BRANCH), prev)],
        out_specs=[pl.BlockSpec((None, qb, Q_W_BRANCH), cur),
                   pl.BlockSpec((None, None, qb, HEADS_B), lambda i, r, a: (i, r, a, 0))],
        out_shape=[jax.ShapeDtypeStruct(qv.shape, BF16),
                   jax.ShapeDtypeStruct((n, dilation, ls, HEADS_B), F32)],
        compiler_params=_params("parallel", "parallel", "arbitrary"),
        name=f"attn_prompt_d{dilation}",
    )(qv, kvv, kvv)
    lse = lse.transpose(0, 2, 1, 3).reshape(n * seq_len, HEADS_B)
    return o.reshape(n * seq_len, Q_W_BRANCH), lse


def _cache_update_kernel(c_ref, kvn_ref, o_ref, g_ref, *, window, dilation):
    nb = c_ref.shape[0]
    nk = ATT_BLOCK
    last = lax.broadcasted_iota(jnp.int32, (1, window), 1) == window - 1
    new_t = jnp.concatenate([kvn_ref[...], jnp.zeros((LANES - nb, KV_W_BRANCH), F32)], axis=0).T
    if dilation > 1:
        src = lax.broadcasted_iota(jnp.int32, (window, nk), 0)
        dst = lax.broadcasted_iota(jnp.int32, (window, nk), 1)
        pick = (src == dst * dilation).astype(BF16)
    for i in range(nb):
        x = c_ref[i].reshape(KV_W_BRANCH, window)
        shifted = jnp.where(last, new_t[:, i:i + 1], pltpu.roll(x, window - 1, axis=1))
        o_ref[i] = shifted.reshape(o_ref.shape[1:])
        seen = x.astype(BF16) if dilation == 1 else _dot(x, pick).astype(BF16)
        g_ref[i] = seen.reshape(g_ref.shape[1:])


def _cache_update(cache_t, kv_new, window, dilation):
    n = cache_t.shape[0]
    nb = max(1, min(SUBLANES, 4096 // window))
    blk = (nb, 2, KV_HEADS_B, HEAD_DIM_B, window)
    seen_blk = (nb, 2, KV_HEADS_B * HEAD_DIM_B, ATT_BLOCK)
    return pl.pallas_call(
        functools.partial(_cache_update_kernel, window=window, dilation=dilation),
        grid=(n // nb,),
        in_specs=[pl.BlockSpec(blk, lambda i: (i, 0, 0, 0, 0)),
                  pl.BlockSpec((None, nb, KV_W_BRANCH), lambda i: (i, 0, 0))],
        out_specs=[pl.BlockSpec(blk, lambda i: (i, 0, 0, 0, 0)),
                   pl.BlockSpec(seen_blk, lambda i: (i, 0, 0, 0))],
        out_shape=[jax.ShapeDtypeStruct(cache_t.shape, F32),
                   jax.ShapeDtypeStruct((n,) + seen_blk[1:], BF16)],
        compiler_params=_params("parallel"),
        name=f"cache_update_w{window}",
    )(cache_t, kv_new.reshape(n // nb, nb, KV_W_BRANCH))


def _attn_sample_kernel(q_ref, kvn_ref, g0_ref, g1_ref, g2_ref, o_ref):
    nk = ATT_BLOCK
    nb = q_ref.shape[0]
    kv_w = KV_HEADS_B * HEAD_DIM_B
    head = lax.broadcasted_iota(jnp.int32, (HEADS_B, 1), 0)
    slopes = jnp.exp2(-8.0 * (head + 1).astype(F32) / HEADS_B)
    key = lax.broadcasted_iota(jnp.int32, (1, nk), 1)
    lane = lax.broadcasted_iota(jnp.int32, (HEADS_B, kv_w), 1)
    own_half = (lane // HEAD_DIM_B) == (head // GQA_B)
    scale = HEAD_DIM_B ** -0.5
    g_refs = (g0_ref, g1_ref, g2_ref)
    pairs = [(i, b) for i in range(nb) for b in range(N_BRANCH)]
    dists = [((nk - key) * d).astype(F32) for d in BRANCH_DILATIONS]
    qs = [q_ref[i, b * HEADS_B:(b + 1) * HEADS_B, :] * scale for i, b in pairs]
    s_old = [_dot(q, g_refs[b][i, 0]) - slopes * dists[b] for q, (i, b) in zip(qs, pairs)]
    s_new = [jnp.sum(q * kvn_ref[b, i:i + 1, :kv_w], axis=-1, keepdims=True) for q, (i, b) in zip(qs, pairs)]
    ms = [jnp.maximum(jnp.max(so, axis=-1, keepdims=True), sn) for so, sn in zip(s_old, s_new)]
    p_old = [jnp.exp(so - m) for so, m in zip(s_old, ms)]
    p_new = [jnp.exp(sn - m) for sn, m in zip(s_new, ms)]
    ls = [jnp.sum(po, axis=-1, keepdims=True) + pn for po, pn in zip(p_old, p_new)]
    outs = [(_dot_nt(po, g_refs[b][i, 1]) + pn * kvn_ref[b, i:i + 1, kv_w:]) / l
            for po, pn, l, (i, b) in zip(p_old, p_new, ls, pairs)]
    lses = [m + jnp.log(l) for m, l in zip(ms, ls)]
    for i in range(nb):
        sl = slice(i * N_BRANCH, (i + 1) * N_BRANCH)
        m = functools.reduce(jnp.maximum, lses[sl])
        ws = [jnp.exp(l - m) for l in lses[sl]]
        merged = sum(w * o for w, o in zip(ws, outs[sl])) / sum(ws)
        o_ref[i] = jnp.where(own_half, merged, 0.0)


def _spread_heads(w, axis):
    w = jnp.moveaxis(w, axis, -1)
    lead = w.shape[:-1]
    w = w.reshape(lead + (-1, KV_HEADS_B, GQA_B, HEAD_DIM_B))
    halves = [jnp.pad(w[..., g, :, :], [(0, 0)] * (len(lead) + 2) + [(g * HEAD_DIM_B, (KV_HEADS_B - 1 - g) * HEAD_DIM_B)])
              for g in range(KV_HEADS_B)]
    out = jnp.stack(halves, axis=-3)
    return jnp.moveaxis(out.reshape(lead + (-1,)), -1, axis)


def _attn_sample(q, kv_new, seen):
    n = q.shape[0]
    nb = SUBLANES
    rows = N_BRANCH * HEADS_B
    seen_blk = (nb, 2, KV_HEADS_B * HEAD_DIM_B, ATT_BLOCK)
    out = pl.pallas_call(
        _attn_sample_kernel,
        grid=(n // nb,),
        in_specs=[pl.BlockSpec((nb, rows, LANES), lambda i: (i, 0, 0)),
                  pl.BlockSpec((N_BRANCH, nb, KV_W_BRANCH), lambda i: (0, i, 0))]
                 + [pl.BlockSpec(seen_blk, lambda i: (i, 0, 0, 0))] * N_BRANCH,
        out_specs=pl.BlockSpec((nb, HEADS_B, LANES), lambda i: (i, 0, 0)),
        out_shape=jax.ShapeDtypeStruct((n, HEADS_B, LANES), F32),
        compiler_params=_params("parallel"),
        name="attn_sample",
    )(q.reshape(n, rows, LANES), kv_new, *seen)
    return out.reshape(n, HEADS_B * LANES)


def _trunk(x, rec0, conv0, kv_bufs, w):
    n, seq_len, d = x.shape
    fresh = kv_bufs is None
    assert seq_len == 1 or (fresh and seq_len % (ATT_BLOCK * max(BRANCH_DILATIONS)) == 0)
    x = x.reshape(n * seq_len, d)
    rec_new, conv_new = [], []
    for l in range(N_LAYERS_A):
        qkvz, ba = _inproj(x, w['norm_mix'][l], w['w_in_a'][l])
        qkv_seq = qkvz.reshape(n, seq_len, QKVZ_W_A)[:, :, :QKV_W_A]
        conv_new.append(jnp.concatenate([conv0[l], qkv_seq], axis=1)[:, seq_len:])
        if fresh:
            o, s_new = _gdn(qkvz, ba, n, seq_len, GDN_CHUNK, rec0[l], conv0[l], w['conv_a'][l], w['a_log'][l],
                            w['dt_bias'][l], w['norm_o_a'][l])
        else:
            o, s_new = _gdn_step(qkvz, ba, rec0, l, conv0[l], w['conv_a'][l], w['a_log'][l], w['dt_bias'][l],
                                 w['norm_o_a'][l])
        rec_new.append(s_new)
        x = _mlp_block(x, [o], w['w_out_a'][l], w['norm_mlp'][l], w['w_up'][l], w['w_down'][l],
                       w['norm_final'], False)

    kv_outs = _norm_matmul(x, w['norm_kv'], w['w_kv'], KV_W_BRANCH, True, (F32, BF16) if fresh else (F32,))
    kv, kv_lo = kv_outs[0], kv_outs[-1]
    bufs_new, seen = [], []
    for b in range(N_BRANCH):
        if fresh:
            new = kv[b].reshape(n, seq_len, 2, KV_HEADS_B, HEAD_DIM_B)
            bufs_new.append(new[:, seq_len - min(BRANCH_WINDOWS[b], seq_len):])
        else:
            assert kv_bufs[b].shape[1] == BRANCH_WINDOWS[b]
            shifted, seen_b = _cache_update(jnp.transpose(kv_bufs[b], (0, 2, 3, 4, 1)), kv[b],
                                            BRANCH_WINDOWS[b], BRANCH_DILATIONS[b])
            bufs_new.append(jnp.transpose(shifted, (0, 4, 1, 2, 3)))
            seen.append(seen_b)

    for l in range(N_LAYERS_A, N_LAYERS_A + N_LAYERS_B):
        lb = l - N_LAYERS_A
        if fresh:
            q, = _norm_matmul(x, w['norm_mix'][l], w['w_q_b'][lb], Q_W_BRANCH, True, (BF16,))
            parts = [_attn_prompt_branch(q[b], kv_lo[b], n, seq_len, BRANCH_DILATIONS[b]) for b in range(N_BRANCH)]
            mix = [p[0] for p in parts] + [p[1] for p in parts]
            w_o = w['w_o_b'][lb]
        else:
            q, = _norm_matmul(x, w['norm_mix'][l], _spread_heads(w['w_q_b'][lb], 1), 1024, False, (F32,))
            mix = [_attn_sample(q, kv, seen)]
            w_o = _spread_heads(w['w_o_b'][lb], 0)
        x = _mlp_block(x, mix, w_o, w['norm_mlp'][l], w['w_up'][l], w['w_down'][l],
                       w['norm_final'], l == N_LAYERS_A + N_LAYERS_B - 1)
    return (x.reshape(n, seq_len, d), jnp.stack(rec_new), jnp.stack(conv_new), *bufs_new)


def kernel(x_prompt, x_sample, state_a_rec, state_a_conv, cache_b0_kv, cache_b1_kv, cache_b2_kv, norm_mix, norm_mlp, w_in_a, conv_a, a_log, dt_bias, norm_o_a, w_out_a, norm_kv, w_kv, w_q_b, w_o_b, w_up, w_down, norm_final):
    w = {'norm_mix': norm_mix, 'norm_mlp': norm_mlp, 'w_in_a': w_in_a, 'conv_a': conv_a, 'a_log': a_log,
         'dt_bias': dt_bias, 'norm_o_a': norm_o_a, 'w_out_a': w_out_a, 'norm_kv': norm_kv, 'w_kv': w_kv,
         'w_q_b': w_q_b, 'w_o_b': w_o_b, 'w_up': w_up, 'w_down': w_down, 'norm_final': norm_final}
    n_p = x_prompt.shape[0]
    p_rec0 = jnp.zeros((N_LAYERS_A, n_p, HEADS_A, DK_A, DV_A), state_a_rec.dtype)
    p_conv0 = jnp.zeros((N_LAYERS_A, n_p, CONV_W - 1, QKV_W_A), state_a_conv.dtype)
    prompt = _trunk(x_prompt, p_rec0, p_conv0, None, w)
    sample = _trunk(x_sample, state_a_rec, state_a_conv, [cache_b0_kv, cache_b1_kv, cache_b2_kv], w)
    return (prompt[0], sample[0], *prompt[1:], *sample[1:])
```

```python
import functools

import jax
import jax.numpy as jnp
from jax import lax
from jax.experimental import pallas as pl
from jax.experimental.pallas import tpu as pltpu

D_MODEL = 1024
HEADS_A = 8
DK_A = 128
DV_A = 128
KEY_W_A = HEADS_A * DK_A
VAL_W_A = HEADS_A * DV_A
QKV_W_A = 2 * KEY_W_A + VAL_W_A
QKVZ_W_A = QKV_W_A + VAL_W_A
CONV_W = 4
N_LAYERS_A = 2
N_LAYERS_B = 2
BRANCH_WINDOWS = (128, 512, 2048)
BRANCH_DILATIONS = (1, 4, 16)
N_BRANCH = 3
HEADS_B = 8
KV_HEADS_B = 2
GQA_B = HEADS_B // KV_HEADS_B
HEAD_DIM_B = 64
Q_W_BRANCH = HEADS_B * HEAD_DIM_B
KV_W_BRANCH = 2 * KV_HEADS_B * HEAD_DIM_B
ALIBI_SLOPES = tuple(2.0 ** (-8.0 * h / HEADS_B) for h in range(1, HEADS_B + 1))
EPS = 1e-6

LANES = 128
SUBLANES = 8
VMEM_LIMIT_BYTES = 56 * 1024 * 1024
ROW_TILE = 1024
MLP_FF_TILE = 512
GDN_CHUNK = 128
ATT_BLOCK = 128

F32 = jnp.float32
BF16 = jnp.bfloat16


def _params(*sem):
    return pltpu.CompilerParams(dimension_semantics=sem, vmem_limit_bytes=VMEM_LIMIT_BYTES)


def _rms(x, gain):
    return x * lax.rsqrt(jnp.mean(x * x, axis=-1, keepdims=True) + EPS) * gain


def _sigmoid(x):
    return 1.0 / (1.0 + jnp.exp(-x))


def _softplus(x):
    return jnp.maximum(x, 0.0) + jnp.log(1.0 + jnp.exp(-jnp.abs(x)))


def _dot(a, b):
    return jnp.dot(a.astype(BF16), b.astype(BF16), preferred_element_type=F32)


def _dot_nt(a, b):
    return lax.dot_general(a.astype(BF16), b.astype(BF16), (((1,), (1,)), ((), ())),
                           preferred_element_type=F32)


def _dot_tn(a, b):
    return lax.dot_general(a.astype(BF16), b.astype(BF16), (((0,), (0,)), ((), ())),
                           preferred_element_type=F32)


def _dot_f32(a, b):
    return jnp.dot(a, b, precision=lax.Precision.HIGHEST, preferred_element_type=F32)


def _norm_matmul_kernel(x_ref, g_ref, w_ref, *rest, groups):
    *o_refs, xn_ref = rest
    j = pl.program_id(1)

    @pl.when(j == 0)
    def _():
        xn_ref[...] = _rms(x_ref[...], g_ref[...]).astype(BF16)

    y = jnp.dot(xn_ref[...], w_ref[...], preferred_element_type=F32)
    if groups == 1:
        for o_ref in o_refs:
            o_ref[...] = y.astype(o_ref.dtype)
    else:
        for g in range(groups):
            @pl.when(j == g)
            def _():
                for o_ref in o_refs[g::groups]:
                    o_ref[...] = y.astype(o_ref.dtype)


def _norm_matmul(x, gain, w, tn, split, dtypes):
    t, d = x.shape
    f = w.shape[1]
    tm = min(ROW_TILE, t)
    nj = f // tn
    if split:
        out_specs = [pl.BlockSpec((tm, tn), lambda i, j: (i, 0))] * (nj * len(dtypes))
        out_shape = [jax.ShapeDtypeStruct((t, tn), dt) for dt in dtypes for _ in range(nj)]
    else:
        out_specs = [pl.BlockSpec((tm, tn), lambda i, j: (i, j))] * len(dtypes)
        out_shape = [jax.ShapeDtypeStruct((t, f), dt) for dt in dtypes]
    outs = pl.pallas_call(
        functools.partial(_norm_matmul_kernel, groups=nj if split else 1),
        grid=(t // tm, nj),
        in_specs=[pl.BlockSpec((tm, d), lambda i, j: (i, 0)),
                  pl.BlockSpec((1, d), lambda i, j: (0, 0)),
                  pl.BlockSpec((d, tn), lambda i, j: (0, j))],
        out_specs=out_specs,
        out_shape=out_shape,
        scratch_shapes=[pltpu.VMEM((tm, d), BF16)],
        compiler_params=_params("parallel", "arbitrary"),
        name="norm_matmul",
    )(x, gain.reshape(1, d), w.astype(BF16))
    if split:
        return [outs[k * nj:(k + 1) * nj] for k in range(len(dtypes))]
    return outs


def _inproj_kernel(x_ref, g_ref, w_ref, wg_ref, o_ref, og_ref, xn_ref):
    @pl.when(pl.program_id(1) == 0)
    def _():
        xn = _rms(x_ref[...], g_ref[...]).astype(BF16)
        xn_ref[...] = xn
        og_ref[...] = jnp.dot(xn, wg_ref[...], preferred_element_type=F32)

    o_ref[...] = jnp.dot(xn_ref[...], w_ref[...], preferred_element_type=F32)


def _inproj(x, gain, w_in):
    t, d = x.shape
    tm = min(ROW_TILE, t)
    tn = 1024
    w_main = w_in[:, :QKVZ_W_A].astype(BF16)
    w_gate = jnp.pad(w_in[:, QKVZ_W_A:], ((0, 0), (0, LANES - 2 * HEADS_A))).astype(BF16)
    return pl.pallas_call(
        _inproj_kernel,
        grid=(t // tm, QKVZ_W_A // tn),
        in_specs=[pl.BlockSpec((tm, d), lambda i, j: (i, 0)),
                  pl.BlockSpec((1, d), lambda i, j: (0, 0)),
                  pl.BlockSpec((d, tn), lambda i, j: (0, j)),
                  pl.BlockSpec((d, LANES), lambda i, j: (0, 0))],
        out_specs=[pl.BlockSpec((tm, tn), lambda i, j: (i, j)),
                   pl.BlockSpec((tm, LANES), lambda i, j: (i, 0))],
        out_shape=[jax.ShapeDtypeStruct((t, QKVZ_W_A), F32),
                   jax.ShapeDtypeStruct((t, LANES), F32)],
        scratch_shapes=[pltpu.VMEM((tm, d), BF16)],
        compiler_params=_params("parallel", "arbitrary"),
        name="gdn_inproj",
    )(x, gain.reshape(1, d), w_main, w_gate)


def _mlp_kernel(*refs, n_mix, final_norm):
    x_ref = refs[0]
    mix_refs = refs[1:1 + n_mix]
    wo_ref, g_ref, wup_ref, wdn_ref, gf_ref, y_ref, x1_ref, xn_ref, acc_ref = refs[1 + n_mix:]
    f = pl.program_id(1)

    @pl.when(f == 0)
    def _():
        if n_mix == 1:
            mixed = mix_refs[0][...]
        else:
            lses = [r[...] for r in mix_refs[N_BRANCH:]]
            m = functools.reduce(jnp.maximum, lses)
            ws = [jnp.exp(l - m) for l in lses]
            total = sum(ws)
            low = lax.broadcasted_iota(jnp.int32, (1, 2 * HEAD_DIM_B), 1) < HEAD_DIM_B
            mixed = 0.0
            for w, o_ref_b in zip(ws, mix_refs[:N_BRANCH]):
                w = w / total
                per_lane = jnp.concatenate([jnp.where(low, w[:, 2 * p:2 * p + 1], w[:, 2 * p + 1:2 * p + 2])
                                            for p in range(HEADS_B // 2)], axis=1)
                mixed = mixed + per_lane * o_ref_b[...].astype(F32)
        x1 = x_ref[...] + _dot(mixed, wo_ref[...])
        x1_ref[...] = x1
        xn_ref[...] = _rms(x1, g_ref[...]).astype(BF16)
        acc_ref[...] = jnp.zeros_like(acc_ref)

    u = jnp.maximum(jnp.dot(xn_ref[...], wup_ref[...], preferred_element_type=F32), 0.0)
    acc_ref[...] += _dot(u * u, wdn_ref[...])

    @pl.when(f == pl.num_programs(1) - 1)
    def _():
        y = x1_ref[...] + acc_ref[...]
        if final_norm:
            y = _rms(y, gf_ref[...])
        y_ref[...] = y


def _mlp_block(x, mix, w_o, gain, w_up, w_down, layer, gain_final, final_norm):
    t, d = x.shape
    k = w_o.shape[0]
    dff = w_up.shape[2]
    tm = min(ROW_TILE, t)
    tf = MLP_FF_TILE
    n_mix = len(mix)
    row = lambda i, f: (i, 0)
    const = lambda i, f: (0, 0)
    return pl.pallas_call(
        functools.partial(_mlp_kernel, n_mix=n_mix, final_norm=final_norm),
        grid=(t // tm, dff // tf),
        in_specs=([pl.BlockSpec((tm, d), row)]
                  + [pl.BlockSpec((tm, a.shape[1]), row) for a in mix]
                  + [pl.BlockSpec((k, d), const),
                     pl.BlockSpec((1, d), const),
                     pl.BlockSpec((None, d, tf), lambda i, f: (layer, 0, f)),
                     pl.BlockSpec((None, tf, d), lambda i, f: (layer, f, 0)),
                     pl.BlockSpec((1, d), const)]),
        out_specs=pl.BlockSpec((tm, d), row),
        out_shape=jax.ShapeDtypeStruct((t, d), F32),
        scratch_shapes=[pltpu.VMEM((tm, d), F32), pltpu.VMEM((tm, d), BF16), pltpu.VMEM((tm, d), F32)],
        compiler_params=_params("parallel", "arbitrary"),
        name="mix_out_mlp",
    )(x, *mix, w_o.astype(BF16), gain.reshape(1, d), w_up.astype(BF16), w_down.astype(BF16),
      gain_final.reshape(1, d))


def _inverse_masks(c):
    row = lax.broadcasted_iota(jnp.int32, (c, c), 0)
    col = lax.broadcasted_iota(jnp.int32, (c, c), 1)
    masks = []
    shift = 0
    while (1 << shift) < c:
        same_pair = (row >> (shift + 1)) == (col >> (shift + 1))
        other_half = (row >> shift) != (col >> shift)
        masks.append((row > col) & same_pair & other_half)
        shift += 1
    return masks


def _unit_lower_inverses(mats, c):
    masks = _inverse_masks(c)
    row = lax.broadcasted_iota(jnp.int32, (c, c), 0)
    col = lax.broadcasted_iota(jnp.int32, (c, c), 1)
    eye = (row == col).astype(F32)
    invs = [eye - jnp.where(masks[0], a, 0.0) for a in mats]
    for mask in masks[1:]:
        xs = [_dot(inv, jnp.where(mask, a, 0.0)) for inv, a in zip(invs, mats)]
        invs = [inv - _dot(x, inv) for inv, x in zip(invs, xs)]
    return invs


def _gdn_kernel(qkv_ref, z_ref, ba_ref, bat_ref, cw_ref, s0_ref, c0_ref, alr_ref, dtr_ref, alc_ref, dtc_ref,
                gn_ref, o_ref, s_ref, ext_ref, *, chunk):
    c = chunk
    ci = pl.program_id(1)
    halo = SUBLANES

    @pl.when(ci == 0)
    def _():
        s_ref[...] = s0_ref[...]
        ext_ref[...] = c0_ref[...]

    x = qkv_ref[...]
    ext = jnp.concatenate([ext_ref[...], x], axis=0)
    conv = x * cw_ref[CONV_W - 1:CONV_W, :]
    for back in range(1, CONV_W):
        conv = conv + pltpu.roll(ext, back, axis=0)[halo:] * cw_ref[CONV_W - 1 - back:CONV_W - back, :]
    ext_ref[...] = x[c - halo:]
    act = conv * _sigmoid(conv)

    ba = ba_ref[...]
    bat = bat_ref[...]
    beta_cols = _sigmoid(ba)
    g_cols = -jnp.exp(alr_ref[...]) * _softplus(ba + dtr_ref[...])
    g_rows = -jnp.exp(alc_ref[...]) * _softplus(bat + dtc_ref[...])
    row = lax.broadcasted_iota(jnp.int32, (c, c), 0)
    col = lax.broadcasted_iota(jnp.int32, (c, c), 1)
    causal = row >= col
    strict = row > col
    cum_cols = _dot_f32(causal.astype(F32), g_cols)
    cum_rows = _dot_f32(g_rows, (row <= col).astype(F32))

    heads = range(HEADS_A)
    qs, ks, vs, betas, gcs, decays = [], [], [], [], [], []
    for h in heads:
        lo = h * DK_A
        q = act[:, lo:lo + DK_A]
        k = act[:, KEY_W_A + lo:KEY_W_A + lo + DK_A]
        qs.append(q * lax.rsqrt(jnp.sum(q * q, axis=-1, keepdims=True) + EPS) * (DK_A ** -0.5))
        ks.append(k * lax.rsqrt(jnp.sum(k * k, axis=-1, keepdims=True) + EPS))
        vs.append(act[:, 2 * KEY_W_A + lo:2 * KEY_W_A + lo + DV_A])
        betas.append(beta_cols[:, h:h + 1])
        gc = cum_cols[:, HEADS_A + h:HEADS_A + h + 1]
        gr = cum_rows[HEADS_A + h:HEADS_A + h + 1, :]
        gcs.append(gc)
        decays.append(jnp.where(causal, jnp.exp(jnp.where(causal, gc - gr, 0.0)), 0.0))
    kbs = [k * b for k, b in zip(ks, betas)]
    kq = [_dot_nt(jnp.concatenate([kb, q], axis=0), k) for kb, q, k in zip(kbs, qs, ks)]
    mats = [jnp.where(strict, x[:c] * d, 0.0) for x, d in zip(kq, decays)]
    attns = [x[c:] * d for x, d in zip(kq, decays)]
    t_invs = _unit_lower_inverses(mats, c)
    e_gcs = [jnp.exp(gc) for gc in gcs]
    uws = [_dot(t, jnp.concatenate([v * b, kb * e], axis=1))
           for t, v, b, kb, e in zip(t_invs, vs, betas, kbs, e_gcs)]
    states = [s_ref[h] for h in heads]
    wq_s = [_dot(jnp.concatenate([uw[:, DV_A:], q * e], axis=0), s)
            for uw, q, e, s in zip(uws, qs, e_gcs, states)]
    v_news = [uw[:, :DV_A] - x[:c] for uw, x in zip(uws, wq_s)]
    outs = [x[c:] + _dot(attn, vn) for x, attn, vn in zip(wq_s, attns, v_news)]
    for h in heads:
        g_last = gcs[h][c - 1:c, :]
        s_ref[h] = states[h] * jnp.exp(g_last) + _dot_tn(ks[h] * jnp.exp(g_last - gcs[h]), v_news[h])
    for h in heads:
        lo = h * DV_A
        o = _rms(outs[h], gn_ref[...])
        z = z_ref[:, lo:lo + DV_A]
        o_ref[:, lo:lo + DV_A] = (o * z * _sigmoid(z)).astype(o_ref.dtype)


def _gdn(qkvz, ba, n, seq_len, chunk, s0, conv0, w_conv, a_log, dt_bias, g_norm):
    nc = seq_len // chunk
    t = n * seq_len
    halo = SUBLANES
    bat = ba[:, :2 * HEADS_A].reshape(n * nc, chunk, 2 * HEADS_A).transpose(0, 2, 1)
    c0 = jnp.pad(conv0, ((0, 0), (halo - (CONV_W - 1), 0), (0, 0)))
    pad_row = lambda p: jnp.pad(p.reshape(1, HEADS_A), ((0, 0), (HEADS_A, LANES - 2 * HEADS_A)))
    pad_col = lambda p: jnp.pad(p.reshape(HEADS_A, 1), ((HEADS_A, 0), (0, 0)))
    blk = lambda i, j: (i * nc + j, 0)
    const = lambda i, j: (0, 0)
    return pl.pallas_call(
        functools.partial(_gdn_kernel, chunk=chunk),
        grid=(n, nc),
        in_specs=[pl.BlockSpec((chunk, QKV_W_A), blk),
                  pl.BlockSpec((chunk, VAL_W_A), lambda i, j: (i * nc + j, QKV_W_A // VAL_W_A)),
                  pl.BlockSpec((chunk, LANES), blk),
                  pl.BlockSpec((None, 2 * HEADS_A, chunk), lambda i, j: (i * nc + j, 0, 0)),
                  pl.BlockSpec((CONV_W, QKV_W_A), const),
                  pl.BlockSpec((None, HEADS_A, DK_A, DV_A), lambda i, j: (i, 0, 0, 0)),
                  pl.BlockSpec((None, halo, QKV_W_A), lambda i, j: (i, 0, 0)),
                  pl.BlockSpec((1, LANES), const),
                  pl.BlockSpec((1, LANES), const),
                  pl.BlockSpec((2 * HEADS_A, 1), const),
                  pl.BlockSpec((2 * HEADS_A, 1), const),
                  pl.BlockSpec((1, DV_A), const)],
        out_specs=[pl.BlockSpec((chunk, VAL_W_A), blk),
                   pl.BlockSpec((None, HEADS_A, DK_A, DV_A), lambda i, j: (i, 0, 0, 0))],
        out_shape=[jax.ShapeDtypeStruct((t, VAL_W_A), BF16),
                   jax.ShapeDtypeStruct((n, HEADS_A, DK_A, DV_A), F32)],
        scratch_shapes=[pltpu.VMEM((halo, QKV_W_A), F32)],
        compiler_params=_params("parallel", "arbitrary"),
        name="gdn_delta_rule",
    )(qkvz, qkvz, ba, bat, w_conv, s0, c0, pad_row(a_log), pad_row(dt_bias), pad_col(a_log), pad_col(dt_bias),
      g_norm.reshape(1, DV_A))


def _gdn_step_kernel(qkv_ref, z_ref, ba_ref, st_ref, cw_ref, s0_ref, alr_ref, dtr_ref, gn_ref, o_ref, s_ref):
    nb = qkv_ref.shape[0]
    conv = qkv_ref[...] * cw_ref[CONV_W - 1:CONV_W, :]
    for j in range(CONV_W - 1):
        conv = conv + st_ref[j] * cw_ref[j:j + 1, :]
    act = conv * _sigmoid(conv)
    ba = ba_ref[...]
    beta_cols = _sigmoid(ba)
    decay_cols = jnp.exp(-jnp.exp(alr_ref[...]) * _softplus(ba + dtr_ref[...]))
    fill = jnp.zeros((DK_A - nb, DK_A), F32)
    for h in range(HEADS_A):
        lo = h * DK_A
        q = act[:, lo:lo + DK_A]
        k = act[:, KEY_W_A + lo:KEY_W_A + lo + DK_A]
        v = act[:, 2 * KEY_W_A + lo:2 * KEY_W_A + lo + DV_A]
        q = q * lax.rsqrt(jnp.sum(q * q, axis=-1, keepdims=True) + EPS) * (DK_A ** -0.5)
        k = k * lax.rsqrt(jnp.sum(k * k, axis=-1, keepdims=True) + EPS)
        q_t = jnp.concatenate([q, fill], axis=0).T
        k_t = jnp.concatenate([k, fill], axis=0).T
        z = z_ref[:, lo:lo + DV_A]
        gate = z * _sigmoid(z)
        seqs = range(nb)
        k_cols = [k_t[:, i:i + 1] for i in seqs]
        decays = [decay_cols[i:i + 1, HEADS_A + h:HEADS_A + h + 1] for i in seqs]
        k_s = [jnp.sum(k_cols[i] * s0_ref[i, h], axis=0, keepdims=True) for i in seqs]
        v_new = [beta_cols[i:i + 1, h:h + 1] * (v[i:i + 1, :] - decays[i] * k_s[i]) for i in seqs]
        s_new = [decays[i] * s0_ref[i, h] + k_cols[i] * v_new[i] for i in seqs]
        for i in seqs:
            s_ref[i, h] = s_new[i]
        outs = [jnp.sum(q_t[:, i:i + 1] * s_new[i], axis=0, keepdims=True) for i in seqs]
        for i in seqs:
            o_ref[i:i + 1, lo:lo + DV_A] = _rms(outs[i], gn_ref[...]) * gate[i:i + 1, :]


def _gdn_step(qkvz, ba, s0_layers, conv0_layers, layer, w_conv, a_log, dt_bias, g_norm):
    n = qkvz.shape[0]
    nb = SUBLANES
    pad_row = lambda p: jnp.pad(p.reshape(1, HEADS_A), ((0, 0), (HEADS_A, LANES - 2 * HEADS_A)))
    blk = lambda i: (i, 0)
    const = lambda i: (0, 0)
    return pl.pallas_call(
        _gdn_step_kernel,
        grid=(n // nb,),
        in_specs=[pl.BlockSpec((nb, QKV_W_A), blk),
                  pl.BlockSpec((nb, VAL_W_A), lambda i: (i, QKV_W_A // VAL_W_A)),
                  pl.BlockSpec((nb, LANES), blk),
                  pl.BlockSpec((None, CONV_W - 1, nb, QKV_W_A), lambda i: (layer, 0, i, 0)),
                  pl.BlockSpec((CONV_W, QKV_W_A), const),
                  pl.BlockSpec((None, nb, HEADS_A, DK_A, DV_A), lambda i: (layer, i, 0, 0, 0)),
                  pl.BlockSpec((1, LANES), const),
                  pl.BlockSpec((1, LANES), const),
                  pl.BlockSpec((1, DV_A), const)],
        out_specs=[pl.BlockSpec((nb, VAL_W_A), blk),
                   pl.BlockSpec((nb, HEADS_A, DK_A, DV_A), lambda i: (i, 0, 0, 0))],
        out_shape=[jax.ShapeDtypeStruct((n, VAL_W_A), F32),
                   jax.ShapeDtypeStruct((n, HEADS_A, DK_A, DV_A), F32)],
        compiler_params=_params("parallel"),
        name="gdn_step",
    )(qkvz, qkvz, ba, conv0_layers.transpose(0, 2, 1, 3), w_conv, s0_layers, pad_row(a_log), pad_row(dt_bias),
      g_norm.reshape(1, DV_A))


def _attn_prompt_kernel(q_ref, kvc_ref, kvp_ref, o_ref, lse_ref, *, dilation, sub_blocks):
    blk = ATT_BLOCK
    hd = HEAD_DIM_B
    a = pl.program_id(2)
    qi = lax.broadcasted_iota(jnp.int32, (blk, 2 * blk), 0)
    kj = lax.broadcasted_iota(jnp.int32, (blk, 2 * blk), 1)
    steps = blk + qi - kj
    in_window = (steps >= 0) & (steps <= blk)
    started = in_window & ((kj >= blk) | (a > 0))
    dist = (steps * dilation).astype(F32)
    low = lax.broadcasted_iota(jnp.int32, (1, 2 * hd), 1) < hd
    kv_all = jnp.concatenate([kvp_ref[...], kvc_ref[...]], axis=0)
    padded = []
    for kvh in range(KV_HEADS_B):
        own = low if kvh == 0 else ~low
        both = []
        for x in (kv_all[:, :2 * hd], kv_all[:, 2 * hd:]):
            kept = jnp.where(own, x, jnp.zeros_like(x))
            moved = pltpu.roll(kept, hd, axis=1)
            both += [kept, moved] if kvh == 0 else [moved, kept]
        padded.append(both)
    scale = hd ** -0.5
    for j in range(sub_blocks):
        valid = started if j == 0 else in_window
        rows = slice(j * blk, (j + 1) * blk)
        keys = slice(j * blk, (j + 2) * blk)
        for kvh in range(KV_HEADS_B):
            k_lo, k_hi, v_lo, v_hi = (x[keys] for x in padded[kvh])
            heads = [kvh * GQA_B + g for g in range(GQA_B)]
            qs = [q_ref[rows, (h // 2) * 2 * hd:(h // 2 + 1) * 2 * hd] for h in heads]
            ss = [_dot_nt(q, k_hi if h % 2 else k_lo) * scale - ALIBI_SLOPES[h] * dist for q, h in zip(qs, heads)]
            ss = [jnp.where(valid, s, -jnp.inf) for s in ss]
            ms = [jnp.max(s, axis=-1, keepdims=True) for s in ss]
            ps = [jnp.exp(s - m) for s, m in zip(ss, ms)]
            ls = [jnp.sum(p, axis=-1, keepdims=True) for p in ps]
            for g in range(0, GQA_B, 2):
                pair = (heads[g] // 2) * 2 * hd
                o = _dot(ps[g], v_lo) + _dot(ps[g + 1], v_hi)
                o_ref[rows, pair:pair + 2 * hd] = (o / jnp.where(low, ls[g], ls[g + 1])).astype(o_ref.dtype)
            for g, h in enumerate(heads):
                lse_ref[rows, h:h + 1] = ms[g] + jnp.log(ls[g])


def _attn_prompt_branch(q, kv, n, seq_len, dilation):
    ls = seq_len // dilation
    sub_blocks = min(4, ls // ATT_BLOCK)
    qb = sub_blocks * ATT_BLOCK
    qv = q.reshape(n, ls, dilation * Q_W_BRANCH)
    kvv = kv.reshape(n, ls, dilation * KV_W_BRANCH)
    cur = lambda i, r, a: (i, a, r)
    prev = lambda i, r, a: (i, jnp.maximum(a * sub_blocks - 1, 0), r)
    o, lse = pl.pallas_call(
        functools.partial(_attn_prompt_kernel, dilation=dilation, sub_blocks=sub_blocks),
        grid=(n, dilation, ls // qb),
        in_specs=[pl.BlockSpec((None, qb, Q_W_BRANCH), cur),
                  pl.BlockSpec((None, qb, KV_W_BRANCH), cur),
                  pl.BlockSpec((None, ATT_BLOCK, KV_W_BRANCH), prev)],
        out_specs=[pl.BlockSpec((None, qb, Q_W_BRANCH), cur),
                   pl.BlockSpec((None, None, qb, HEADS_B), lambda i, r, a: (i, r, a, 0))],
        out_shape=[jax.ShapeDtypeStruct(qv.shape, BF16),
                   jax.ShapeDtypeStruct((n, dilation, ls, HEADS_B), F32)],
        compiler_params=_params("parallel", "parallel", "arbitrary"),
        name=f"attn_prompt_d{dilation}",
    )(qv, kvv, kvv)
    lse = lse.transpose(0, 2, 1, 3).reshape(n * seq_len, HEADS_B)
    return o.reshape(n * seq_len, Q_W_BRANCH), lse


def _cache_update_kernel(c_ref, kvn_ref, o_ref, g_ref, *, window, dilation):
    nb = c_ref.shape[0]
    nk = ATT_BLOCK
    last = lax.broadcasted_iota(jnp.int32, (1, window), 1) == window - 1
    new_t = jnp.concatenate([kvn_ref[...], jnp.zeros((LANES - nb, KV_W_BRANCH), F32)], axis=0).T
    if dilation > 1:
        src = lax.broadcasted_iota(jnp.int32, (window, nk), 0)
        dst = lax.broadcasted_iota(jnp.int32, (window, nk), 1)
        pick = (src == dst * dilation).astype(BF16)
    for i in range(nb):
        x = c_ref[i].reshape(KV_W_BRANCH, window)
        shifted = jnp.where(last, new_t[:, i:i + 1], pltpu.roll(x, window - 1, axis=1))
        o_ref[i] = shifted.reshape(o_ref.shape[1:])
        seen = x.astype(BF16) if dilation == 1 else _dot(x, pick).astype(BF16)
        g_ref[i] = seen.reshape(g_ref.shape[1:])


def _cache_update(cache_t, kv_new, window, dilation):
    n = cache_t.shape[0]
    nb = max(1, min(SUBLANES, 4096 // window))
    blk = (nb, 2, KV_HEADS_B, HEAD_DIM_B, window)
    seen_blk = (nb, 2, KV_HEADS_B * HEAD_DIM_B, ATT_BLOCK)
    return pl.pallas_call(
        functools.partial(_cache_update_kernel, window=window, dilation=dilation),
        grid=(n // nb,),
        in_specs=[pl.BlockSpec(blk, lambda i: (i, 0, 0, 0, 0)),
                  pl.BlockSpec((None, nb, KV_W_BRANCH), lambda i: (i, 0, 0))],
        out_specs=[pl.BlockSpec(blk, lambda i: (i, 0, 0, 0, 0)),
                   pl.BlockSpec(seen_blk, lambda i: (i, 0, 0, 0))],
        out_shape=[jax.ShapeDtypeStruct(cache_t.shape, F32),
                   jax.ShapeDtypeStruct((n,) + seen_blk[1:], BF16)],
        compiler_params=_params("parallel"),
        name=f"cache_update_w{window}",
    )(cache_t, kv_new.reshape(n // nb, nb, KV_W_BRANCH))


def _attn_sample_kernel(q_ref, kvn_ref, g0_ref, g1_ref, g2_ref, o_ref):
    nk = ATT_BLOCK
    nb = q_ref.shape[0]
    kv_w = KV_HEADS_B * HEAD_DIM_B
    head = lax.broadcasted_iota(jnp.int32, (HEADS_B, 1), 0)
    slopes = jnp.exp2(-8.0 * (head + 1).astype(F32) / HEADS_B)
    key = lax.broadcasted_iota(jnp.int32, (1, nk), 1)
    lane = lax.broadcasted_iota(jnp.int32, (HEADS_B, kv_w), 1)
    own_half = (lane // HEAD_DIM_B) == (head // GQA_B)
    scale = HEAD_DIM_B ** -0.5
    g_refs = (g0_ref, g1_ref, g2_ref)
    pairs = [(i, b) for i in range(nb) for b in range(N_BRANCH)]
    dists = [((nk - key) * d).astype(F32) for d in BRANCH_DILATIONS]
    qs = [q_ref[i, b * HEADS_B:(b + 1) * HEADS_B, :] * scale for i, b in pairs]
    s_old = [_dot(q, g_refs[b][i, 0]) - slopes * dists[b] for q, (i, b) in zip(qs, pairs)]
    s_new = [jnp.sum(q * kvn_ref[b, i:i + 1, :kv_w], axis=-1, keepdims=True) for q, (i, b) in zip(qs, pairs)]
    ms = [jnp.maximum(jnp.max(so, axis=-1, keepdims=True), sn) for so, sn in zip(s_old, s_new)]
    p_old = [jnp.exp(so - m) for so, m in zip(s_old, ms)]
    p_new = [jnp.exp(sn - m) for sn, m in zip(s_new, ms)]
    ls = [jnp.sum(po, axis=-1, keepdims=True) + pn for po, pn in zip(p_old, p_new)]
    outs = [(_dot_nt(po, g_refs[b][i, 1]) + pn * kvn_ref[b, i:i + 1, kv_w:]) / l
            for po, pn, l, (i, b) in zip(p_old, p_new, ls, pairs)]
    lses = [m + jnp.log(l) for m, l in zip(ms, ls)]
    for i in range(nb):
        sl = slice(i * N_BRANCH, (i + 1) * N_BRANCH)
        m = functools.reduce(jnp.maximum, lses[sl])
        ws = [jnp.exp(l - m) for l in lses[sl]]
        merged = sum(w * o for w, o in zip(ws, outs[sl])) / sum(ws)
        o_ref[i] = jnp.where(own_half, merged, 0.0)


def _spread_heads(w, axis):
    w = jnp.moveaxis(w, axis, -1)
    lead = w.shape[:-1]
    w = w.reshape(lead + (-1, KV_HEADS_B, GQA_B, HEAD_DIM_B))
    halves = [jnp.pad(w[..., g, :, :], [(0, 0)] * (len(lead) + 2) + [(g * HEAD_DIM_B, (KV_HEADS_B - 1 - g) * HEAD_DIM_B)])
              for g in range(KV_HEADS_B)]
    out = jnp.stack(halves, axis=-3)
    return jnp.moveaxis(out.reshape(lead + (-1,)), -1, axis)


def _attn_sample(q, kv_new, seen):
    n = q.shape[0]
    nb = SUBLANES
    rows = N_BRANCH * HEADS_B
    seen_blk = (nb, 2, KV_HEADS_B * HEAD_DIM_B, ATT_BLOCK)
    out = pl.pallas_call(
        _attn_sample_kernel,
        grid=(n // nb,),
        in_specs=[pl.BlockSpec((nb, rows, LANES), lambda i: (i, 0, 0)),
                  pl.BlockSpec((N_BRANCH, nb, KV_W_BRANCH), lambda i: (0, i, 0))]
                 + [pl.BlockSpec(seen_blk, lambda i: (i, 0, 0, 0))] * N_BRANCH,
        out_specs=pl.BlockSpec((nb, HEADS_B, LANES), lambda i: (i, 0, 0)),
        out_shape=jax.ShapeDtypeStruct((n, HEADS_B, LANES), F32),
        compiler_params=_params("parallel"),
        name="attn_sample",
    )(q.reshape(n, rows, LANES), kv_new, *seen)
    return out.reshape(n, HEADS_B * LANES)


def _trunk(x, rec0, conv0, kv_bufs, w):
    n, seq_len, d = x.shape
    fresh = kv_bufs is None
    assert seq_len == 1 or (fresh and seq_len % (ATT_BLOCK * max(BRANCH_DILATIONS)) == 0)
    x = x.reshape(n * seq_len, d)
    rec_new, conv_new = [], []
    for l in range(N_LAYERS_A):
        qkvz, ba = _inproj(x, w['norm_mix'][l], w['w_in_a'][l])
        qkv_seq = qkvz.reshape(n, seq_len, QKVZ_W_A)[:, :, :QKV_W_A]
        conv_new.append(jnp.concatenate([conv0[l], qkv_seq], axis=1)[:, seq_len:])
        if fresh:
            o, s_new = _gdn(qkvz, ba, n, seq_len, GDN_CHUNK, rec0[l], conv0[l], w['conv_a'][l], w['a_log'][l],
                            w['dt_bias'][l], w['norm_o_a'][l])
        else:
            o, s_new = _gdn_step(qkvz, ba, rec0, conv0, l, w['conv_a'][l], w['a_log'][l], w['dt_bias'][l],
                                 w['norm_o_a'][l])
        rec_new.append(s_new)
        x = _mlp_block(x, [o], w['w_out_a'][l], w['norm_mlp'][l], w['w_up'], w['w_down'], l,
                       w['norm_final'], False)

    kv_outs = _norm_matmul(x, w['norm_kv'], w['w_kv'], KV_W_BRANCH, True, (F32, BF16) if fresh else (F32,))
    kv, kv_lo = kv_outs[0], kv_outs[-1]
    bufs_new, seen = [], []
    for b in range(N_BRANCH):
        if fresh:
            keep = min(BRANCH_WINDOWS[b], seq_len)
            last = kv[b].reshape(n, seq_len, KV_W_BRANCH)[:, seq_len - keep:]
            bufs_new.append(last.reshape(n, keep, 2, KV_HEADS_B, HEAD_DIM_B))
        else:
            assert kv_bufs[b].shape[1] == BRANCH_WINDOWS[b]
            shifted, seen_b = _cache_update(jnp.transpose(kv_bufs[b], (0, 2, 3, 4, 1)), kv[b],
                                            BRANCH_WINDOWS[b], BRANCH_DILATIONS[b])
            bufs_new.append(jnp.transpose(shifted, (0, 4, 1, 2, 3)))
            seen.append(seen_b)

    for l in range(N_LAYERS_A, N_LAYERS_A + N_LAYERS_B):
        lb = l - N_LAYERS_A
        if fresh:
            q, = _norm_matmul(x, w['norm_mix'][l], w['w_q_b'][lb], Q_W_BRANCH, True, (BF16,))
            parts = [_attn_prompt_branch(q[b], kv_lo[b], n, seq_len, BRANCH_DILATIONS[b]) for b in range(N_BRANCH)]
            mix = [p[0] for p in parts] + [p[1] for p in parts]
            w_o = w['w_o_b'][lb]
        else:
            q, = _norm_matmul(x, w['norm_mix'][l], _spread_heads(w['w_q_b'][lb], 1), 1024, False, (F32,))
            mix = [_attn_sample(q, jnp.stack(kv), seen)]
            w_o = _spread_heads(w['w_o_b'][lb], 0)
        x = _mlp_block(x, mix, w_o, w['norm_mlp'][l], w['w_up'], w['w_down'], l,
                       w['norm_final'], l == N_LAYERS_A + N_LAYERS_B - 1)
    return (x.reshape(n, seq_len, d), jnp.stack(rec_new), jnp.stack(conv_new), *bufs_new)


def kernel(x_prompt, x_sample, state_a_rec, state_a_conv, cache_b0_kv, cache_b1_kv, cache_b2_kv, norm_mix, norm_mlp, w_in_a, conv_a, a_log, dt_bias, norm_o_a, w_out_a, norm_kv, w_kv, w_q_b, w_o_b, w_up, w_down, norm_final):
    w = {'norm_mix': norm_mix, 'norm_mlp': norm_mlp, 'w_in_a': w_in_a, 'conv_a': conv_a, 'a_log': a_log,
         'dt_bias': dt_bias, 'norm_o_a': norm_o_a, 'w_out_a': w_out_a, 'norm_kv': norm_kv, 'w_kv': w_kv,
         'w_q_b': w_q_b, 'w_o_b': w_o_b, 'w_up': w_up, 'w_down': w_down, 'norm_final': norm_final}
    n_p = x_prompt.shape[0]
    p_rec0 = jnp.zeros((N_LAYERS_A, n_p, HEADS_A, DK_A, DV_A), state_a_rec.dtype)
    p_conv0 = jnp.zeros((N_LAYERS_A, n_p, CONV_W - 1, QKV_W_A), state_a_conv.dtype)
    prompt = _trunk(x_prompt, p_rec0, p_conv0, None, w)
    sample = _trunk(x_sample, state_a_rec, state_a_conv, [cache_b0_kv, cache_b1_kv, cache_b2_kv], w)
    return (prompt[0], sample[0], *prompt[1:], *sample[1:])
```

```python
import functools

import jax
import jax.numpy as jnp
from jax import lax
from jax.experimental import pallas as pl
from jax.experimental.pallas import tpu as pltpu

D_MODEL = 1024
HEADS_A = 8
DK_A = 128
DV_A = 128
KEY_W_A = HEADS_A * DK_A
VAL_W_A = HEADS_A * DV_A
QKV_W_A = 2 * KEY_W_A + VAL_W_A
QKVZ_W_A = QKV_W_A + VAL_W_A
CONV_W = 4
N_LAYERS_A = 2
N_LAYERS_B = 2
BRANCH_WINDOWS = (128, 512, 2048)
BRANCH_DILATIONS = (1, 4, 16)
N_BRANCH = 3
HEADS_B = 8
KV_HEADS_B = 2
GQA_B = HEADS_B // KV_HEADS_B
HEAD_DIM_B = 64
Q_W_BRANCH = HEADS_B * HEAD_DIM_B
KV_W_BRANCH = 2 * KV_HEADS_B * HEAD_DIM_B
ALIBI_SLOPES = tuple(2.0 ** (-8.0 * h / HEADS_B) for h in range(1, HEADS_B + 1))
EPS = 1e-6

LANES = 128
SUBLANES = 8
VMEM_LIMIT_BYTES = 56 * 1024 * 1024
ROW_TILE = 1024
MLP_FF_TILE = 512
GDN_CHUNK = 128
ATT_BLOCK = 128

F32 = jnp.float32
BF16 = jnp.bfloat16


def _params(*sem):
    return pltpu.CompilerParams(dimension_semantics=sem, vmem_limit_bytes=VMEM_LIMIT_BYTES)


def _rms(x, gain):
    return x * lax.rsqrt(jnp.mean(x * x, axis=-1, keepdims=True) + EPS) * gain


def _sigmoid(x):
    return 1.0 / (1.0 + jnp.exp(-x))


def _softplus(x):
    return jnp.maximum(x, 0.0) + jnp.log(1.0 + jnp.exp(-jnp.abs(x)))


def _dot(a, b):
    return jnp.dot(a.astype(BF16), b.astype(BF16), preferred_element_type=F32)


def _dot_nt(a, b):
    return lax.dot_general(a.astype(BF16), b.astype(BF16), (((1,), (1,)), ((), ())),
                           preferred_element_type=F32)


def _dot_tn(a, b):
    return lax.dot_general(a.astype(BF16), b.astype(BF16), (((0,), (0,)), ((), ())),
                           preferred_element_type=F32)


def _dot_f32(a, b):
    return jnp.dot(a, b, precision=lax.Precision.HIGHEST, preferred_element_type=F32)


def _norm_matmul_kernel(x_ref, g_ref, w_ref, *rest, groups):
    *o_refs, xn_ref = rest
    j = pl.program_id(1)

    @pl.when(j == 0)
    def _():
        xn_ref[...] = _rms(x_ref[...], g_ref[...]).astype(BF16)

    y = jnp.dot(xn_ref[...], w_ref[...], preferred_element_type=F32)
    if groups == 1:
        for o_ref in o_refs:
            o_ref[...] = y.astype(o_ref.dtype)
    else:
        for g in range(groups):
            @pl.when(j == g)
            def _():
                for o_ref in o_refs[g::groups]:
                    o_ref[...] = y.astype(o_ref.dtype)


def _norm_matmul(x, gain, w, tn, split, dtypes):
    t, d = x.shape
    f = w.shape[1]
    tm = min(ROW_TILE, t)
    nj = f // tn
    if split:
        out_specs = [pl.BlockSpec((tm, tn), lambda i, j: (i, 0))] * (nj * len(dtypes))
        out_shape = [jax.ShapeDtypeStruct((t, tn), dt) for dt in dtypes for _ in range(nj)]
    else:
        out_specs = [pl.BlockSpec((tm, tn), lambda i, j: (i, j))] * len(dtypes)
        out_shape = [jax.ShapeDtypeStruct((t, f), dt) for dt in dtypes]
    outs = pl.pallas_call(
        functools.partial(_norm_matmul_kernel, groups=nj if split else 1),
        grid=(t // tm, nj),
        in_specs=[pl.BlockSpec((tm, d), lambda i, j: (i, 0)),
                  pl.BlockSpec((1, d), lambda i, j: (0, 0)),
                  pl.BlockSpec((d, tn), lambda i, j: (0, j))],
        out_specs=out_specs,
        out_shape=out_shape,
        scratch_shapes=[pltpu.VMEM((tm, d), BF16)],
        compiler_params=_params("parallel", "arbitrary"),
        name="norm_matmul",
    )(x, gain.reshape(1, d), w.astype(BF16))
    if split:
        return [outs[k * nj:(k + 1) * nj] for k in range(len(dtypes))]
    return outs


def _branch_proj_kernel(x_ref, g_ref, w_ref, *rest, keep_f32):
    *o_refs, xn_ref, y_ref = rest
    chunks, tm, _ = y_ref.shape
    tn = chunks * LANES
    j = pl.program_id(1)

    @pl.when(j == 0)
    def _():
        xn_ref[...] = _rms(x_ref[...], g_ref[...]).astype(BF16)

    y = jnp.dot(xn_ref[...], w_ref[...], preferred_element_type=F32)
    for b, dil in enumerate(BRANCH_DILATIONS):
        @pl.when(j == b)
        def _():
            if keep_f32:
                o_refs[N_BRANCH + b][...] = y
            if dil == 1:
                o_refs[b][...] = y.astype(BF16)
            else:
                for c in range(chunks):
                    y_ref[c] = y[:, c * LANES:(c + 1) * LANES]
                for r in range(dil):
                    for c in range(chunks):
                        lo = r * tn + c * LANES
                        o_refs[b][:, lo:lo + LANES] = y_ref[c, pl.ds(r, tm // dil, stride=dil), :].astype(BF16)


def _branch_proj(x, gain, w, keep_f32):
    t, d = x.shape
    tn = w.shape[1] // N_BRANCH
    tm = min(ROW_TILE, t)
    row = lambda i, j: (i, 0)
    out_specs = [pl.BlockSpec((tm // dil, dil * tn), row) for dil in BRANCH_DILATIONS]
    out_shape = [jax.ShapeDtypeStruct((t // dil, dil * tn), BF16) for dil in BRANCH_DILATIONS]
    if keep_f32:
        out_specs += [pl.BlockSpec((tm, tn), row)] * N_BRANCH
        out_shape += [jax.ShapeDtypeStruct((t, tn), F32)] * N_BRANCH
    outs = pl.pallas_call(
        functools.partial(_branch_proj_kernel, keep_f32=keep_f32),
        grid=(t // tm, N_BRANCH),
        in_specs=[pl.BlockSpec((tm, d), row),
                  pl.BlockSpec((1, d), lambda i, j: (0, 0)),
                  pl.BlockSpec((d, tn), lambda i, j: (0, j))],
        out_specs=out_specs,
        out_shape=out_shape,
        scratch_shapes=[pltpu.VMEM((tm, d), BF16), pltpu.VMEM((tn // LANES, tm, LANES), F32)],
        compiler_params=_params("parallel", "arbitrary"),
        name="branch_proj",
    )(x, gain.reshape(1, d), w.astype(BF16))
    return outs[:N_BRANCH], outs[N_BRANCH:]


def _inproj_kernel(x_ref, g_ref, w_ref, wg_ref, o_ref, og_ref, xn_ref):
    @pl.when(pl.program_id(1) == 0)
    def _():
        xn = _rms(x_ref[...], g_ref[...]).astype(BF16)
        xn_ref[...] = xn
        og_ref[...] = jnp.dot(xn, wg_ref[...], preferred_element_type=F32)

    o_ref[...] = jnp.dot(xn_ref[...], w_ref[...], preferred_element_type=F32)


def _inproj(x, gain, w_in):
    t, d = x.shape
    tm = min(ROW_TILE, t)
    tn = 1024
    w_main = w_in[:, :QKVZ_W_A].astype(BF16)
    w_gate = jnp.pad(w_in[:, QKVZ_W_A:], ((0, 0), (0, LANES - 2 * HEADS_A))).astype(BF16)
    return pl.pallas_call(
        _inproj_kernel,
        grid=(t // tm, QKVZ_W_A // tn),
        in_specs=[pl.BlockSpec((tm, d), lambda i, j: (i, 0)),
                  pl.BlockSpec((1, d), lambda i, j: (0, 0)),
                  pl.BlockSpec((d, tn), lambda i, j: (0, j)),
                  pl.BlockSpec((d, LANES), lambda i, j: (0, 0))],
        out_specs=[pl.BlockSpec((tm, tn), lambda i, j: (i, j)),
                   pl.BlockSpec((tm, LANES), lambda i, j: (i, 0))],
        out_shape=[jax.ShapeDtypeStruct((t, QKVZ_W_A), F32),
                   jax.ShapeDtypeStruct((t, LANES), F32)],
        scratch_shapes=[pltpu.VMEM((tm, d), BF16)],
        compiler_params=_params("parallel", "arbitrary"),
        name="gdn_inproj",
    )(x, gain.reshape(1, d), w_main, w_gate)


def _mlp_kernel(*refs, n_mix, final_norm):
    x_ref = refs[0]
    mix_refs = refs[1:1 + n_mix]
    wo_ref, g_ref, wup_ref, wdn_ref, gf_ref, y_ref, xn_ref = refs[1 + n_mix:1 + n_mix + 7]
    nat_refs = refs[1 + n_mix + 7:]
    f = pl.program_id(1)

    @pl.when(f == 0)
    def _():
        if n_mix == 1:
            mixed = mix_refs[0][...]
        else:
            tm = x_ref.shape[0]
            chunks = Q_W_BRANCH // LANES
            for b, dil in enumerate(BRANCH_DILATIONS):
                o_blk, lse_blk = mix_refs[b], mix_refs[N_BRANCH + b]
                o_nat, lse_nat = nat_refs[b], nat_refs[N_BRANCH + b]
                rows = tm // dil
                for r in range(dil):
                    dst = pl.ds(r, rows, stride=dil) if dil > 1 else pl.ds(0, rows)
                    for c in range(chunks):
                        lo = r * Q_W_BRANCH + c * LANES
                        o_nat[c, dst, :] = o_blk[:, lo:lo + LANES].astype(F32)
                    lse_nat[dst, :] = lse_blk[r]
            lses = [r[...] for r in nat_refs[N_BRANCH:]]
            m = functools.reduce(jnp.maximum, lses)
            ws = [jnp.exp(l - m) for l in lses]
            total = sum(ws)
            low = lax.broadcasted_iota(jnp.int32, (1, 2 * HEAD_DIM_B), 1) < HEAD_DIM_B
            mixed = 0.0
            for w, o_nat in zip(ws, nat_refs[:N_BRANCH]):
                w = w / total
                mixed = mixed + jnp.concatenate(
                    [jnp.where(low, w[:, 2 * p:2 * p + 1], w[:, 2 * p + 1:2 * p + 2]) * o_nat[p]
                     for p in range(HEADS_B // 2)], axis=1)
        x1 = x_ref[...] + _dot(mixed, wo_ref[...])
        y_ref[...] = x1
        xn_ref[...] = _rms(x1, g_ref[...]).astype(BF16)

    u = jnp.maximum(jnp.dot(xn_ref[...], wup_ref[...], preferred_element_type=F32), 0.0)
    y_ref[...] += _dot(u * u, wdn_ref[...])

    if final_norm:
        @pl.when(f == pl.num_programs(1) - 1)
        def _():
            y_ref[...] = _rms(y_ref[...], gf_ref[...])


def _mlp_block(x, mix, w_o, gain, w_up, w_down, layer, gain_final, final_norm):
    t, d = x.shape
    k = w_o.shape[0]
    dff = w_up.shape[2]
    tm = min(ROW_TILE, t)
    tf = MLP_FF_TILE
    n_mix = len(mix)
    row = lambda i, f: (i, 0)
    const = lambda i, f: (0, 0)
    scratch = [pltpu.VMEM((tm, d), BF16)]
    if n_mix == 1:
        mix_specs = [pl.BlockSpec((tm, k), row)]
    else:
        mix_specs = [pl.BlockSpec((tm // dil, dil * Q_W_BRANCH), row) for dil in BRANCH_DILATIONS]
        for dil, lse in zip(BRANCH_DILATIONS, mix[N_BRANCH:]):
            per_seq = lse.shape[2] // (tm // dil)
            mix_specs.append(pl.BlockSpec((None, dil, tm // dil, LANES),
                                          lambda i, f, per_seq=per_seq: (i // per_seq, 0, i % per_seq, 0)))
        scratch += ([pltpu.VMEM((Q_W_BRANCH // LANES, tm, LANES), F32)] * N_BRANCH
                    + [pltpu.VMEM((tm, LANES), F32)] * N_BRANCH)
    return pl.pallas_call(
        functools.partial(_mlp_kernel, n_mix=n_mix, final_norm=final_norm),
        grid=(t // tm, dff // tf),
        in_specs=([pl.BlockSpec((tm, d), row)] + mix_specs
                  + [pl.BlockSpec((k, d), const),
                     pl.BlockSpec((1, d), const),
                     pl.BlockSpec((None, d, tf), lambda i, f: (layer, 0, f)),
                     pl.BlockSpec((None, tf, d), lambda i, f: (layer, f, 0)),
                     pl.BlockSpec((1, d), const)]),
        out_specs=pl.BlockSpec((tm, d), row),
        out_shape=jax.ShapeDtypeStruct((t, d), F32),
        scratch_shapes=scratch,
        compiler_params=_params("parallel", "arbitrary"),
        name="mix_out_mlp",
    )(x, *mix, w_o.astype(BF16), gain.reshape(1, d), w_up.astype(BF16), w_down.astype(BF16),
      gain_final.reshape(1, d))


def _inverse_masks(c):
    row = lax.broadcasted_iota(jnp.int32, (c, c), 0)
    col = lax.broadcasted_iota(jnp.int32, (c, c), 1)
    masks = []
    shift = 0
    while (1 << shift) < c:
        same_pair = (row >> (shift + 1)) == (col >> (shift + 1))
        other_half = (row >> shift) != (col >> shift)
        masks.append((row > col) & same_pair & other_half)
        shift += 1
    return masks


def _unit_lower_inverses(mats, c):
    masks = _inverse_masks(c)
    row = lax.broadcasted_iota(jnp.int32, (c, c), 0)
    col = lax.broadcasted_iota(jnp.int32, (c, c), 1)
    eye = (row == col).astype(F32)
    invs = [eye - jnp.where(masks[0], a, 0.0) for a in mats]
    for mask in masks[1:]:
        xs = [_dot(inv, jnp.where(mask, a, 0.0)) for inv, a in zip(invs, mats)]
        invs = [inv - _dot(x, inv) for inv, x in zip(invs, xs)]
    return invs


def _gdn_kernel(qkv_ref, z_ref, ba_ref, bat_ref, cw_ref, s0_ref, c0_ref, alr_ref, dtr_ref, alc_ref, dtc_ref,
                gn_ref, o_ref, s_ref, ext_ref, *, chunk):
    c = chunk
    ci = pl.program_id(1)
    halo = SUBLANES

    @pl.when(ci == 0)
    def _():
        s_ref[...] = s0_ref[...]
        ext_ref[...] = c0_ref[...]

    x = qkv_ref[...]
    ext = jnp.concatenate([ext_ref[...], x], axis=0)
    conv = x * cw_ref[CONV_W - 1:CONV_W, :]
    for back in range(1, CONV_W):
        conv = conv + pltpu.roll(ext, back, axis=0)[halo:] * cw_ref[CONV_W - 1 - back:CONV_W - back, :]
    ext_ref[...] = x[c - halo:]
    act = conv * _sigmoid(conv)

    ba = ba_ref[...]
    bat = bat_ref[...]
    beta_cols = _sigmoid(ba)
    g_cols = -jnp.exp(alr_ref[...]) * _softplus(ba + dtr_ref[...])
    g_rows = -jnp.exp(alc_ref[...]) * _softplus(bat + dtc_ref[...])
    row = lax.broadcasted_iota(jnp.int32, (c, c), 0)
    col = lax.broadcasted_iota(jnp.int32, (c, c), 1)
    causal = row >= col
    strict = row > col
    cum_cols = _dot_f32(causal.astype(F32), g_cols)
    cum_rows = _dot_f32(g_rows, (row <= col).astype(F32))

    heads = range(HEADS_A)
    qs, ks, vs, betas, gcs, decays = [], [], [], [], [], []
    for h in heads:
        lo = h * DK_A
        q = act[:, lo:lo + DK_A]
        k = act[:, KEY_W_A + lo:KEY_W_A + lo + DK_A]
        qs.append(q * lax.rsqrt(jnp.sum(q * q, axis=-1, keepdims=True) + EPS) * (DK_A ** -0.5))
        ks.append(k * lax.rsqrt(jnp.sum(k * k, axis=-1, keepdims=True) + EPS))
        vs.append(act[:, 2 * KEY_W_A + lo:2 * KEY_W_A + lo + DV_A])
        betas.append(beta_cols[:, h:h + 1])
        gc = cum_cols[:, HEADS_A + h:HEADS_A + h + 1]
        gr = cum_rows[HEADS_A + h:HEADS_A + h + 1, :]
        gcs.append(gc)
        decays.append(jnp.where(causal, jnp.exp(jnp.where(causal, gc - gr, 0.0)), 0.0))
    kbs = [k * b for k, b in zip(ks, betas)]
    kq = [_dot_nt(jnp.concatenate([kb, q], axis=0), k) for kb, q, k in zip(kbs, qs, ks)]
    mats = [jnp.where(strict, x[:c] * d, 0.0) for x, d in zip(kq, decays)]
    attns = [x[c:] * d for x, d in zip(kq, decays)]
    t_invs = _unit_lower_inverses(mats, c)
    e_gcs = [jnp.exp(gc) for gc in gcs]
    uws = [_dot(t, jnp.concatenate([v * b, kb * e], axis=1))
           for t, v, b, kb, e in zip(t_invs, vs, betas, kbs, e_gcs)]
    states = [s_ref[h] for h in heads]
    wq_s = [_dot(jnp.concatenate([uw[:, DV_A:], q * e], axis=0), s)
            for uw, q, e, s in zip(uws, qs, e_gcs, states)]
    v_news = [uw[:, :DV_A] - x[:c] for uw, x in zip(uws, wq_s)]
    outs = [x[c:] + _dot(attn, vn) for x, attn, vn in zip(wq_s, attns, v_news)]
    for h in heads:
        g_last = gcs[h][c - 1:c, :]
        s_ref[h] = states[h] * jnp.exp(g_last) + _dot_tn(ks[h] * jnp.exp(g_last - gcs[h]), v_news[h])
    for h in heads:
        lo = h * DV_A
        o = _rms(outs[h], gn_ref[...])
        z = z_ref[:, lo:lo + DV_A]
        o_ref[:, lo:lo + DV_A] = (o * z * _sigmoid(z)).astype(o_ref.dtype)


def _gdn(qkvz, ba, n, seq_len, chunk, s0, conv0, w_conv, a_log, dt_bias, g_norm):
    nc = seq_len // chunk
    t = n * seq_len
    halo = SUBLANES
    bat = ba[:, :2 * HEADS_A].reshape(n * nc, chunk, 2 * HEADS_A).transpose(0, 2, 1)
    c0 = jnp.pad(conv0, ((0, 0), (halo - (CONV_W - 1), 0), (0, 0)))
    pad_row = lambda p: jnp.pad(p.reshape(1, HEADS_A), ((0, 0), (HEADS_A, LANES - 2 * HEADS_A)))
    pad_col = lambda p: jnp.pad(p.reshape(HEADS_A, 1), ((HEADS_A, 0), (0, 0)))
    blk = lambda i, j: (i * nc + j, 0)
    const = lambda i, j: (0, 0)
    return pl.pallas_call(
        functools.partial(_gdn_kernel, chunk=chunk),
        grid=(n, nc),
        in_specs=[pl.BlockSpec((chunk, QKV_W_A), blk),
                  pl.BlockSpec((chunk, VAL_W_A), lambda i, j: (i * nc + j, QKV_W_A // VAL_W_A)),
                  pl.BlockSpec((chunk, LANES), blk),
                  pl.BlockSpec((None, 2 * HEADS_A, chunk), lambda i, j: (i * nc + j, 0, 0)),
                  pl.BlockSpec((CONV_W, QKV_W_A), const),
                  pl.BlockSpec((None, HEADS_A, DK_A, DV_A), lambda i, j: (i, 0, 0, 0)),
                  pl.BlockSpec((None, halo, QKV_W_A), lambda i, j: (i, 0, 0)),
                  pl.BlockSpec((1, LANES), const),
                  pl.BlockSpec((1, LANES), const),
                  pl.BlockSpec((2 * HEADS_A, 1), const),
                  pl.BlockSpec((2 * HEADS_A, 1), const),
                  pl.BlockSpec((1, DV_A), const)],
        out_specs=[pl.BlockSpec((chunk, VAL_W_A), blk),
                   pl.BlockSpec((None, HEADS_A, DK_A, DV_A), lambda i, j: (i, 0, 0, 0))],
        out_shape=[jax.ShapeDtypeStruct((t, VAL_W_A), BF16),
                   jax.ShapeDtypeStruct((n, HEADS_A, DK_A, DV_A), F32)],
        scratch_shapes=[pltpu.VMEM((halo, QKV_W_A), F32)],
        compiler_params=_params("parallel", "arbitrary"),
        name="gdn_delta_rule",
    )(qkvz, qkvz, ba, bat, w_conv, s0, c0, pad_row(a_log), pad_row(dt_bias), pad_col(a_log), pad_col(dt_bias),
      g_norm.reshape(1, DV_A))


def _gdn_step_kernel(qkv_ref, z_ref, ba_ref, st_ref, cw_ref, s0_ref, alr_ref, dtr_ref, gn_ref, o_ref, s_ref):
    nb = qkv_ref.shape[0]
    conv = qkv_ref[...] * cw_ref[CONV_W - 1:CONV_W, :]
    for j in range(CONV_W - 1):
        conv = conv + st_ref[j] * cw_ref[j:j + 1, :]
    act = conv * _sigmoid(conv)
    ba = ba_ref[...]
    beta_cols = _sigmoid(ba)
    decay_cols = jnp.exp(-jnp.exp(alr_ref[...]) * _softplus(ba + dtr_ref[...]))
    fill = jnp.zeros((DK_A - nb, DK_A), F32)
    for h in range(HEADS_A):
        lo = h * DK_A
        q = act[:, lo:lo + DK_A]
        k = act[:, KEY_W_A + lo:KEY_W_A + lo + DK_A]
        v = act[:, 2 * KEY_W_A + lo:2 * KEY_W_A + lo + DV_A]
        q = q * lax.rsqrt(jnp.sum(q * q, axis=-1, keepdims=True) + EPS) * (DK_A ** -0.5)
        k = k * lax.rsqrt(jnp.sum(k * k, axis=-1, keepdims=True) + EPS)
        q_t = jnp.concatenate([q, fill], axis=0).T
        k_t = jnp.concatenate([k, fill], axis=0).T
        z = z_ref[:, lo:lo + DV_A]
        gate = z * _sigmoid(z)
        seqs = range(nb)
        k_cols = [k_t[:, i:i + 1] for i in seqs]
        decays = [decay_cols[i:i + 1, HEADS_A + h:HEADS_A + h + 1] for i in seqs]
        k_s = [jnp.sum(k_cols[i] * s0_ref[i, h], axis=0, keepdims=True) for i in seqs]
        v_new = [beta_cols[i:i + 1, h:h + 1] * (v[i:i + 1, :] - decays[i] * k_s[i]) for i in seqs]
        s_new = [decays[i] * s0_ref[i, h] + k_cols[i] * v_new[i] for i in seqs]
        for i in seqs:
            s_ref[i, h] = s_new[i]
        outs = [jnp.sum(q_t[:, i:i + 1] * s_new[i], axis=0, keepdims=True) for i in seqs]
        for i in seqs:
            o_ref[i:i + 1, lo:lo + DV_A] = _rms(outs[i], gn_ref[...]) * gate[i:i + 1, :]


def _gdn_step(qkvz, ba, s0_layers, conv0_layers, layer, w_conv, a_log, dt_bias, g_norm):
    n = qkvz.shape[0]
    nb = SUBLANES
    pad_row = lambda p: jnp.pad(p.reshape(1, HEADS_A), ((0, 0), (HEADS_A, LANES - 2 * HEADS_A)))
    blk = lambda i: (i, 0)
    const = lambda i: (0, 0)
    return pl.pallas_call(
        _gdn_step_kernel,
        grid=(n // nb,),
        in_specs=[pl.BlockSpec((nb, QKV_W_A), blk),
                  pl.BlockSpec((nb, VAL_W_A), lambda i: (i, QKV_W_A // VAL_W_A)),
                  pl.BlockSpec((nb, LANES), blk),
                  pl.BlockSpec((None, CONV_W - 1, nb, QKV_W_A), lambda i: (layer, 0, i, 0)),
                  pl.BlockSpec((CONV_W, QKV_W_A), const),
                  pl.BlockSpec((None, nb, HEADS_A, DK_A, DV_A), lambda i: (layer, i, 0, 0, 0)),
                  pl.BlockSpec((1, LANES), const),
                  pl.BlockSpec((1, LANES), const),
                  pl.BlockSpec((1, DV_A), const)],
        out_specs=[pl.BlockSpec((nb, VAL_W_A), blk),
                   pl.BlockSpec((nb, HEADS_A, DK_A, DV_A), lambda i: (i, 0, 0, 0))],
        out_shape=[jax.ShapeDtypeStruct((n, VAL_W_A), F32),
                   jax.ShapeDtypeStruct((n, HEADS_A, DK_A, DV_A), F32)],
        compiler_params=_params("parallel"),
        name="gdn_step",
    )(qkvz, qkvz, ba, conv0_layers.transpose(0, 2, 1, 3), w_conv, s0_layers, pad_row(a_log), pad_row(dt_bias),
      g_norm.reshape(1, DV_A))


def _attn_prompt_kernel(q_ref, kvc_ref, kvp_ref, o_ref, lse_ref, *, dilation, sub_blocks):
    blk = ATT_BLOCK
    hd = HEAD_DIM_B
    a = pl.program_id(2)
    qi = lax.broadcasted_iota(jnp.int32, (blk, 2 * blk), 0)
    kj = lax.broadcasted_iota(jnp.int32, (blk, 2 * blk), 1)
    steps = blk + qi - kj
    in_window = (steps >= 0) & (steps <= blk)
    started = in_window & ((kj >= blk) | (a > 0))
    dist = (steps * dilation).astype(F32)
    low = lax.broadcasted_iota(jnp.int32, (1, 2 * hd), 1) < hd
    kv_all = jnp.concatenate([kvp_ref[...], kvc_ref[...]], axis=0)
    padded = []
    for kvh in range(KV_HEADS_B):
        own = low if kvh == 0 else ~low
        both = []
        for x in (kv_all[:, :2 * hd], kv_all[:, 2 * hd:]):
            kept = jnp.where(own, x, jnp.zeros_like(x))
            moved = pltpu.roll(kept, hd, axis=1)
            both += [kept, moved] if kvh == 0 else [moved, kept]
        padded.append(both)
    scale = hd ** -0.5
    head_lane = lax.broadcasted_iota(jnp.int32, (1, LANES), 1)
    for j in range(sub_blocks):
        lse_all = jnp.zeros((blk, LANES), F32)
        valid = started if j == 0 else in_window
        rows = slice(j * blk, (j + 1) * blk)
        keys = slice(j * blk, (j + 2) * blk)
        for kvh in range(KV_HEADS_B):
            k_lo, k_hi, v_lo, v_hi = (x[keys] for x in padded[kvh])
            heads = [kvh * GQA_B + g for g in range(GQA_B)]
            qs = [q_ref[rows, (h // 2) * 2 * hd:(h // 2 + 1) * 2 * hd] for h in heads]
            ss = [_dot_nt(q, k_hi if h % 2 else k_lo) * scale - ALIBI_SLOPES[h] * dist for q, h in zip(qs, heads)]
            ss = [jnp.where(valid, s, -jnp.inf) for s in ss]
            ms = [jnp.max(s, axis=-1, keepdims=True) for s in ss]
            ps = [jnp.exp(s - m) for s, m in zip(ss, ms)]
            ls = [jnp.sum(p, axis=-1, keepdims=True) for p in ps]
            for g in range(0, GQA_B, 2):
                pair = (heads[g] // 2) * 2 * hd
                o = _dot(ps[g], v_lo) + _dot(ps[g + 1], v_hi)
                o_ref[rows, pair:pair + 2 * hd] = (o / jnp.where(low, ls[g], ls[g + 1])).astype(o_ref.dtype)
            for g, h in enumerate(heads):
                lse_all = jnp.where(head_lane == h, ms[g] + jnp.log(ls[g]), lse_all)
        lse_ref[rows, :] = lse_all


def _attn_prompt_branch(q, kv, n, seq_len, dilation):
    ls = seq_len // dilation
    sub_blocks = min(4, ls // ATT_BLOCK)
    qb = sub_blocks * ATT_BLOCK
    qv = q.reshape(n, ls, dilation * Q_W_BRANCH)
    kvv = kv.reshape(n, ls, dilation * KV_W_BRANCH)
    cur = lambda i, r, a: (i, a, r)
    prev = lambda i, r, a: (i, jnp.maximum(a * sub_blocks - 1, 0), r)
    o, lse = pl.pallas_call(
        functools.partial(_attn_prompt_kernel, dilation=dilation, sub_blocks=sub_blocks),
        grid=(n, dilation, ls // qb),
        in_specs=[pl.BlockSpec((None, qb, Q_W_BRANCH), cur),
                  pl.BlockSpec((None, qb, KV_W_BRANCH), cur),
                  pl.BlockSpec((None, ATT_BLOCK, KV_W_BRANCH), prev)],
        out_specs=[pl.BlockSpec((None, qb, Q_W_BRANCH), cur),
                   pl.BlockSpec((None, None, qb, LANES), lambda i, r, a: (i, r, a, 0))],
        out_shape=[jax.ShapeDtypeStruct(qv.shape, BF16),
                   jax.ShapeDtypeStruct((n, dilation, ls, LANES), F32)],
        compiler_params=_params("parallel", "parallel", "arbitrary"),
        name=f"attn_prompt_d{dilation}",
    )(qv, kvv, kvv)
    return o.reshape(n * ls, dilation * Q_W_BRANCH), lse


def _cache_update_kernel(c_ref, kvn_ref, o_ref, g_ref, *, window, dilation):
    nb = c_ref.shape[0]
    nk = ATT_BLOCK
    last = lax.broadcasted_iota(jnp.int32, (1, window), 1) == window - 1
    new_t = jnp.concatenate([kvn_ref[...], jnp.zeros((LANES - nb, KV_W_BRANCH), F32)], axis=0).T
    if dilation > 1:
        src = lax.broadcasted_iota(jnp.int32, (window, nk), 0)
        dst = lax.broadcasted_iota(jnp.int32, (window, nk), 1)
        pick = (src == dst * dilation).astype(BF16)
    for i in range(nb):
        x = c_ref[i].reshape(KV_W_BRANCH, window)
        shifted = jnp.where(last, new_t[:, i:i + 1], pltpu.roll(x, window - 1, axis=1))
        o_ref[i] = shifted.reshape(o_ref.shape[1:])
        seen = x.astype(BF16) if dilation == 1 else _dot(x, pick).astype(BF16)
        g_ref[i] = seen.reshape(g_ref.shape[1:])


def _cache_update(cache_t, kv_new, window, dilation):
    n = cache_t.shape[0]
    nb = max(1, min(SUBLANES, 4096 // window))
    blk = (nb, 2, KV_HEADS_B, HEAD_DIM_B, window)
    seen_blk = (nb, 2, KV_HEADS_B * HEAD_DIM_B, ATT_BLOCK)
    return pl.pallas_call(
        functools.partial(_cache_update_kernel, window=window, dilation=dilation),
        grid=(n // nb,),
        in_specs=[pl.BlockSpec(blk, lambda i: (i, 0, 0, 0, 0)),
                  pl.BlockSpec((None, nb, KV_W_BRANCH), lambda i: (i, 0, 0))],
        out_specs=[pl.BlockSpec(blk, lambda i: (i, 0, 0, 0, 0)),
                   pl.BlockSpec(seen_blk, lambda i: (i, 0, 0, 0))],
        out_shape=[jax.ShapeDtypeStruct(cache_t.shape, F32),
                   jax.ShapeDtypeStruct((n,) + seen_blk[1:], BF16)],
        compiler_params=_params("parallel"),
        name=f"cache_update_w{window}",
    )(cache_t, kv_new.reshape(n // nb, nb, KV_W_BRANCH))


def _attn_sample_kernel(q_ref, kvn_ref, g0_ref, g1_ref, g2_ref, o_ref):
    nk = ATT_BLOCK
    nb = q_ref.shape[0]
    kv_w = KV_HEADS_B * HEAD_DIM_B
    head = lax.broadcasted_iota(jnp.int32, (HEADS_B, 1), 0)
    slopes = jnp.exp2(-8.0 * (head + 1).astype(F32) / HEADS_B)
    key = lax.broadcasted_iota(jnp.int32, (1, nk), 1)
    lane = lax.broadcasted_iota(jnp.int32, (HEADS_B, kv_w), 1)
    own_half = (lane // HEAD_DIM_B) == (head // GQA_B)
    scale = HEAD_DIM_B ** -0.5
    g_refs = (g0_ref, g1_ref, g2_ref)
    pairs = [(i, b) for i in range(nb) for b in range(N_BRANCH)]
    dists = [((nk - key) * d).astype(F32) for d in BRANCH_DILATIONS]
    qs = [q_ref[i, b * HEADS_B:(b + 1) * HEADS_B, :] * scale for i, b in pairs]
    s_old = [_dot(q, g_refs[b][i, 0]) - slopes * dists[b] for q, (i, b) in zip(qs, pairs)]
    s_new = [jnp.sum(q * kvn_ref[b, i:i + 1, :kv_w], axis=-1, keepdims=True) for q, (i, b) in zip(qs, pairs)]
    ms = [jnp.maximum(jnp.max(so, axis=-1, keepdims=True), sn) for so, sn in zip(s_old, s_new)]
    p_old = [jnp.exp(so - m) for so, m in zip(s_old, ms)]
    p_new = [jnp.exp(sn - m) for sn, m in zip(s_new, ms)]
    ls = [jnp.sum(po, axis=-1, keepdims=True) + pn for po, pn in zip(p_old, p_new)]
    outs = [(_dot_nt(po, g_refs[b][i, 1]) + pn * kvn_ref[b, i:i + 1, kv_w:]) / l
            for po, pn, l, (i, b) in zip(p_old, p_new, ls, pairs)]
    lses = [m + jnp.log(l) for m, l in zip(ms, ls)]
    for i in range(nb):
        sl = slice(i * N_BRANCH, (i + 1) * N_BRANCH)
        m = functools.reduce(jnp.maximum, lses[sl])
        ws = [jnp.exp(l - m) for l in lses[sl]]
        merged = sum(w * o for w, o in zip(ws, outs[sl])) / sum(ws)
        o_ref[i] = jnp.where(own_half, merged, 0.0)


def _spread_heads(w, axis):
    w = jnp.moveaxis(w, axis, -1)
    lead = w.shape[:-1]
    w = w.reshape(lead + (-1, KV_HEADS_B, GQA_B, HEAD_DIM_B))
    halves = [jnp.pad(w[..., g, :, :], [(0, 0)] * (len(lead) + 2) + [(g * HEAD_DIM_B, (KV_HEADS_B - 1 - g) * HEAD_DIM_B)])
              for g in range(KV_HEADS_B)]
    out = jnp.stack(halves, axis=-3)
    return jnp.moveaxis(out.reshape(lead + (-1,)), -1, axis)


def _attn_sample(q, kv_new, seen):
    n = q.shape[0]
    nb = SUBLANES
    rows = N_BRANCH * HEADS_B
    seen_blk = (nb, 2, KV_HEADS_B * HEAD_DIM_B, ATT_BLOCK)
    out = pl.pallas_call(
        _attn_sample_kernel,
        grid=(n // nb,),
        in_specs=[pl.BlockSpec((nb, rows, LANES), lambda i: (i, 0, 0)),
                  pl.BlockSpec((N_BRANCH, nb, KV_W_BRANCH), lambda i: (0, i, 0))]
                 + [pl.BlockSpec(seen_blk, lambda i: (i, 0, 0, 0))] * N_BRANCH,
        out_specs=pl.BlockSpec((nb, HEADS_B, LANES), lambda i: (i, 0, 0)),
        out_shape=jax.ShapeDtypeStruct((n, HEADS_B, LANES), F32),
        compiler_params=_params("parallel"),
        name="attn_sample",
    )(q.reshape(n, rows, LANES), kv_new, *seen)
    return out.reshape(n, HEADS_B * LANES)


def _trunk(x, rec0, conv0, kv_bufs, w):
    n, seq_len, d = x.shape
    fresh = kv_bufs is None
    assert seq_len == 1 or (fresh and seq_len % (ATT_BLOCK * max(BRANCH_DILATIONS)) == 0)
    x = x.reshape(n * seq_len, d)
    rec_new, conv_new = [], []
    for l in range(N_LAYERS_A):
        qkvz, ba = _inproj(x, w['norm_mix'][l], w['w_in_a'][l])
        qkv_seq = qkvz.reshape(n, seq_len, QKVZ_W_A)[:, :, :QKV_W_A]
        conv_new.append(jnp.concatenate([conv0[l], qkv_seq], axis=1)[:, seq_len:])
        if fresh:
            o, s_new = _gdn(qkvz, ba, n, seq_len, GDN_CHUNK, rec0[l], conv0[l], w['conv_a'][l], w['a_log'][l],
                            w['dt_bias'][l], w['norm_o_a'][l])
        else:
            o, s_new = _gdn_step(qkvz, ba, rec0, conv0, l, w['conv_a'][l], w['a_log'][l], w['dt_bias'][l],
                                 w['norm_o_a'][l])
        rec_new.append(s_new)
        x = _mlp_block(x, [o], w['w_out_a'][l], w['norm_mlp'][l], w['w_up'], w['w_down'], l,
                       w['norm_final'], False)

    if fresh:
        kv_lo, kv = _branch_proj(x, w['norm_kv'], w['w_kv'], True)
    else:
        kv, = _norm_matmul(x, w['norm_kv'], w['w_kv'], KV_W_BRANCH, True, (F32,))
    bufs_new, seen = [], []
    for b in range(N_BRANCH):
        if fresh:
            keep = min(BRANCH_WINDOWS[b], seq_len)
            last = kv[b].reshape(n, seq_len, KV_W_BRANCH)[:, seq_len - keep:]
            bufs_new.append(last.reshape(n, keep, 2, KV_HEADS_B, HEAD_DIM_B))
        else:
            assert kv_bufs[b].shape[1] == BRANCH_WINDOWS[b]
            shifted, seen_b = _cache_update(jnp.transpose(kv_bufs[b], (0, 2, 3, 4, 1)), kv[b],
                                            BRANCH_WINDOWS[b], BRANCH_DILATIONS[b])
            bufs_new.append(jnp.transpose(shifted, (0, 4, 1, 2, 3)))
            seen.append(seen_b)

    for l in range(N_LAYERS_A, N_LAYERS_A + N_LAYERS_B):
        lb = l - N_LAYERS_A
        if fresh:
            q, _ = _branch_proj(x, w['norm_mix'][l], w['w_q_b'][lb], False)
            parts = [_attn_prompt_branch(q[b], kv_lo[b], n, seq_len, BRANCH_DILATIONS[b]) for b in range(N_BRANCH)]
            mix = [p[0] for p in parts] + [p[1] for p in parts]
            w_o = w['w_o_b'][lb]
        else:
            q, = _norm_matmul(x, w['norm_mix'][l], _spread_heads(w['w_q_b'][lb], 1), 1024, False, (F32,))
            mix = [_attn_sample(q, jnp.stack(kv), seen)]
            w_o = _spread_heads(w['w_o_b'][lb], 0)
        x = _mlp_block(x, mix, w_o, w['norm_mlp'][l], w['w_up'], w['w_down'], l,
                       w['norm_final'], l == N_LAYERS_A + N_LAYERS_B - 1)
    return (x.reshape(n, seq_len, d), jnp.stack(rec_new), jnp.stack(conv_new), *bufs_new)


def kernel(x_prompt, x_sample, state_a_rec, state_a_conv, cache_b0_kv, cache_b1_kv, cache_b2_kv, norm_mix, norm_mlp, w_in_a, conv_a, a_log, dt_bias, norm_o_a, w_out_a, norm_kv, w_kv, w_q_b, w_o_b, w_up, w_down, norm_final):
    w = {'norm_mix': norm_mix, 'norm_mlp': norm_mlp, 'w_in_a': w_in_a, 'conv_a': conv_a, 'a_log': a_log,
         'dt_bias': dt_bias, 'norm_o_a': norm_o_a, 'w_out_a': w_out_a, 'norm_kv': norm_kv, 'w_kv': w_kv,
         'w_q_b': w_q_b, 'w_o_b': w_o_b, 'w_up': w_up, 'w_down': w_down, 'norm_final': norm_final}
    n_p = x_prompt.shape[0]
    p_rec0 = jnp.zeros((N_LAYERS_A, n_p, HEADS_A, DK_A, DV_A), state_a_rec.dtype)
    p_conv0 = jnp.zeros((N_LAYERS_A, n_p, CONV_W - 1, QKV_W_A), state_a_conv.dtype)
    prompt = _trunk(x_prompt, p_rec0, p_conv0, None, w)
    sample = _trunk(x_sample, state_a_rec, state_a_conv, [cache_b0_kv, cache_b1_kv, cache_b2_kv], w)
    return (prompt[0], sample[0], *prompt[1:], *sample[1:])
```

```python
import functools

import jax
import jax.numpy as jnp
from jax import lax
from jax.experimental import pallas as pl
from jax.experimental.pallas import tpu as pltpu

D_MODEL = 1024
HEADS_A = 8
DK_A = 128
DV_A = 128
KEY_W_A = HEADS_A * DK_A
VAL_W_A = HEADS_A * DV_A
QKV_W_A = 2 * KEY_W_A + VAL_W_A
QKVZ_W_A = QKV_W_A + VAL_W_A
CONV_W = 4
N_LAYERS_A = 2
N_LAYERS_B = 2
BRANCH_WINDOWS = (128, 512, 2048)
BRANCH_DILATIONS = (1, 4, 16)
N_BRANCH = 3
HEADS_B = 8
KV_HEADS_B = 2
GQA_B = HEADS_B // KV_HEADS_B
HEAD_DIM_B = 64
Q_W_BRANCH = HEADS_B * HEAD_DIM_B
KV_W_BRANCH = 2 * KV_HEADS_B * HEAD_DIM_B
ALIBI_SLOPES = tuple(2.0 ** (-8.0 * h / HEADS_B) for h in range(1, HEADS_B + 1))
EPS = 1e-6

LANES = 128
SUBLANES = 8
VMEM_LIMIT_BYTES = 56 * 1024 * 1024
ROW_TILE = 1024
MLP_FF_TILE = 512
GDN_CHUNK = 128
ATT_BLOCK = 128

F32 = jnp.float32
BF16 = jnp.bfloat16


def _params(*sem):
    return pltpu.CompilerParams(dimension_semantics=sem, vmem_limit_bytes=VMEM_LIMIT_BYTES)


def _rms(x, gain):
    return x * lax.rsqrt(jnp.mean(x * x, axis=-1, keepdims=True) + EPS) * gain


def _sigmoid(x):
    return 1.0 / (1.0 + jnp.exp(-x))


def _softplus(x):
    return jnp.maximum(x, 0.0) + jnp.log(1.0 + jnp.exp(-jnp.abs(x)))


def _dot(a, b):
    return jnp.dot(a.astype(BF16), b.astype(BF16), preferred_element_type=F32)


def _dot_nt(a, b):
    return lax.dot_general(a.astype(BF16), b.astype(BF16), (((1,), (1,)), ((), ())),
                           preferred_element_type=F32)


def _dot_tn(a, b):
    return lax.dot_general(a.astype(BF16), b.astype(BF16), (((0,), (0,)), ((), ())),
                           preferred_element_type=F32)


def _dot_f32(a, b):
    return jnp.dot(a, b, precision=lax.Precision.HIGHEST, preferred_element_type=F32)


def _norm_matmul_kernel(x_ref, g_ref, w_ref, *rest, groups):
    *o_refs, xn_ref = rest
    j = pl.program_id(1)

    @pl.when(j == 0)
    def _():
        xn_ref[...] = _rms(x_ref[...], g_ref[...]).astype(BF16)

    y = jnp.dot(xn_ref[...], w_ref[...], preferred_element_type=F32)
    if groups == 1:
        for o_ref in o_refs:
            o_ref[...] = y.astype(o_ref.dtype)
    else:
        for g in range(groups):
            @pl.when(j == g)
            def _():
                for o_ref in o_refs[g::groups]:
                    o_ref[...] = y.astype(o_ref.dtype)


def _norm_matmul(x, gain, w, tn, split, dtypes):
    t, d = x.shape
    f = w.shape[1]
    tm = min(ROW_TILE, t)
    nj = f // tn
    if split:
        out_specs = [pl.BlockSpec((tm, tn), lambda i, j: (i, 0))] * (nj * len(dtypes))
        out_shape = [jax.ShapeDtypeStruct((t, tn), dt) for dt in dtypes for _ in range(nj)]
    else:
        out_specs = [pl.BlockSpec((tm, tn), lambda i, j: (i, j))] * len(dtypes)
        out_shape = [jax.ShapeDtypeStruct((t, f), dt) for dt in dtypes]
    outs = pl.pallas_call(
        functools.partial(_norm_matmul_kernel, groups=nj if split else 1),
        grid=(t // tm, nj),
        in_specs=[pl.BlockSpec((tm, d), lambda i, j: (i, 0)),
                  pl.BlockSpec((1, d), lambda i, j: (0, 0)),
                  pl.BlockSpec((d, tn), lambda i, j: (0, j))],
        out_specs=out_specs,
        out_shape=out_shape,
        scratch_shapes=[pltpu.VMEM((tm, d), BF16)],
        compiler_params=_params("parallel", "arbitrary"),
        name="norm_matmul",
    )(x, gain.reshape(1, d), w.astype(BF16))
    if split:
        return [outs[k * nj:(k + 1) * nj] for k in range(len(dtypes))]
    return outs


def _branch_proj_kernel(x_ref, g_ref, w_ref, *rest, keep_f32):
    *o_refs, xn_ref, y_ref = rest
    chunks, tm, _ = y_ref.shape
    tn = chunks * LANES
    j = pl.program_id(1)

    @pl.when(j == 0)
    def _():
        xn_ref[...] = _rms(x_ref[...], g_ref[...]).astype(BF16)

    y = jnp.dot(xn_ref[...], w_ref[...], preferred_element_type=F32)
    for b, dil in enumerate(BRANCH_DILATIONS):
        @pl.when(j == b)
        def _():
            if keep_f32:
                o_refs[N_BRANCH + b][...] = y
            if dil == 1:
                o_refs[b][...] = y.astype(BF16)
            else:
                for c in range(chunks):
                    y_ref[c] = y[:, c * LANES:(c + 1) * LANES]
                for r in range(dil):
                    for c in range(chunks):
                        lo = r * tn + c * LANES
                        o_refs[b][:, lo:lo + LANES] = y_ref[c, pl.ds(r, tm // dil, stride=dil), :].astype(BF16)


def _branch_proj(x, gain, w, keep_f32):
    t, d = x.shape
    tn = w.shape[1] // N_BRANCH
    tm = min(ROW_TILE, t)
    row = lambda i, j: (i, 0)
    out_specs = [pl.BlockSpec((tm // dil, dil * tn), row) for dil in BRANCH_DILATIONS]
    out_shape = [jax.ShapeDtypeStruct((t // dil, dil * tn), BF16) for dil in BRANCH_DILATIONS]
    if keep_f32:
        out_specs += [pl.BlockSpec((tm, tn), row)] * N_BRANCH
        out_shape += [jax.ShapeDtypeStruct((t, tn), F32)] * N_BRANCH
    outs = pl.pallas_call(
        functools.partial(_branch_proj_kernel, keep_f32=keep_f32),
        grid=(t // tm, N_BRANCH),
        in_specs=[pl.BlockSpec((tm, d), row),
                  pl.BlockSpec((1, d), lambda i, j: (0, 0)),
                  pl.BlockSpec((d, tn), lambda i, j: (0, j))],
        out_specs=out_specs,
        out_shape=out_shape,
        scratch_shapes=[pltpu.VMEM((tm, d), BF16), pltpu.VMEM((tn // LANES, tm, LANES), F32)],
        compiler_params=_params("parallel", "arbitrary"),
        name="branch_proj",
    )(x, gain.reshape(1, d), w.astype(BF16))
    return outs[:N_BRANCH], outs[N_BRANCH:]


def _inproj_kernel(x_ref, g_ref, w_ref, wg_ref, o_ref, og_ref, xn_ref):
    @pl.when(pl.program_id(1) == 0)
    def _():
        xn = _rms(x_ref[...], g_ref[...]).astype(BF16)
        xn_ref[...] = xn
        og_ref[...] = jnp.dot(xn, wg_ref[...], preferred_element_type=F32)

    o_ref[...] = jnp.dot(xn_ref[...], w_ref[...], preferred_element_type=F32)


def _inproj(x, gain, w_in):
    t, d = x.shape
    tm = min(ROW_TILE, t)
    tn = 1024
    w_main = w_in[:, :QKVZ_W_A].astype(BF16)
    w_gate = jnp.pad(w_in[:, QKVZ_W_A:], ((0, 0), (0, LANES - 2 * HEADS_A))).astype(BF16)
    return pl.pallas_call(
        _inproj_kernel,
        grid=(t // tm, QKVZ_W_A // tn),
        in_specs=[pl.BlockSpec((tm, d), lambda i, j: (i, 0)),
                  pl.BlockSpec((1, d), lambda i, j: (0, 0)),
                  pl.BlockSpec((d, tn), lambda i, j: (0, j)),
                  pl.BlockSpec((d, LANES), lambda i, j: (0, 0))],
        out_specs=[pl.BlockSpec((tm, tn), lambda i, j: (i, j)),
                   pl.BlockSpec((tm, LANES), lambda i, j: (i, 0))],
        out_shape=[jax.ShapeDtypeStruct((t, QKVZ_W_A), F32),
                   jax.ShapeDtypeStruct((t, LANES), F32)],
        scratch_shapes=[pltpu.VMEM((tm, d), BF16)],
        compiler_params=_params("parallel", "arbitrary"),
        name="gdn_inproj",
    )(x, gain.reshape(1, d), w_main, w_gate)


def _inproj_conv_kernel(x_ref, g_ref, w_ref, wg_ref, cw_ref, c0_ref, o_ref, og_ref, tail_ref, xn_ref, halo_ref,
                        *, tiles_per_seq):
    i = pl.program_id(0)
    j = pl.program_id(1)
    halo = SUBLANES

    @pl.when(j == 0)
    def _():
        xn = _rms(x_ref[...], g_ref[...]).astype(BF16)
        xn_ref[...] = xn
        og_ref[...] = jnp.dot(xn, wg_ref[...], preferred_element_type=F32)

    y = jnp.dot(xn_ref[...], w_ref[...], preferred_element_type=F32)
    tm = y.shape[0]
    prev = jnp.where(i % tiles_per_seq == 0, c0_ref[...], halo_ref[j])
    ext = jnp.concatenate([prev, y], axis=0)
    conv = y * cw_ref[CONV_W - 1:CONV_W, :]
    for back in range(1, CONV_W):
        conv = conv + pltpu.roll(ext, back, axis=0)[halo:] * cw_ref[CONV_W - 1 - back:CONV_W - back, :]
    halo_ref[j] = y[tm - halo:]
    tail_ref[...] = y[tm - halo:]
    act = conv * _sigmoid(conv)
    col_head = lax.broadcasted_iota(jnp.int32, (y.shape[1], LANES), 0) // DK_A
    lane = lax.broadcasted_iota(jnp.int32, (y.shape[1], LANES), 1)
    sums = _dot(act * act, (col_head == lane).astype(BF16))
    unit = lax.rsqrt(sums + EPS) * jnp.where(j == 0, DK_A ** -0.5, 1.0)
    factor = jnp.where(j < 2, unit, 1.0)
    for h in range(HEADS_A):
        sl = slice(h * DK_A, (h + 1) * DK_A)
        o_ref[:, sl] = (act[:, sl] * factor[:, h:h + 1]).astype(o_ref.dtype)


def _inproj_conv(x, gain, w_in, w_conv, conv0, n, seq_len):
    t, d = x.shape
    tm = min(ROW_TILE, seq_len)
    tn = KEY_W_A
    halo = SUBLANES
    tiles_per_seq = seq_len // tm
    w_main = w_in[:, :QKVZ_W_A].astype(BF16)
    w_gate = jnp.pad(w_in[:, QKVZ_W_A:], ((0, 0), (0, LANES - 2 * HEADS_A))).astype(BF16)
    pass_taps = jnp.zeros((CONV_W, VAL_W_A), F32).at[CONV_W - 1].set(1.0)
    taps = jnp.concatenate([w_conv, pass_taps], axis=1)
    c0 = jnp.pad(conv0, ((0, 0), (halo - (CONV_W - 1), 0), (0, VAL_W_A)))
    conv_col = lambda i, j: (0, j)
    seq_col = lambda i, j: (i // tiles_per_seq, 0, j)
    return pl.pallas_call(
        functools.partial(_inproj_conv_kernel, tiles_per_seq=tiles_per_seq),
        grid=(t // tm, QKVZ_W_A // tn),
        in_specs=[pl.BlockSpec((tm, d), lambda i, j: (i, 0)),
                  pl.BlockSpec((1, d), lambda i, j: (0, 0)),
                  pl.BlockSpec((d, tn), lambda i, j: (0, j)),
                  pl.BlockSpec((d, LANES), lambda i, j: (0, 0)),
                  pl.BlockSpec((CONV_W, tn), conv_col),
                  pl.BlockSpec((None, halo, tn), seq_col)],
        out_specs=[pl.BlockSpec((tm, tn), lambda i, j: (i, j)),
                   pl.BlockSpec((tm, LANES), lambda i, j: (i, 0)),
                   pl.BlockSpec((None, halo, tn), lambda i, j: (i, 0, j))],
        out_shape=[jax.ShapeDtypeStruct((t, QKVZ_W_A), BF16),
                   jax.ShapeDtypeStruct((t, LANES), F32),
                   jax.ShapeDtypeStruct((t // tm, halo, QKVZ_W_A), F32)],
        scratch_shapes=[pltpu.VMEM((tm, d), BF16), pltpu.VMEM((QKVZ_W_A // tn, halo, tn), F32)],
        compiler_params=_params("arbitrary", "arbitrary"),
        name="gdn_inproj_conv",
    )(x, gain.reshape(1, d), w_main, w_gate, taps, c0)


def _mlp_kernel(*refs, n_mix, final_norm):
    x_ref = refs[0]
    mix_refs = refs[1:1 + n_mix]
    wo_ref, g_ref, wup_ref, wdn_ref, gf_ref, y_ref, xn_ref = refs[1 + n_mix:1 + n_mix + 7]
    nat_refs = refs[1 + n_mix + 7:]
    f = pl.program_id(1)

    @pl.when(f == 0)
    def _():
        if n_mix == 1:
            mixed = mix_refs[0][...]
        else:
            tm = x_ref.shape[0]
            chunks = Q_W_BRANCH // LANES
            for b, dil in enumerate(BRANCH_DILATIONS):
                o_blk, lse_blk = mix_refs[b], mix_refs[N_BRANCH + b]
                o_nat, lse_nat = nat_refs[b], nat_refs[N_BRANCH + b]
                rows = tm // dil
                for r in range(dil):
                    dst = pl.ds(r, rows, stride=dil) if dil > 1 else pl.ds(0, rows)
                    for c in range(chunks):
                        lo = r * Q_W_BRANCH + c * LANES
                        o_nat[c, dst, :] = o_blk[:, lo:lo + LANES].astype(F32)
                    lse_nat[dst, :] = lse_blk[r]
            lses = [r[...] for r in nat_refs[N_BRANCH:]]
            m = functools.reduce(jnp.maximum, lses)
            ws = [jnp.exp(l - m) for l in lses]
            total = sum(ws)
            low = lax.broadcasted_iota(jnp.int32, (1, 2 * HEAD_DIM_B), 1) < HEAD_DIM_B
            mixed = 0.0
            for w, o_nat in zip(ws, nat_refs[:N_BRANCH]):
                w = w / total
                mixed = mixed + jnp.concatenate(
                    [jnp.where(low, w[:, 2 * p:2 * p + 1], w[:, 2 * p + 1:2 * p + 2]) * o_nat[p]
                     for p in range(HEADS_B // 2)], axis=1)
        x1 = x_ref[...] + _dot(mixed, wo_ref[...])
        y_ref[...] = x1
        xn_ref[...] = _rms(x1, g_ref[...]).astype(BF16)

    u = jnp.maximum(jnp.dot(xn_ref[...], wup_ref[...], preferred_element_type=F32), 0.0)
    y_ref[...] += _dot(u * u, wdn_ref[...])

    if final_norm:
        @pl.when(f == pl.num_programs(1) - 1)
        def _():
            y_ref[...] = _rms(y_ref[...], gf_ref[...])


def _mlp_block(x, mix, w_o, gain, w_up, w_down, layer, gain_final, final_norm):
    t, d = x.shape
    k = w_o.shape[0]
    dff = w_up.shape[2]
    tm = min(ROW_TILE, t)
    tf = MLP_FF_TILE
    n_mix = len(mix)
    row = lambda i, f: (i, 0)
    const = lambda i, f: (0, 0)
    scratch = [pltpu.VMEM((tm, d), BF16)]
    if n_mix == 1:
        mix_specs = [pl.BlockSpec((tm, k), row)]
    else:
        mix_specs = [pl.BlockSpec((tm // dil, dil * Q_W_BRANCH), row) for dil in BRANCH_DILATIONS]
        for dil, lse in zip(BRANCH_DILATIONS, mix[N_BRANCH:]):
            per_seq = lse.shape[2] // (tm // dil)
            mix_specs.append(pl.BlockSpec((None, dil, tm // dil, LANES),
                                          lambda i, f, per_seq=per_seq: (i // per_seq, 0, i % per_seq, 0)))
        scratch += ([pltpu.VMEM((Q_W_BRANCH // LANES, tm, LANES), F32)] * N_BRANCH
                    + [pltpu.VMEM((tm, LANES), F32)] * N_BRANCH)
    return pl.pallas_call(
        functools.partial(_mlp_kernel, n_mix=n_mix, final_norm=final_norm),
        grid=(t // tm, dff // tf),
        in_specs=([pl.BlockSpec((tm, d), row)] + mix_specs
                  + [pl.BlockSpec((k, d), const),
                     pl.BlockSpec((1, d), const),
                     pl.BlockSpec((None, d, tf), lambda i, f: (layer, 0, f)),
                     pl.BlockSpec((None, tf, d), lambda i, f: (layer, f, 0)),
                     pl.BlockSpec((1, d), const)]),
        out_specs=pl.BlockSpec((tm, d), row),
        out_shape=jax.ShapeDtypeStruct((t, d), F32),
        scratch_shapes=scratch,
        compiler_params=_params("parallel", "arbitrary"),
        name="mix_out_mlp",
    )(x, *mix, w_o.astype(BF16), gain.reshape(1, d), w_up.astype(BF16), w_down.astype(BF16),
      gain_final.reshape(1, d))


def _inverse_masks(c):
    row = lax.broadcasted_iota(jnp.int32, (c, c), 0)
    col = lax.broadcasted_iota(jnp.int32, (c, c), 1)
    masks = []
    shift = 0
    while (1 << shift) < c:
        same_pair = (row >> (shift + 1)) == (col >> (shift + 1))
        other_half = (row >> shift) != (col >> shift)
        masks.append((row > col) & same_pair & other_half)
        shift += 1
    return masks


def _unit_lower_inverses(mats, c):
    masks = _inverse_masks(c)
    row = lax.broadcasted_iota(jnp.int32, (c, c), 0)
    col = lax.broadcasted_iota(jnp.int32, (c, c), 1)
    eye = (row == col).astype(F32)
    invs = [eye - jnp.where(masks[0], a, 0.0) for a in mats]
    for mask in masks[1:]:
        xs = [_dot(inv, jnp.where(mask, a, 0.0)) for inv, a in zip(invs, mats)]
        invs = [inv - _dot(x, inv) for inv, x in zip(invs, xs)]
    return invs


def _gdn_kernel(q_ref, k_ref, v_ref, z_ref, ba_ref, bat_ref, s0_ref, alr_ref, dtr_ref, alc_ref, dtc_ref, gn_ref,
                o_ref, s_ref, *, chunk):
    c = chunk

    @pl.when(pl.program_id(1) == 0)
    def _():
        s_ref[...] = s0_ref[...]

    ba = ba_ref[...]
    bat = bat_ref[...]
    beta_cols = _sigmoid(ba)
    g_cols = -jnp.exp(alr_ref[...]) * _softplus(ba + dtr_ref[...])
    g_rows = -jnp.exp(alc_ref[...]) * _softplus(bat + dtc_ref[...])
    row = lax.broadcasted_iota(jnp.int32, (c, c), 0)
    col = lax.broadcasted_iota(jnp.int32, (c, c), 1)
    causal = row >= col
    strict = row > col
    cum_cols = _dot_f32(causal.astype(F32), g_cols)
    cum_rows = _dot_f32(g_rows, (row <= col).astype(F32))

    heads = range(HEADS_A)
    head_cols = [slice(h * DK_A, (h + 1) * DK_A) for h in heads]
    qs = [q_ref[:, sl] for sl in head_cols]
    ks = [k_ref[:, sl] for sl in head_cols]
    kfs = [k.astype(F32) for k in ks]
    betas = [beta_cols[:, h:h + 1] for h in heads]
    gcs = [cum_cols[:, HEADS_A + h:HEADS_A + h + 1] for h in heads]
    grs = [cum_rows[HEADS_A + h:HEADS_A + h + 1, :] for h in heads]
    decays = [jnp.where(causal, jnp.exp(jnp.where(causal, gc - gr, 0.0)), 0.0) for gc, gr in zip(gcs, grs)]
    kbs = [k * b for k, b in zip(kfs, betas)]
    kq = [_dot_nt(jnp.concatenate([kb.astype(BF16), q], axis=0), k) for kb, q, k in zip(kbs, qs, ks)]
    mats = [jnp.where(strict, x[:c] * d, 0.0) for x, d in zip(kq, decays)]
    attns = [x[c:] * d for x, d in zip(kq, decays)]
    t_invs = _unit_lower_inverses(mats, c)
    e_gcs = [jnp.exp(gc) for gc in gcs]
    uws = [_dot(t, jnp.concatenate([v_ref[:, sl].astype(F32) * b, kb * e], axis=1))
           for t, sl, b, kb, e in zip(t_invs, head_cols, betas, kbs, e_gcs)]
    states = [s_ref[h] for h in heads]
    wq_s = [_dot(jnp.concatenate([uw[:, DV_A:], q.astype(F32) * e], axis=0), s)
            for uw, q, e, s in zip(uws, qs, e_gcs, states)]
    v_news = [uw[:, :DV_A] - x[:c] for uw, x in zip(uws, wq_s)]
    outs = [x[c:] + _dot(attn, vn) for x, attn, vn in zip(wq_s, attns, v_news)]
    for h in heads:
        g_last = gcs[h][c - 1:c, :]
        s_ref[h] = states[h] * jnp.exp(g_last) + _dot_tn(kfs[h] * jnp.exp(g_last - gcs[h]), v_news[h])
    for h, sl in zip(heads, head_cols):
        o_ref[:, sl] = (_rms(outs[h], gn_ref[...]) * z_ref[:, sl].astype(F32)).astype(o_ref.dtype)


def _gdn(act, ba, n, seq_len, chunk, s0, a_log, dt_bias, g_norm):
    nc = seq_len // chunk
    t = n * seq_len
    bat = ba[:, :2 * HEADS_A].reshape(n * nc, chunk, 2 * HEADS_A).transpose(0, 2, 1)
    pad_row = lambda p: jnp.pad(p.reshape(1, HEADS_A), ((0, 0), (HEADS_A, LANES - 2 * HEADS_A)))
    pad_col = lambda p: jnp.pad(p.reshape(HEADS_A, 1), ((HEADS_A, 0), (0, 0)))
    blk = lambda i, j: (i * nc + j, 0)
    const = lambda i, j: (0, 0)
    col_blk = lambda col: pl.BlockSpec((chunk, KEY_W_A), lambda i, j: (i * nc + j, col))
    return pl.pallas_call(
        functools.partial(_gdn_kernel, chunk=chunk),
        grid=(n, nc),
        in_specs=[col_blk(0), col_blk(1), col_blk(2), col_blk(3),
                  pl.BlockSpec((chunk, LANES), blk),
                  pl.BlockSpec((None, 2 * HEADS_A, chunk), lambda i, j: (i * nc + j, 0, 0)),
                  pl.BlockSpec((None, HEADS_A, DK_A, DV_A), lambda i, j: (i, 0, 0, 0)),
                  pl.BlockSpec((1, LANES), const),
                  pl.BlockSpec((1, LANES), const),
                  pl.BlockSpec((2 * HEADS_A, 1), const),
                  pl.BlockSpec((2 * HEADS_A, 1), const),
                  pl.BlockSpec((1, DV_A), const)],
        out_specs=[pl.BlockSpec((chunk, VAL_W_A), blk),
                   pl.BlockSpec((None, HEADS_A, DK_A, DV_A), lambda i, j: (i, 0, 0, 0))],
        out_shape=[jax.ShapeDtypeStruct((t, VAL_W_A), BF16),
                   jax.ShapeDtypeStruct((n, HEADS_A, DK_A, DV_A), F32)],
        compiler_params=_params("parallel", "arbitrary"),
        name="gdn_delta_rule",
    )(act, act, act, act, ba, bat, s0, pad_row(a_log), pad_row(dt_bias), pad_col(a_log), pad_col(dt_bias),
      g_norm.reshape(1, DV_A))


def _gdn_step_kernel(qkv_ref, z_ref, ba_ref, st_ref, cw_ref, s0_ref, alr_ref, dtr_ref, gn_ref, o_ref, s_ref):
    nb = qkv_ref.shape[0]
    conv = qkv_ref[...] * cw_ref[CONV_W - 1:CONV_W, :]
    for j in range(CONV_W - 1):
        conv = conv + st_ref[j] * cw_ref[j:j + 1, :]
    act = conv * _sigmoid(conv)
    ba = ba_ref[...]
    beta_cols = _sigmoid(ba)
    decay_cols = jnp.exp(-jnp.exp(alr_ref[...]) * _softplus(ba + dtr_ref[...]))
    fill = jnp.zeros((DK_A - nb, DK_A), F32)
    for h in range(HEADS_A):
        lo = h * DK_A
        q = act[:, lo:lo + DK_A]
        k = act[:, KEY_W_A + lo:KEY_W_A + lo + DK_A]
        v = act[:, 2 * KEY_W_A + lo:2 * KEY_W_A + lo + DV_A]
        q = q * lax.rsqrt(jnp.sum(q * q, axis=-1, keepdims=True) + EPS) * (DK_A ** -0.5)
        k = k * lax.rsqrt(jnp.sum(k * k, axis=-1, keepdims=True) + EPS)
        q_t = jnp.concatenate([q, fill], axis=0).T
        k_t = jnp.concatenate([k, fill], axis=0).T
        z = z_ref[:, lo:lo + DV_A]
        gate = z * _sigmoid(z)
        seqs = range(nb)
        k_cols = [k_t[:, i:i + 1] for i in seqs]
        decays = [decay_cols[i:i + 1, HEADS_A + h:HEADS_A + h + 1] for i in seqs]
        k_s = [jnp.sum(k_cols[i] * s0_ref[i, h], axis=0, keepdims=True) for i in seqs]
        v_new = [beta_cols[i:i + 1, h:h + 1] * (v[i:i + 1, :] - decays[i] * k_s[i]) for i in seqs]
        s_new = [decays[i] * s0_ref[i, h] + k_cols[i] * v_new[i] for i in seqs]
        for i in seqs:
            s_ref[i, h] = s_new[i]
        outs = [jnp.sum(q_t[:, i:i + 1] * s_new[i], axis=0, keepdims=True) for i in seqs]
        for i in seqs:
            o_ref[i:i + 1, lo:lo + DV_A] = _rms(outs[i], gn_ref[...]) * gate[i:i + 1, :]


def _gdn_step(qkvz, ba, s0_layers, conv0_layers, layer, w_conv, a_log, dt_bias, g_norm):
    n = qkvz.shape[0]
    nb = SUBLANES
    pad_row = lambda p: jnp.pad(p.reshape(1, HEADS_A), ((0, 0), (HEADS_A, LANES - 2 * HEADS_A)))
    blk = lambda i: (i, 0)
    const = lambda i: (0, 0)
    return pl.pallas_call(
        _gdn_step_kernel,
        grid=(n // nb,),
        in_specs=[pl.BlockSpec((nb, QKV_W_A), blk),
                  pl.BlockSpec((nb, VAL_W_A), lambda i: (i, QKV_W_A // VAL_W_A)),
                  pl.BlockSpec((nb, LANES), blk),
                  pl.BlockSpec((None, CONV_W - 1, nb, QKV_W_A), lambda i: (layer, 0, i, 0)),
                  pl.BlockSpec((CONV_W, QKV_W_A), const),
                  pl.BlockSpec((None, nb, HEADS_A, DK_A, DV_A), lambda i: (layer, i, 0, 0, 0)),
                  pl.BlockSpec((1, LANES), const),
                  pl.BlockSpec((1, LANES), const),
                  pl.BlockSpec((1, DV_A), const)],
        out_specs=[pl.BlockSpec((nb, VAL_W_A), blk),
                   pl.BlockSpec((nb, HEADS_A, DK_A, DV_A), lambda i: (i, 0, 0, 0))],
        out_shape=[jax.ShapeDtypeStruct((n, VAL_W_A), F32),
                   jax.ShapeDtypeStruct((n, HEADS_A, DK_A, DV_A), F32)],
        compiler_params=_params("parallel"),
        name="gdn_step",
    )(qkvz, qkvz, ba, conv0_layers.transpose(0, 2, 1, 3), w_conv, s0_layers, pad_row(a_log), pad_row(dt_bias),
      g_norm.reshape(1, DV_A))


def _attn_prompt_kernel(q_ref, kvc_ref, kvp_ref, o_ref, lse_ref, *, dilation, sub_blocks):
    blk = ATT_BLOCK
    hd = HEAD_DIM_B
    a = pl.program_id(2)
    qi = lax.broadcasted_iota(jnp.int32, (blk, 2 * blk), 0)
    kj = lax.broadcasted_iota(jnp.int32, (blk, 2 * blk), 1)
    steps = blk + qi - kj
    in_window = (steps >= 0) & (steps <= blk)
    started = in_window & ((kj >= blk) | (a > 0))
    dist = (steps * dilation).astype(F32)
    low = lax.broadcasted_iota(jnp.int32, (1, 2 * hd), 1) < hd
    kv_all = jnp.concatenate([kvp_ref[...], kvc_ref[...]], axis=0)
    padded = []
    for kvh in range(KV_HEADS_B):
        own = low if kvh == 0 else ~low
        both = []
        for x in (kv_all[:, :2 * hd], kv_all[:, 2 * hd:]):
            kept = jnp.where(own, x, jnp.zeros_like(x))
            moved = pltpu.roll(kept, hd, axis=1)
            both += [kept, moved] if kvh == 0 else [moved, kept]
        padded.append(both)
    scale = hd ** -0.5
    head_lane = lax.broadcasted_iota(jnp.int32, (1, LANES), 1)
    for j in range(sub_blocks):
        lse_all = jnp.zeros((blk, LANES), F32)
        valid = started if j == 0 else in_window
        rows = slice(j * blk, (j + 1) * blk)
        keys = slice(j * blk, (j + 2) * blk)
        for kvh in range(KV_HEADS_B):
            k_lo, k_hi, v_lo, v_hi = (x[keys] for x in padded[kvh])
            heads = [kvh * GQA_B + g for g in range(GQA_B)]
            qs = [q_ref[rows, (h // 2) * 2 * hd:(h // 2 + 1) * 2 * hd] for h in heads]
            ss = [_dot_nt(q, k_hi if h % 2 else k_lo) * scale - ALIBI_SLOPES[h] * dist for q, h in zip(qs, heads)]
            ss = [jnp.where(valid, s, -jnp.inf) for s in ss]
            ms = [jnp.max(s, axis=-1, keepdims=True) for s in ss]
            ps = [jnp.exp(s - m) for s, m in zip(ss, ms)]
            ls = [jnp.sum(p, axis=-1, keepdims=True) for p in ps]
            for g in range(0, GQA_B, 2):
                pair = (heads[g] // 2) * 2 * hd
                o = _dot(ps[g], v_lo) + _dot(ps[g + 1], v_hi)
                o_ref[rows, pair:pair + 2 * hd] = (o / jnp.where(low, ls[g], ls[g + 1])).astype(o_ref.dtype)
            for g, h in enumerate(heads):
                lse_all = jnp.where(head_lane == h, ms[g] + jnp.log(ls[g]), lse_all)
        lse_ref[rows, :] = lse_all


def _attn_prompt_branch(q, kv, n, seq_len, dilation):
    ls = seq_len // dilation
    sub_blocks = min(4, ls // ATT_BLOCK)
    qb = sub_blocks * ATT_BLOCK
    qv = q.reshape(n, ls, dilation * Q_W_BRANCH)
    kvv = kv.reshape(n, ls, dilation * KV_W_BRANCH)
    cur = lambda i, r, a: (i, a, r)
    prev = lambda i, r, a: (i, jnp.maximum(a * sub_blocks - 1, 0), r)
    o, lse = pl.pallas_call(
        functools.partial(_attn_prompt_kernel, dilation=dilation, sub_blocks=sub_blocks),
        grid=(n, dilation, ls // qb),
        in_specs=[pl.BlockSpec((None, qb, Q_W_BRANCH), cur),
                  pl.BlockSpec((None, qb, KV_W_BRANCH), cur),
                  pl.BlockSpec((None, ATT_BLOCK, KV_W_BRANCH), prev)],
        out_specs=[pl.BlockSpec((None, qb, Q_W_BRANCH), cur),
                   pl.BlockSpec((None, None, qb, LANES), lambda i, r, a: (i, r, a, 0))],
        out_shape=[jax.ShapeDtypeStruct(qv.shape, BF16),
                   jax.ShapeDtypeStruct((n, dilation, ls, LANES), F32)],
        compiler_params=_params("parallel", "parallel", "arbitrary"),
        name=f"attn_prompt_d{dilation}",
    )(qv, kvv, kvv)
    return o.reshape(n * ls, dilation * Q_W_BRANCH), lse


def _cache_update_kernel(c_ref, kvn_ref, o_ref, g_ref, *, window, dilation):
    nb = c_ref.shape[0]
    nk = ATT_BLOCK
    last = lax.broadcasted_iota(jnp.int32, (1, window), 1) == window - 1
    new_t = jnp.concatenate([kvn_ref[...], jnp.zeros((LANES - nb, KV_W_BRANCH), F32)], axis=0).T
    if dilation > 1:
        src = lax.broadcasted_iota(jnp.int32, (window, nk), 0)
        dst = lax.broadcasted_iota(jnp.int32, (window, nk), 1)
        pick = (src == dst * dilation).astype(BF16)
    for i in range(nb):
        x = c_ref[i].reshape(KV_W_BRANCH, window)
        shifted = jnp.where(last, new_t[:, i:i + 1], pltpu.roll(x, window - 1, axis=1))
        o_ref[i] = shifted.reshape(o_ref.shape[1:])
        seen = x.astype(BF16) if dilation == 1 else _dot(x, pick).astype(BF16)
        g_ref[i] = seen.reshape(g_ref.shape[1:])


def _cache_update(cache_t, kv_new, window, dilation):
    n = cache_t.shape[0]
    nb = max(1, min(SUBLANES, 4096 // window))
    blk = (nb, 2, KV_HEADS_B, HEAD_DIM_B, window)
    seen_blk = (nb, 2, KV_HEADS_B * HEAD_DIM_B, ATT_BLOCK)
    return pl.pallas_call(
        functools.partial(_cache_update_kernel, window=window, dilation=dilation),
        grid=(n // nb,),
        in_specs=[pl.BlockSpec(blk, lambda i: (i, 0, 0, 0, 0)),
                  pl.BlockSpec((None, nb, KV_W_BRANCH), lambda i: (i, 0, 0))],
        out_specs=[pl.BlockSpec(blk, lambda i: (i, 0, 0, 0, 0)),
                   pl.BlockSpec(seen_blk, lambda i: (i, 0, 0, 0))],
        out_shape=[jax.ShapeDtypeStruct(cache_t.shape, F32),
                   jax.ShapeDtypeStruct((n,) + seen_blk[1:], BF16)],
        compiler_params=_params("parallel"),
        name=f"cache_update_w{window}",
    )(cache_t, kv_new.reshape(n // nb, nb, KV_W_BRANCH))


def _attn_sample_kernel(q_ref, kvn_ref, g0_ref, g1_ref, g2_ref, o_ref):
    nk = ATT_BLOCK
    nb = q_ref.shape[0]
    kv_w = KV_HEADS_B * HEAD_DIM_B
    head = lax.broadcasted_iota(jnp.int32, (HEADS_B, 1), 0)
    slopes = jnp.exp2(-8.0 * (head + 1).astype(F32) / HEADS_B)
    key = lax.broadcasted_iota(jnp.int32, (1, nk), 1)
    lane = lax.broadcasted_iota(jnp.int32, (HEADS_B, kv_w), 1)
    own_half = (lane // HEAD_DIM_B) == (head // GQA_B)
    scale = HEAD_DIM_B ** -0.5
    g_refs = (g0_ref, g1_ref, g2_ref)
    pairs = [(i, b) for i in range(nb) for b in range(N_BRANCH)]
    dists = [((nk - key) * d).astype(F32) for d in BRANCH_DILATIONS]
    qs = [q_ref[i, b * HEADS_B:(b + 1) * HEADS_B, :] * scale for i, b in pairs]
    s_old = [_dot(q, g_refs[b][i, 0]) - slopes * dists[b] for q, (i, b) in zip(qs, pairs)]
    s_new = [jnp.sum(q * kvn_ref[b, i:i + 1, :kv_w], axis=-1, keepdims=True) for q, (i, b) in zip(qs, pairs)]
    ms = [jnp.maximum(jnp.max(so, axis=-1, keepdims=True), sn) for so, sn in zip(s_old, s_new)]
    p_old = [jnp.exp(so - m) for so, m in zip(s_old, ms)]
    p_new = [jnp.exp(sn - m) for sn, m in zip(s_new, ms)]
    ls = [jnp.sum(po, axis=-1, keepdims=True) + pn for po, pn in zip(p_old, p_new)]
    outs = [(_dot_nt(po, g_refs[b][i, 1]) + pn * kvn_ref[b, i:i + 1, kv_w:]) / l
            for po, pn, l, (i, b) in zip(p_old, p_new, ls, pairs)]
    lses = [m + jnp.log(l) for m, l in zip(ms, ls)]
    for i in range(nb):
        sl = slice(i * N_BRANCH, (i + 1) * N_BRANCH)
        m = functools.reduce(jnp.maximum, lses[sl])
        ws = [jnp.exp(l - m) for l in lses[sl]]
        merged = sum(w * o for w, o in zip(ws, outs[sl])) / sum(ws)
        o_ref[i] = jnp.where(own_half, merged, 0.0)


def _spread_heads(w, axis):
    w = jnp.moveaxis(w, axis, -1)
    lead = w.shape[:-1]
    w = w.reshape(lead + (-1, KV_HEADS_B, GQA_B, HEAD_DIM_B))
    halves = [jnp.pad(w[..., g, :, :], [(0, 0)] * (len(lead) + 2) + [(g * HEAD_DIM_B, (KV_HEADS_B - 1 - g) * HEAD_DIM_B)])
              for g in range(KV_HEADS_B)]
    out = jnp.stack(halves, axis=-3)
    return jnp.moveaxis(out.reshape(lead + (-1,)), -1, axis)


def _attn_sample(q, kv_new, seen):
    n = q.shape[0]
    nb = SUBLANES
    rows = N_BRANCH * HEADS_B
    seen_blk = (nb, 2, KV_HEADS_B * HEAD_DIM_B, ATT_BLOCK)
    out = pl.pallas_call(
        _attn_sample_kernel,
        grid=(n // nb,),
        in_specs=[pl.BlockSpec((nb, rows, LANES), lambda i: (i, 0, 0)),
                  pl.BlockSpec((N_BRANCH, nb, KV_W_BRANCH), lambda i: (0, i, 0))]
                 + [pl.BlockSpec(seen_blk, lambda i: (i, 0, 0, 0))] * N_BRANCH,
        out_specs=pl.BlockSpec((nb, HEADS_B, LANES), lambda i: (i, 0, 0)),
        out_shape=jax.ShapeDtypeStruct((n, HEADS_B, LANES), F32),
        compiler_params=_params("parallel"),
        name="attn_sample",
    )(q.reshape(n, rows, LANES), kv_new, *seen)
    return out.reshape(n, HEADS_B * LANES)


def _trunk(x, rec0, conv0, kv_bufs, w):
    n, seq_len, d = x.shape
    fresh = kv_bufs is None
    assert seq_len == 1 or (fresh and seq_len % (ATT_BLOCK * max(BRANCH_DILATIONS)) == 0)
    x = x.reshape(n * seq_len, d)
    rec_new, conv_new = [], []
    for l in range(N_LAYERS_A):
        if fresh:
            act, ba, tail = _inproj_conv(x, w['norm_mix'][l], w['w_in_a'][l], w['conv_a'][l], conv0[l], n, seq_len)
            last_tile = tail.reshape(n, -1, SUBLANES, QKVZ_W_A)[:, -1]
            conv_new.append(last_tile[:, SUBLANES - (CONV_W - 1):, :QKV_W_A])
            o, s_new = _gdn(act, ba, n, seq_len, GDN_CHUNK, rec0[l], w['a_log'][l], w['dt_bias'][l],
                            w['norm_o_a'][l])
        else:
            qkvz, ba = _inproj(x, w['norm_mix'][l], w['w_in_a'][l])
            conv_new.append(jnp.concatenate([conv0[l][:, 1:], qkvz[:, None, :QKV_W_A]], axis=1))
            o, s_new = _gdn_step(qkvz, ba, rec0, conv0, l, w['conv_a'][l], w['a_log'][l], w['dt_bias'][l],
                                 w['norm_o_a'][l])
        rec_new.append(s_new)
        x = _mlp_block(x, [o], w['w_out_a'][l], w['norm_mlp'][l], w['w_up'], w['w_down'], l,
                       w['norm_final'], False)

    if fresh:
        kv_lo, kv = _branch_proj(x, w['norm_kv'], w['w_kv'], True)
    else:
        kv, = _norm_matmul(x, w['norm_kv'], w['w_kv'], KV_W_BRANCH, True, (F32,))
    bufs_new, seen = [], []
    for b in range(N_BRANCH):
        if fresh:
            keep = min(BRANCH_WINDOWS[b], seq_len)
            last = kv[b].reshape(n, seq_len, KV_W_BRANCH)[:, seq_len - keep:]
            bufs_new.append(last.reshape(n, keep, 2, KV_HEADS_B, HEAD_DIM_B))
        else:
            assert kv_bufs[b].shape[1] == BRANCH_WINDOWS[b]
            shifted, seen_b = _cache_update(jnp.transpose(kv_bufs[b], (0, 2, 3, 4, 1)), kv[b],
                                            BRANCH_WINDOWS[b], BRANCH_DILATIONS[b])
            bufs_new.append(jnp.transpose(shifted, (0, 4, 1, 2, 3)))
            seen.append(seen_b)

    for l in range(N_LAYERS_A, N_LAYERS_A + N_LAYERS_B):
        lb = l - N_LAYERS_A
        if fresh:
            q, _ = _branch_proj(x, w['norm_mix'][l], w['w_q_b'][lb], False)
            parts = [_attn_prompt_branch(q[b], kv_lo[b], n, seq_len, BRANCH_DILATIONS[b]) for b in range(N_BRANCH)]
            mix = [p[0] for p in parts] + [p[1] for p in parts]
            w_o = w['w_o_b'][lb]
        else:
            q, = _norm_matmul(x, w['norm_mix'][l], _spread_heads(w['w_q_b'][lb], 1), 1024, False, (F32,))
            mix = [_attn_sample(q, jnp.stack(kv), seen)]
            w_o = _spread_heads(w['w_o_b'][lb], 0)
        x = _mlp_block(x, mix, w_o, w['norm_mlp'][l], w['w_up'], w['w_down'], l,
                       w['norm_final'], l == N_LAYERS_A + N_LAYERS_B - 1)
    return (x.reshape(n, seq_len, d), jnp.stack(rec_new), jnp.stack(conv_new), *bufs_new)


def kernel(x_prompt, x_sample, state_a_rec, state_a_conv, cache_b0_kv, cache_b1_kv, cache_b2_kv, norm_mix, norm_mlp, w_in_a, conv_a, a_log, dt_bias, norm_o_a, w_out_a, norm_kv, w_kv, w_q_b, w_o_b, w_up, w_down, norm_final):
    w = {'norm_mix': norm_mix, 'norm_mlp': norm_mlp, 'w_in_a': w_in_a, 'conv_a': conv_a, 'a_log': a_log,
         'dt_bias': dt_bias, 'norm_o_a': norm_o_a, 'w_out_a': w_out_a, 'norm_kv': norm_kv, 'w_kv': w_kv,
         'w_q_b': w_q_b, 'w_o_b': w_o_b, 'w_up': w_up, 'w_down': w_down, 'norm_final': norm_final}
    n_p = x_prompt.shape[0]
    p_rec0 = jnp.zeros((N_LAYERS_A, n_p, HEADS_A, DK_A, DV_A), state_a_rec.dtype)
    p_conv0 = jnp.zeros((N_LAYERS_A, n_p, CONV_W - 1, QKV_W_A), state_a_conv.dtype)
    prompt = _trunk(x_prompt, p_rec0, p_conv0, None, w)
    sample = _trunk(x_sample, state_a_rec, state_a_conv, [cache_b0_kv, cache_b1_kv, cache_b2_kv], w)
    return (prompt[0], sample[0], *prompt[1:], *sample[1:])
```

```python
import functools

import jax
import jax.numpy as jnp
from jax import lax
from jax.experimental import pallas as pl
from jax.experimental.pallas import tpu as pltpu

D_MODEL = 1024
HEADS_A = 8
DK_A = 128
DV_A = 128
KEY_W_A = HEADS_A * DK_A
VAL_W_A = HEADS_A * DV_A
QKV_W_A = 2 * KEY_W_A + VAL_W_A
QKVZ_W_A = QKV_W_A + VAL_W_A
CONV_W = 4
N_LAYERS_A = 2
N_LAYERS_B = 2
BRANCH_WINDOWS = (128, 512, 2048)
BRANCH_DILATIONS = (1, 4, 16)
N_BRANCH = 3
HEADS_B = 8
KV_HEADS_B = 2
GQA_B = HEADS_B // KV_HEADS_B
HEAD_DIM_B = 64
Q_W_BRANCH = HEADS_B * HEAD_DIM_B
KV_W_BRANCH = 2 * KV_HEADS_B * HEAD_DIM_B
ALIBI_SLOPES = tuple(2.0 ** (-8.0 * h / HEADS_B) for h in range(1, HEADS_B + 1))
EPS = 1e-6

LANES = 128
SUBLANES = 8
VMEM_LIMIT_BYTES = 56 * 1024 * 1024
ROW_TILE = 1024
MLP_FF_TILE = 512
GDN_CHUNK = 128
GDN_CHUNKS_PER_STEP = 2
ATT_BLOCK = 128

F32 = jnp.float32
BF16 = jnp.bfloat16


def _params(*sem):
    return pltpu.CompilerParams(dimension_semantics=sem, vmem_limit_bytes=VMEM_LIMIT_BYTES)


def _rms(x, gain):
    return x * lax.rsqrt(jnp.mean(x * x, axis=-1, keepdims=True) + EPS) * gain


def _sigmoid(x):
    return 1.0 / (1.0 + jnp.exp(-x))


def _softplus(x):
    return jnp.maximum(x, 0.0) + jnp.log(1.0 + jnp.exp(-jnp.abs(x)))


def _dot(a, b):
    return jnp.dot(a.astype(BF16), b.astype(BF16), preferred_element_type=F32)


def _dot_nt(a, b):
    return lax.dot_general(a.astype(BF16), b.astype(BF16), (((1,), (1,)), ((), ())),
                           preferred_element_type=F32)


def _dot_tn(a, b):
    return lax.dot_general(a.astype(BF16), b.astype(BF16), (((0,), (0,)), ((), ())),
                           preferred_element_type=F32)


def _dot_f32(a, b):
    return jnp.dot(a, b, precision=lax.Precision.HIGHEST, preferred_element_type=F32)


def _norm_matmul_kernel(x_ref, g_ref, w_ref, *rest, groups):
    *o_refs, xn_ref = rest
    j = pl.program_id(1)

    @pl.when(j == 0)
    def _():
        xn_ref[...] = _rms(x_ref[...], g_ref[...]).astype(BF16)

    y = jnp.dot(xn_ref[...], w_ref[...], preferred_element_type=F32)
    if groups == 1:
        for o_ref in o_refs:
            o_ref[...] = y.astype(o_ref.dtype)
    else:
        for g in range(groups):
            @pl.when(j == g)
            def _():
                for o_ref in o_refs[g::groups]:
                    o_ref[...] = y.astype(o_ref.dtype)


def _norm_matmul(x, gain, w, tn, split, dtypes):
    t, d = x.shape
    f = w.shape[1]
    tm = min(ROW_TILE, t)
    nj = f // tn
    if split:
        out_specs = [pl.BlockSpec((tm, tn), lambda i, j: (i, 0))] * (nj * len(dtypes))
        out_shape = [jax.ShapeDtypeStruct((t, tn), dt) for dt in dtypes for _ in range(nj)]
    else:
        out_specs = [pl.BlockSpec((tm, tn), lambda i, j: (i, j))] * len(dtypes)
        out_shape = [jax.ShapeDtypeStruct((t, f), dt) for dt in dtypes]
    outs = pl.pallas_call(
        functools.partial(_norm_matmul_kernel, groups=nj if split else 1),
        grid=(t // tm, nj),
        in_specs=[pl.BlockSpec((tm, d), lambda i, j: (i, 0)),
                  pl.BlockSpec((1, d), lambda i, j: (0, 0)),
                  pl.BlockSpec((d, tn), lambda i, j: (0, j))],
        out_specs=out_specs,
        out_shape=out_shape,
        scratch_shapes=[pltpu.VMEM((tm, d), BF16)],
        compiler_params=_params("parallel", "arbitrary"),
        name="norm_matmul",
    )(x, gain.reshape(1, d), w.astype(BF16))
    if split:
        return [outs[k * nj:(k + 1) * nj] for k in range(len(dtypes))]
    return outs


def _branch_proj_kernel(x_ref, g_ref, w_ref, *rest, keep_f32, scale):
    *o_refs, xn_ref, y_ref = rest
    chunks, tm, _ = y_ref.shape
    tn = chunks * LANES
    j = pl.program_id(1)

    @pl.when(j == 0)
    def _():
        xn_ref[...] = _rms(x_ref[...], g_ref[...]).astype(BF16)

    y = jnp.dot(xn_ref[...], w_ref[...], preferred_element_type=F32)
    if scale != 1.0:
        y = y * scale
    for b, dil in enumerate(BRANCH_DILATIONS):
        @pl.when(j == b)
        def _():
            if keep_f32:
                o_refs[N_BRANCH + b][...] = y
            if dil == 1:
                o_refs[b][...] = y.astype(BF16)
            else:
                for c in range(chunks):
                    y_ref[c] = y[:, c * LANES:(c + 1) * LANES]
                for r in range(dil):
                    for c in range(chunks):
                        lo = r * tn + c * LANES
                        o_refs[b][:, lo:lo + LANES] = y_ref[c, pl.ds(r, tm // dil, stride=dil), :].astype(BF16)


def _branch_proj(x, gain, w, keep_f32, scale=1.0):
    t, d = x.shape
    tn = w.shape[1] // N_BRANCH
    tm = min(ROW_TILE, t)
    row = lambda i, j: (i, 0)
    out_specs = [pl.BlockSpec((tm // dil, dil * tn), row) for dil in BRANCH_DILATIONS]
    out_shape = [jax.ShapeDtypeStruct((t // dil, dil * tn), BF16) for dil in BRANCH_DILATIONS]
    if keep_f32:
        out_specs += [pl.BlockSpec((tm, tn), row)] * N_BRANCH
        out_shape += [jax.ShapeDtypeStruct((t, tn), F32)] * N_BRANCH
    outs = pl.pallas_call(
        functools.partial(_branch_proj_kernel, keep_f32=keep_f32, scale=scale),
        grid=(t // tm, N_BRANCH),
        in_specs=[pl.BlockSpec((tm, d), row),
                  pl.BlockSpec((1, d), lambda i, j: (0, 0)),
                  pl.BlockSpec((d, tn), lambda i, j: (0, j))],
        out_specs=out_specs,
        out_shape=out_shape,
        scratch_shapes=[pltpu.VMEM((tm, d), BF16), pltpu.VMEM((tn // LANES, tm, LANES), F32)],
        compiler_params=_params("parallel", "arbitrary"),
        name="branch_proj",
    )(x, gain.reshape(1, d), w.astype(BF16))
    return outs[:N_BRANCH], outs[N_BRANCH:]


def _inproj_kernel(x_ref, g_ref, w_ref, wg_ref, o_ref, og_ref, xn_ref):
    @pl.when(pl.program_id(1) == 0)
    def _():
        xn = _rms(x_ref[...], g_ref[...]).astype(BF16)
        xn_ref[...] = xn
        og_ref[...] = jnp.dot(xn, wg_ref[...], preferred_element_type=F32)

    o_ref[...] = jnp.dot(xn_ref[...], w_ref[...], preferred_element_type=F32)


def _inproj(x, gain, w_in):
    t, d = x.shape
    tm = min(ROW_TILE, t)
    tn = 1024
    w_main = w_in[:, :QKVZ_W_A].astype(BF16)
    w_gate = jnp.pad(w_in[:, QKVZ_W_A:], ((0, 0), (0, LANES - 2 * HEADS_A))).astype(BF16)
    return pl.pallas_call(
        _inproj_kernel,
        grid=(t // tm, QKVZ_W_A // tn),
        in_specs=[pl.BlockSpec((tm, d), lambda i, j: (i, 0)),
                  pl.BlockSpec((1, d), lambda i, j: (0, 0)),
                  pl.BlockSpec((d, tn), lambda i, j: (0, j)),
                  pl.BlockSpec((d, LANES), lambda i, j: (0, 0))],
        out_specs=[pl.BlockSpec((tm, tn), lambda i, j: (i, j)),
                   pl.BlockSpec((tm, LANES), lambda i, j: (i, 0))],
        out_shape=[jax.ShapeDtypeStruct((t, QKVZ_W_A), F32),
                   jax.ShapeDtypeStruct((t, LANES), F32)],
        scratch_shapes=[pltpu.VMEM((tm, d), BF16)],
        compiler_params=_params("parallel", "arbitrary"),
        name="gdn_inproj",
    )(x, gain.reshape(1, d), w_main, w_gate)


def _inproj_conv_kernel(x_ref, g_ref, w_ref, wg_ref, cw_ref, c0_ref, o_ref, og_ref, tail_ref, xn_ref, halo_ref,
                        *, tiles_per_seq):
    i = pl.program_id(0)
    j = pl.program_id(1)
    halo = SUBLANES

    @pl.when(j == 0)
    def _():
        xn = _rms(x_ref[...], g_ref[...]).astype(BF16)
        xn_ref[...] = xn
        og_ref[...] = jnp.dot(xn, wg_ref[...], preferred_element_type=F32)

    y = jnp.dot(xn_ref[...], w_ref[...], preferred_element_type=F32)
    tm = y.shape[0]
    prev = jnp.where(i % tiles_per_seq == 0, c0_ref[...], halo_ref[j])
    ext = jnp.concatenate([prev, y], axis=0)
    conv = y * cw_ref[CONV_W - 1:CONV_W, :]
    for back in range(1, CONV_W):
        conv = conv + pltpu.roll(ext, back, axis=0)[halo:] * cw_ref[CONV_W - 1 - back:CONV_W - back, :]
    halo_ref[j] = y[tm - halo:]
    tail_ref[...] = y[tm - halo:]
    act = conv * _sigmoid(conv)
    col_head = lax.broadcasted_iota(jnp.int32, (y.shape[1], LANES), 0) // DK_A
    lane = lax.broadcasted_iota(jnp.int32, (y.shape[1], LANES), 1)
    sums = _dot(act * act, (col_head == lane).astype(BF16))
    unit = lax.rsqrt(sums + EPS) * jnp.where(j == 0, DK_A ** -0.5, 1.0)
    factor = jnp.where(j < 2, unit, 1.0)
    for h in range(HEADS_A):
        sl = slice(h * DK_A, (h + 1) * DK_A)
        o_ref[:, sl] = (act[:, sl] * factor[:, h:h + 1]).astype(o_ref.dtype)


def _inproj_conv(x, gain, w_in, w_conv, conv0, n, seq_len):
    t, d = x.shape
    tm = min(ROW_TILE, seq_len)
    tn = KEY_W_A
    halo = SUBLANES
    tiles_per_seq = seq_len // tm
    w_main = w_in[:, :QKVZ_W_A].astype(BF16)
    w_gate = jnp.pad(w_in[:, QKVZ_W_A:], ((0, 0), (0, LANES - 2 * HEADS_A))).astype(BF16)
    pass_taps = jnp.zeros((CONV_W, VAL_W_A), F32).at[CONV_W - 1].set(1.0)
    taps = jnp.concatenate([w_conv, pass_taps], axis=1)
    c0 = jnp.pad(conv0, ((0, 0), (halo - (CONV_W - 1), 0), (0, VAL_W_A)))
    conv_col = lambda i, j: (0, j)
    seq_col = lambda i, j: (i // tiles_per_seq, 0, j)
    return pl.pallas_call(
        functools.partial(_inproj_conv_kernel, tiles_per_seq=tiles_per_seq),
        grid=(t // tm, QKVZ_W_A // tn),
        in_specs=[pl.BlockSpec((tm, d), lambda i, j: (i, 0)),
                  pl.BlockSpec((1, d), lambda i, j: (0, 0)),
                  pl.BlockSpec((d, tn), lambda i, j: (0, j)),
                  pl.BlockSpec((d, LANES), lambda i, j: (0, 0)),
                  pl.BlockSpec((CONV_W, tn), conv_col),
                  pl.BlockSpec((None, halo, tn), seq_col)],
        out_specs=[pl.BlockSpec((tm, tn), lambda i, j: (i, j)),
                   pl.BlockSpec((tm, LANES), lambda i, j: (i, 0)),
                   pl.BlockSpec((None, halo, tn), lambda i, j: (i, 0, j))],
        out_shape=[jax.ShapeDtypeStruct((t, QKVZ_W_A), BF16),
                   jax.ShapeDtypeStruct((t, LANES), F32),
                   jax.ShapeDtypeStruct((t // tm, halo, QKVZ_W_A), F32)],
        scratch_shapes=[pltpu.VMEM((tm, d), BF16), pltpu.VMEM((QKVZ_W_A // tn, halo, tn), F32)],
        compiler_params=_params("arbitrary", "arbitrary"),
        name="gdn_inproj_conv",
    )(x, gain.reshape(1, d), w_main, w_gate, taps, c0)


def _mlp_kernel(*refs, n_mix, final_norm):
    x_ref = refs[0]
    mix_refs = refs[1:1 + n_mix]
    wo_ref, g_ref, wup_ref, wdn_ref, gf_ref, y_ref, xn_ref = refs[1 + n_mix:1 + n_mix + 7]
    nat_refs = refs[1 + n_mix + 7:]
    f = pl.program_id(1)

    @pl.when(f == 0)
    def _():
        if n_mix == 1:
            mixed = mix_refs[0][...]
        else:
            tm = x_ref.shape[0]
            chunks = Q_W_BRANCH // LANES
            for b, dil in enumerate(BRANCH_DILATIONS):
                o_blk, lse_blk = mix_refs[b], mix_refs[N_BRANCH + b]
                o_nat, lse_nat = nat_refs[b], nat_refs[N_BRANCH + b]
                rows = tm // dil
                for r in range(dil):
                    dst = pl.ds(r, rows, stride=dil) if dil > 1 else pl.ds(0, rows)
                    for c in range(chunks):
                        lo = r * Q_W_BRANCH + c * LANES
                        o_nat[c, dst, :] = o_blk[:, lo:lo + LANES].astype(F32)
                    lse_nat[dst, :] = lse_blk[r]
            lses = [r[...] for r in nat_refs[N_BRANCH:]]
            m = functools.reduce(jnp.maximum, lses)
            ws = [jnp.exp(l - m) for l in lses]
            total = sum(ws)
            low = lax.broadcasted_iota(jnp.int32, (1, 2 * HEAD_DIM_B), 1) < HEAD_DIM_B
            mixed = 0.0
            for w, o_nat in zip(ws, nat_refs[:N_BRANCH]):
                w = w / total
                mixed = mixed + jnp.concatenate(
                    [jnp.where(low, w[:, 2 * p:2 * p + 1], w[:, 2 * p + 1:2 * p + 2]) * o_nat[p]
                     for p in range(HEADS_B // 2)], axis=1)
        x1 = x_ref[...] + _dot(mixed, wo_ref[...])
        y_ref[...] = x1
        xn_ref[...] = _rms(x1, g_ref[...]).astype(BF16)

    u = jnp.maximum(jnp.dot(xn_ref[...], wup_ref[...], preferred_element_type=F32), 0.0)
    y_ref[...] += _dot(u * u, wdn_ref[...])

    if final_norm:
        @pl.when(f == pl.num_programs(1) - 1)
        def _():
            y_ref[...] = _rms(y_ref[...], gf_ref[...])


def _mlp_block(x, mix, w_o, gain, w_up, w_down, layer, gain_final, final_norm):
    t, d = x.shape
    k = w_o.shape[0]
    dff = w_up.shape[2]
    tm = min(ROW_TILE, t)
    tf = MLP_FF_TILE
    n_mix = len(mix)
    row = lambda i, f: (i, 0)
    const = lambda i, f: (0, 0)
    scratch = [pltpu.VMEM((tm, d), BF16)]
    if n_mix == 1:
        mix_specs = [pl.BlockSpec((tm, k), row)]
    else:
        mix_specs = [pl.BlockSpec((tm // dil, dil * Q_W_BRANCH), row) for dil in BRANCH_DILATIONS]
        for dil, lse in zip(BRANCH_DILATIONS, mix[N_BRANCH:]):
            per_seq = lse.shape[2] // (tm // dil)
            mix_specs.append(pl.BlockSpec((None, dil, tm // dil, LANES),
                                          lambda i, f, per_seq=per_seq: (i // per_seq, 0, i % per_seq, 0)))
        scratch += ([pltpu.VMEM((Q_W_BRANCH // LANES, tm, LANES), F32)] * N_BRANCH
                    + [pltpu.VMEM((tm, LANES), F32)] * N_BRANCH)
    return pl.pallas_call(
        functools.partial(_mlp_kernel, n_mix=n_mix, final_norm=final_norm),
        grid=(t // tm, dff // tf),
        in_specs=([pl.BlockSpec((tm, d), row)] + mix_specs
                  + [pl.BlockSpec((k, d), const),
                     pl.BlockSpec((1, d), const),
                     pl.BlockSpec((None, d, tf), lambda i, f: (layer, 0, f)),
                     pl.BlockSpec((None, tf, d), lambda i, f: (layer, f, 0)),
                     pl.BlockSpec((1, d), const)]),
        out_specs=pl.BlockSpec((tm, d), row),
        out_shape=jax.ShapeDtypeStruct((t, d), F32),
        scratch_shapes=scratch,
        compiler_params=_params("parallel", "arbitrary"),
        name="mix_out_mlp",
    )(x, *mix, w_o.astype(BF16), gain.reshape(1, d), w_up.astype(BF16), w_down.astype(BF16),
      gain_final.reshape(1, d))


def _inverse_masks(c):
    row = lax.broadcasted_iota(jnp.int32, (c, c), 0)
    col = lax.broadcasted_iota(jnp.int32, (c, c), 1)
    masks = []
    shift = 0
    while (1 << shift) < c:
        same_pair = (row >> (shift + 1)) == (col >> (shift + 1))
        other_half = (row >> shift) != (col >> shift)
        masks.append((row > col) & same_pair & other_half)
        shift += 1
    return masks


def _unit_lower_inverses(mats, c):
    masks = _inverse_masks(c)
    row = lax.broadcasted_iota(jnp.int32, (c, c), 0)
    col = lax.broadcasted_iota(jnp.int32, (c, c), 1)
    eye = (row == col).astype(F32)
    invs = [eye - jnp.where(masks[0], a, 0.0) for a in mats]
    for mask in masks[1:]:
        xs = [_dot(inv, jnp.where(mask, a, 0.0)) for inv, a in zip(invs, mats)]
        invs = [inv - _dot(x, inv) for inv, x in zip(invs, xs)]
    return invs


def _gdn_kernel(q_ref, k_ref, v_ref, z_ref, ba_ref, bat_ref, s0_ref, alr_ref, dtr_ref, alc_ref, dtc_ref, gn_ref,
                o_ref, s_ref, *, chunk, chunks_per_step):
    c = chunk

    @pl.when(pl.program_id(1) == 0)
    def _():
        s_ref[...] = s0_ref[...]

    row = lax.broadcasted_iota(jnp.int32, (c, c), 0)
    col = lax.broadcasted_iota(jnp.int32, (c, c), 1)
    causal = row >= col
    strict = row > col
    lower_ones = causal.astype(F32)
    upper_ones = (row <= col).astype(F32)
    heads = range(HEADS_A)
    pairs = [(ck, h) for ck in range(chunks_per_step) for h in heads]
    rows_of = [slice(ck * c, (ck + 1) * c) for ck in range(chunks_per_step)]
    cols_of = [slice(h * DK_A, (h + 1) * DK_A) for h in heads]

    beta_cols, cum_cols, cum_rows = [], [], []
    for ck in range(chunks_per_step):
        ba = ba_ref[rows_of[ck], :]
        bat = bat_ref[ck]
        beta_cols.append(_sigmoid(ba))
        g_cols = -jnp.exp(alr_ref[...]) * _softplus(ba + dtr_ref[...])
        g_rows = -jnp.exp(alc_ref[...]) * _softplus(bat + dtc_ref[...])
        cum_cols.append(_dot_f32(lower_ones, g_cols))
        cum_rows.append(_dot_f32(g_rows, upper_ones))

    qs = [q_ref[rows_of[ck], cols_of[h]] for ck, h in pairs]
    ks = [k_ref[rows_of[ck], cols_of[h]] for ck, h in pairs]
    kfs = [k.astype(F32) for k in ks]
    betas = [beta_cols[ck][:, h:h + 1] for ck, h in pairs]
    gcs = [cum_cols[ck][:, HEADS_A + h:HEADS_A + h + 1] for ck, h in pairs]
    grs = [cum_rows[ck][HEADS_A + h:HEADS_A + h + 1, :] for ck, h in pairs]
    decays = [jnp.where(causal, jnp.exp(jnp.where(causal, gc - gr, 0.0)), 0.0) for gc, gr in zip(gcs, grs)]
    kbs = [k * b for k, b in zip(kfs, betas)]
    kq = [_dot_nt(jnp.concatenate([kb.astype(BF16), q], axis=0), k) for kb, q, k in zip(kbs, qs, ks)]
    mats = [jnp.where(strict, x[:c] * d, 0.0) for x, d in zip(kq, decays)]
    attns = [x[c:] * d for x, d in zip(kq, decays)]
    t_invs = _unit_lower_inverses(mats, c)
    e_gcs = [jnp.exp(gc) for gc in gcs]
    uws = [_dot(t, jnp.concatenate([v_ref[rows_of[ck], cols_of[h]].astype(F32) * b, kb * e], axis=1))
           for t, (ck, h), b, kb, e in zip(t_invs, pairs, betas, kbs, e_gcs)]
    q_decs = [q.astype(F32) * e for q, e in zip(qs, e_gcs)]

    states = [s_ref[h] for h in heads]
    for ck in range(chunks_per_step):
        at = slice(ck * HEADS_A, (ck + 1) * HEADS_A)
        wq_s = [_dot(jnp.concatenate([uw[:, DV_A:], qd], axis=0), s)
                for uw, qd, s in zip(uws[at], q_decs[at], states)]
        v_news = [uw[:, :DV_A] - x[:c] for uw, x in zip(uws[at], wq_s)]
        outs = [x[c:] + _dot(attn, vn) for x, attn, vn in zip(wq_s, attns[at], v_news)]
        g_lasts = [gc[c - 1:c, :] for gc in gcs[at]]
        states = [s * jnp.exp(gl) + _dot_tn(kf * jnp.exp(gl - gc), vn)
                  for s, gl, kf, gc, vn in zip(states, g_lasts, kfs[at], gcs[at], v_news)]
        for h in heads:
            gate = z_ref[rows_of[ck], cols_of[h]].astype(F32)
            o_ref[rows_of[ck], cols_of[h]] = (_rms(outs[h], gn_ref[...]) * gate).astype(o_ref.dtype)
    for h in heads:
        s_ref[h] = states[h]


def _gdn(act, ba, n, seq_len, chunk, s0, a_log, dt_bias, g_norm):
    nc = seq_len // chunk
    per_step = GDN_CHUNKS_PER_STEP
    steps = nc // per_step
    rows = per_step * chunk
    t = n * seq_len
    bat = ba[:, :2 * HEADS_A].reshape(n * nc, chunk, 2 * HEADS_A).transpose(0, 2, 1)
    pad_row = lambda p: jnp.pad(p.reshape(1, HEADS_A), ((0, 0), (HEADS_A, LANES - 2 * HEADS_A)))
    pad_col = lambda p: jnp.pad(p.reshape(HEADS_A, 1), ((HEADS_A, 0), (0, 0)))
    blk = lambda i, j: (i * steps + j, 0)
    const = lambda i, j: (0, 0)
    col_blk = lambda col: pl.BlockSpec((rows, KEY_W_A), lambda i, j: (i * steps + j, col))
    return pl.pallas_call(
        functools.partial(_gdn_kernel, chunk=chunk, chunks_per_step=per_step),
        grid=(n, steps),
        in_specs=[col_blk(0), col_blk(1), col_blk(2), col_blk(3),
                  pl.BlockSpec((rows, LANES), blk),
                  pl.BlockSpec((per_step, 2 * HEADS_A, chunk), lambda i, j: (i * steps + j, 0, 0)),
                  pl.BlockSpec((None, HEADS_A, DK_A, DV_A), lambda i, j: (i, 0, 0, 0)),
                  pl.BlockSpec((1, LANES), const),
                  pl.BlockSpec((1, LANES), const),
                  pl.BlockSpec((2 * HEADS_A, 1), const),
                  pl.BlockSpec((2 * HEADS_A, 1), const),
                  pl.BlockSpec((1, DV_A), const)],
        out_specs=[pl.BlockSpec((rows, VAL_W_A), blk),
                   pl.BlockSpec((None, HEADS_A, DK_A, DV_A), lambda i, j: (i, 0, 0, 0))],
        out_shape=[jax.ShapeDtypeStruct((t, VAL_W_A), BF16),
                   jax.ShapeDtypeStruct((n, HEADS_A, DK_A, DV_A), F32)],
        compiler_params=_params("parallel", "arbitrary"),
        name="gdn_delta_rule",
    )(act, act, act, act, ba, bat, s0, pad_row(a_log), pad_row(dt_bias), pad_col(a_log), pad_col(dt_bias),
      g_norm.reshape(1, DV_A))


def _gdn_step_kernel(qkv_ref, z_ref, ba_ref, st_ref, cw_ref, s0_ref, alr_ref, dtr_ref, gn_ref, o_ref, s_ref):
    nb = qkv_ref.shape[0]
    conv = qkv_ref[...] * cw_ref[CONV_W - 1:CONV_W, :]
    for j in range(CONV_W - 1):
        conv = conv + st_ref[j] * cw_ref[j:j + 1, :]
    act = conv * _sigmoid(conv)
    ba = ba_ref[...]
    beta_cols = _sigmoid(ba)
    decay_cols = jnp.exp(-jnp.exp(alr_ref[...]) * _softplus(ba + dtr_ref[...]))
    fill = jnp.zeros((DK_A - nb, DK_A), F32)
    for h in range(HEADS_A):
        lo = h * DK_A
        q = act[:, lo:lo + DK_A]
        k = act[:, KEY_W_A + lo:KEY_W_A + lo + DK_A]
        v = act[:, 2 * KEY_W_A + lo:2 * KEY_W_A + lo + DV_A]
        q = q * lax.rsqrt(jnp.sum(q * q, axis=-1, keepdims=True) + EPS) * (DK_A ** -0.5)
        k = k * lax.rsqrt(jnp.sum(k * k, axis=-1, keepdims=True) + EPS)
        q_t = jnp.concatenate([q, fill], axis=0).T
        k_t = jnp.concatenate([k, fill], axis=0).T
        z = z_ref[:, lo:lo + DV_A]
        gate = z * _sigmoid(z)
        seqs = range(nb)
        k_cols = [k_t[:, i:i + 1] for i in seqs]
        decays = [decay_cols[i:i + 1, HEADS_A + h:HEADS_A + h + 1] for i in seqs]
        k_s = [jnp.sum(k_cols[i] * s0_ref[i, h], axis=0, keepdims=True) for i in seqs]
        v_new = [beta_cols[i:i + 1, h:h + 1] * (v[i:i + 1, :] - decays[i] * k_s[i]) for i in seqs]
        s_new = [decays[i] * s0_ref[i, h] + k_cols[i] * v_new[i] for i in seqs]
        for i in seqs:
            s_ref[i, h] = s_new[i]
        outs = [jnp.sum(q_t[:, i:i + 1] * s_new[i], axis=0, keepdims=True) for i in seqs]
        for i in seqs:
            o_ref[i:i + 1, lo:lo + DV_A] = _rms(outs[i], gn_ref[...]) * gate[i:i + 1, :]


def _gdn_step(qkvz, ba, s0_layers, conv0_layers, layer, w_conv, a_log, dt_bias, g_norm):
    n = qkvz.shape[0]
    nb = SUBLANES
    pad_row = lambda p: jnp.pad(p.reshape(1, HEADS_A), ((0, 0), (HEADS_A, LANES - 2 * HEADS_A)))
    blk = lambda i: (i, 0)
    const = lambda i: (0, 0)
    return pl.pallas_call(
        _gdn_step_kernel,
        grid=(n // nb,),
        in_specs=[pl.BlockSpec((nb, QKV_W_A), blk),
                  pl.BlockSpec((nb, VAL_W_A), lambda i: (i, QKV_W_A // VAL_W_A)),
                  pl.BlockSpec((nb, LANES), blk),
                  pl.BlockSpec((None, CONV_W - 1, nb, QKV_W_A), lambda i: (layer, 0, i, 0)),
                  pl.BlockSpec((CONV_W, QKV_W_A), const),
                  pl.BlockSpec((None, nb, HEADS_A, DK_A, DV_A), lambda i: (layer, i, 0, 0, 0)),
                  pl.BlockSpec((1, LANES), const),
                  pl.BlockSpec((1, LANES), const),
                  pl.BlockSpec((1, DV_A), const)],
        out_specs=[pl.BlockSpec((nb, VAL_W_A), blk),
                   pl.BlockSpec((nb, HEADS_A, DK_A, DV_A), lambda i: (i, 0, 0, 0))],
        out_shape=[jax.ShapeDtypeStruct((n, VAL_W_A), F32),
                   jax.ShapeDtypeStruct((n, HEADS_A, DK_A, DV_A), F32)],
        compiler_params=_params("parallel"),
        name="gdn_step",
    )(qkvz, qkvz, ba, conv0_layers.transpose(0, 2, 1, 3), w_conv, s0_layers, pad_row(a_log), pad_row(dt_bias),
      g_norm.reshape(1, DV_A))


def _attn_prompt_kernel(q_ref, kvc_ref, kvp_ref, o_ref, lse_ref, *, dilation, sub_blocks):
    blk = ATT_BLOCK
    hd = HEAD_DIM_B
    a = pl.program_id(2)
    qi = lax.broadcasted_iota(jnp.int32, (blk, 2 * blk), 0)
    kj = lax.broadcasted_iota(jnp.int32, (blk, 2 * blk), 1)
    steps = blk + qi - kj
    in_window = (steps >= 0) & (steps <= blk)
    started = in_window & ((kj >= blk) | (a > 0))
    dist = (steps * dilation).astype(F32)
    bias_first = [jnp.where(started, -slope * dist, -jnp.inf) for slope in ALIBI_SLOPES]
    bias_later = [jnp.where(in_window, -slope * dist, -jnp.inf) for slope in ALIBI_SLOPES]
    low = lax.broadcasted_iota(jnp.int32, (1, 2 * hd), 1) < hd
    kv_all = jnp.concatenate([kvp_ref[...], kvc_ref[...]], axis=0)
    padded = []
    for kvh in range(KV_HEADS_B):
        own = low if kvh == 0 else ~low
        both = []
        for x in (kv_all[:, :2 * hd], kv_all[:, 2 * hd:]):
            kept = jnp.where(own, x, jnp.zeros_like(x))
            moved = pltpu.roll(kept, hd, axis=1)
            both += [kept, moved] if kvh == 0 else [moved, kept]
        padded.append(both)
    head_lane = lax.broadcasted_iota(jnp.int32, (1, LANES), 1)
    for j in range(sub_blocks):
        lse_all = jnp.zeros((blk, LANES), F32)
        bias = bias_first if j == 0 else bias_later
        rows = slice(j * blk, (j + 1) * blk)
        keys = slice(j * blk, (j + 2) * blk)
        for kvh in range(KV_HEADS_B):
            k_lo, k_hi, v_lo, v_hi = (x[keys] for x in padded[kvh])
            heads = [kvh * GQA_B + g for g in range(GQA_B)]
            qs = [q_ref[rows, (h // 2) * 2 * hd:(h // 2 + 1) * 2 * hd] for h in heads]
            ss = [_dot_nt(q, k_hi if h % 2 else k_lo) + bias[h] for q, h in zip(qs, heads)]
            ms = [jnp.max(s, axis=-1, keepdims=True) for s in ss]
            ps = [jnp.exp(s - m) for s, m in zip(ss, ms)]
            ls = [jnp.sum(p, axis=-1, keepdims=True) for p in ps]
            for g in range(0, GQA_B, 2):
                pair = (heads[g] // 2) * 2 * hd
                o = _dot(ps[g], v_lo) + _dot(ps[g + 1], v_hi)
                o_ref[rows, pair:pair + 2 * hd] = (o / jnp.where(low, ls[g], ls[g + 1])).astype(o_ref.dtype)
            for g, h in enumerate(heads):
                lse_all = jnp.where(head_lane == h, ms[g] + jnp.log(ls[g]), lse_all)
        lse_ref[rows, :] = lse_all


def _attn_prompt_branch(q, kv, n, seq_len, dilation):
    ls = seq_len // dilation
    sub_blocks = min(4, ls // ATT_BLOCK)
    qb = sub_blocks * ATT_BLOCK
    qv = q.reshape(n, ls, dilation * Q_W_BRANCH)
    kvv = kv.reshape(n, ls, dilation * KV_W_BRANCH)
    cur = lambda i, r, a: (i, a, r)
    prev = lambda i, r, a: (i, jnp.maximum(a * sub_blocks - 1, 0), r)
    o, lse = pl.pallas_call(
        functools.partial(_attn_prompt_kernel, dilation=dilation, sub_blocks=sub_blocks),
        grid=(n, dilation, ls // qb),
        in_specs=[pl.BlockSpec((None, qb, Q_W_BRANCH), cur),
                  pl.BlockSpec((None, qb, KV_W_BRANCH), cur),
                  pl.BlockSpec((None, ATT_BLOCK, KV_W_BRANCH), prev)],
        out_specs=[pl.BlockSpec((None, qb, Q_W_BRANCH), cur),
                   pl.BlockSpec((None, None, qb, LANES), lambda i, r, a: (i, r, a, 0))],
        out_shape=[jax.ShapeDtypeStruct(qv.shape, BF16),
                   jax.ShapeDtypeStruct((n, dilation, ls, LANES), F32)],
        compiler_params=_params("parallel", "parallel", "arbitrary"),
        name=f"attn_prompt_d{dilation}",
    )(qv, kvv, kvv)
    return o.reshape(n * ls, dilation * Q_W_BRANCH), lse


def _cache_update_kernel(c_ref, kvn_ref, o_ref, g_ref, *, window, dilation):
    nb = c_ref.shape[0]
    nk = ATT_BLOCK
    last = lax.broadcasted_iota(jnp.int32, (1, window), 1) == window - 1
    new_t = jnp.concatenate([kvn_ref[...], jnp.zeros((LANES - nb, KV_W_BRANCH), F32)], axis=0).T
    if dilation > 1:
        src = lax.broadcasted_iota(jnp.int32, (window, nk), 0)
        dst = lax.broadcasted_iota(jnp.int32, (window, nk), 1)
        pick = (src == dst * dilation).astype(BF16)
    for i in range(nb):
        x = c_ref[i].reshape(KV_W_BRANCH, window)
        shifted = jnp.where(last, new_t[:, i:i + 1], pltpu.roll(x, window - 1, axis=1))
        o_ref[i] = shifted.reshape(o_ref.shape[1:])
        seen = x.astype(BF16) if dilation == 1 else _dot(x, pick).astype(BF16)
        g_ref[i] = seen.reshape(g_ref.shape[1:])


def _cache_update(cache_t, kv_new, window, dilation):
    n = cache_t.shape[0]
    nb = max(1, min(SUBLANES, 4096 // window))
    blk = (nb, 2, KV_HEADS_B, HEAD_DIM_B, window)
    seen_blk = (nb, 2, KV_HEADS_B * HEAD_DIM_B, ATT_BLOCK)
    return pl.pallas_call(
        functools.partial(_cache_update_kernel, window=window, dilation=dilation),
        grid=(n // nb,),
        in_specs=[pl.BlockSpec(blk, lambda i: (i, 0, 0, 0, 0)),
                  pl.BlockSpec((None, nb, KV_W_BRANCH), lambda i: (i, 0, 0))],
        out_specs=[pl.BlockSpec(blk, lambda i: (i, 0, 0, 0, 0)),
                   pl.BlockSpec(seen_blk, lambda i: (i, 0, 0, 0))],
        out_shape=[jax.ShapeDtypeStruct(cache_t.shape, F32),
                   jax.ShapeDtypeStruct((n,) + seen_blk[1:], BF16)],
        compiler_params=_params("parallel"),
        name=f"cache_update_w{window}",
    )(cache_t, kv_new.reshape(n // nb, nb, KV_W_BRANCH))


def _attn_sample_kernel(q_ref, kvn_ref, g0_ref, g1_ref, g2_ref, o_ref):
    nk = ATT_BLOCK
    nb = q_ref.shape[0]
    kv_w = KV_HEADS_B * HEAD_DIM_B
    head = lax.broadcasted_iota(jnp.int32, (HEADS_B, 1), 0)
    slopes = jnp.exp2(-8.0 * (head + 1).astype(F32) / HEADS_B)
    key = lax.broadcasted_iota(jnp.int32, (1, nk), 1)
    lane = lax.broadcasted_iota(jnp.int32, (HEADS_B, kv_w), 1)
    own_half = (lane // HEAD_DIM_B) == (head // GQA_B)
    scale = HEAD_DIM_B ** -0.5
    g_refs = (g0_ref, g1_ref, g2_ref)
    pairs = [(i, b) for i in range(nb) for b in range(N_BRANCH)]
    dists = [((nk - key) * d).astype(F32) for d in BRANCH_DILATIONS]
    qs = [q_ref[i, b * HEADS_B:(b + 1) * HEADS_B, :] * scale for i, b in pairs]
    s_old = [_dot(q, g_refs[b][i, 0]) - slopes * dists[b] for q, (i, b) in zip(qs, pairs)]
    s_new = [jnp.sum(q * kvn_ref[b, i:i + 1, :kv_w], axis=-1, keepdims=True) for q, (i, b) in zip(qs, pairs)]
    ms = [jnp.maximum(jnp.max(so, axis=-1, keepdims=True), sn) for so, sn in zip(s_old, s_new)]
    p_old = [jnp.exp(so - m) for so, m in zip(s_old, ms)]
    p_new = [jnp.exp(sn - m) for sn, m in zip(s_new, ms)]
    ls = [jnp.sum(po, axis=-1, keepdims=True) + pn for po, pn in zip(p_old, p_new)]
    outs = [(_dot_nt(po, g_refs[b][i, 1]) + pn * kvn_ref[b, i:i + 1, kv_w:]) / l
            for po, pn, l, (i, b) in zip(p_old, p_new, ls, pairs)]
    lses = [m + jnp.log(l) for m, l in zip(ms, ls)]
    for i in range(nb):
        sl = slice(i * N_BRANCH, (i + 1) * N_BRANCH)
        m = functools.reduce(jnp.maximum, lses[sl])
        ws = [jnp.exp(l - m) for l in lses[sl]]
        merged = sum(w * o for w, o in zip(ws, outs[sl])) / sum(ws)
        o_ref[i] = jnp.where(own_half, merged, 0.0)


def _spread_heads(w, axis):
    w = jnp.moveaxis(w, axis, -1)
    lead = w.shape[:-1]
    w = w.reshape(lead + (-1, KV_HEADS_B, GQA_B, HEAD_DIM_B))
    halves = [jnp.pad(w[..., g, :, :], [(0, 0)] * (len(lead) + 2) + [(g * HEAD_DIM_B, (KV_HEADS_B - 1 - g) * HEAD_DIM_B)])
              for g in range(KV_HEADS_B)]
    out = jnp.stack(halves, axis=-3)
    return jnp.moveaxis(out.reshape(lead + (-1,)), -1, axis)


def _attn_sample(q, kv_new, seen):
    n = q.shape[0]
    nb = SUBLANES
    rows = N_BRANCH * HEADS_B
    seen_blk = (nb, 2, KV_HEADS_B * HEAD_DIM_B, ATT_BLOCK)
    out = pl.pallas_call(
        _attn_sample_kernel,
        grid=(n // nb,),
        in_specs=[pl.BlockSpec((nb, rows, LANES), lambda i: (i, 0, 0)),
                  pl.BlockSpec((N_BRANCH, nb, KV_W_BRANCH), lambda i: (0, i, 0))]
                 + [pl.BlockSpec(seen_blk, lambda i: (i, 0, 0, 0))] * N_BRANCH,
        out_specs=pl.BlockSpec((nb, HEADS_B, LANES), lambda i: (i, 0, 0)),
        out_shape=jax.ShapeDtypeStruct((n, HEADS_B, LANES), F32),
        compiler_params=_params("parallel"),
        name="attn_sample",
    )(q.reshape(n, rows, LANES), kv_new, *seen)
    return out.reshape(n, HEADS_B * LANES)


def _trunk(x, rec0, conv0, kv_bufs, w):
    n, seq_len, d = x.shape
    fresh = kv_bufs is None
    assert seq_len == 1 or (fresh and seq_len % (ATT_BLOCK * max(BRANCH_DILATIONS)) == 0)
    x = x.reshape(n * seq_len, d)
    rec_new, conv_new = [], []
    for l in range(N_LAYERS_A):
        if fresh:
            act, ba, tail = _inproj_conv(x, w['norm_mix'][l], w['w_in_a'][l], w['conv_a'][l], conv0[l], n, seq_len)
            last_tile = tail.reshape(n, -1, SUBLANES, QKVZ_W_A)[:, -1]
            conv_new.append(last_tile[:, SUBLANES - (CONV_W - 1):, :QKV_W_A])
            o, s_new = _gdn(act, ba, n, seq_len, GDN_CHUNK, rec0[l], w['a_log'][l], w['dt_bias'][l],
                            w['norm_o_a'][l])
        else:
            qkvz, ba = _inproj(x, w['norm_mix'][l], w['w_in_a'][l])
            conv_new.append(jnp.concatenate([conv0[l][:, 1:], qkvz[:, None, :QKV_W_A]], axis=1))
            o, s_new = _gdn_step(qkvz, ba, rec0, conv0, l, w['conv_a'][l], w['a_log'][l], w['dt_bias'][l],
                                 w['norm_o_a'][l])
        rec_new.append(s_new)
        x = _mlp_block(x, [o], w['w_out_a'][l], w['norm_mlp'][l], w['w_up'], w['w_down'], l,
                       w['norm_final'], False)

    if fresh:
        kv_lo, kv = _branch_proj(x, w['norm_kv'], w['w_kv'], True)
    else:
        kv, = _norm_matmul(x, w['norm_kv'], w['w_kv'], KV_W_BRANCH, True, (F32,))
    bufs_new, seen = [], []
    for b in range(N_BRANCH):
        if fresh:
            keep = min(BRANCH_WINDOWS[b], seq_len)
            last = kv[b].reshape(n, seq_len, KV_W_BRANCH)[:, seq_len - keep:]
            bufs_new.append(last.reshape(n, keep, 2, KV_HEADS_B, HEAD_DIM_B))
        else:
            assert kv_bufs[b].shape[1] == BRANCH_WINDOWS[b]
            shifted, seen_b = _cache_update(jnp.transpose(kv_bufs[b], (0, 2, 3, 4, 1)), kv[b],
                                            BRANCH_WINDOWS[b], BRANCH_DILATIONS[b])
            bufs_new.append(jnp.transpose(shifted, (0, 4, 1, 2, 3)))
            seen.append(seen_b)

    for l in range(N_LAYERS_A, N_LAYERS_A + N_LAYERS_B):
        lb = l - N_LAYERS_A
        if fresh:
            q, _ = _branch_proj(x, w['norm_mix'][l], w['w_q_b'][lb], False, HEAD_DIM_B ** -0.5)
            parts = [_attn_prompt_branch(q[b], kv_lo[b], n, seq_len, BRANCH_DILATIONS[b]) for b in range(N_BRANCH)]
            mix = [p[0] for p in parts] + [p[1] for p in parts]
            w_o = w['w_o_b'][lb]
        else:
            q, = _norm_matmul(x, w['norm_mix'][l], _spread_heads(w['w_q_b'][lb], 1), 1024, False, (F32,))
            mix = [_attn_sample(q, jnp.stack(kv), seen)]
            w_o = _spread_heads(w['w_o_b'][lb], 0)
        x = _mlp_block(x, mix, w_o, w['norm_mlp'][l], w['w_up'], w['w_down'], l,
                       w['norm_final'], l == N_LAYERS_A + N_LAYERS_B - 1)
    return (x.reshape(n, seq_len, d), jnp.stack(rec_new), jnp.stack(conv_new), *bufs_new)


def kernel(x_prompt, x_sample, state_a_rec, state_a_conv, cache_b0_kv, cache_b1_kv, cache_b2_kv, norm_mix, norm_mlp, w_in_a, conv_a, a_log, dt_bias, norm_o_a, w_out_a, norm_kv, w_kv, w_q_b, w_o_b, w_up, w_down, norm_final):
    w = {'norm_mix': norm_mix, 'norm_mlp': norm_mlp, 'w_in_a': w_in_a, 'conv_a': conv_a, 'a_log': a_log,
         'dt_bias': dt_bias, 'norm_o_a': norm_o_a, 'w_out_a': w_out_a, 'norm_kv': norm_kv, 'w_kv': w_kv,
         'w_q_b': w_q_b, 'w_o_b': w_o_b, 'w_up': w_up, 'w_down': w_down, 'norm_final': norm_final}
    n_p = x_prompt.shape[0]
    p_rec0 = jnp.zeros((N_LAYERS_A, n_p, HEADS_A, DK_A, DV_A), state_a_rec.dtype)
    p_conv0 = jnp.zeros((N_LAYERS_A, n_p, CONV_W - 1, QKV_W_A), state_a_conv.dtype)
    prompt = _trunk(x_prompt, p_rec0, p_conv0, None, w)
    sample = _trunk(x_sample, state_a_rec, state_a_conv, [cache_b0_kv, cache_b1_kv, cache_b2_kv], w)
    return (prompt[0], sample[0], *prompt[1:], *sample[1:])
```

```python
import functools

import jax
import jax.numpy as jnp
from jax import lax
from jax.experimental import pallas as pl
from jax.experimental.pallas import tpu as pltpu

D_MODEL = 1024
HEADS_A = 8
DK_A = 128
DV_A = 128
KEY_W_A = HEADS_A * DK_A
VAL_W_A = HEADS_A * DV_A
QKV_W_A = 2 * KEY_W_A + VAL_W_A
QKVZ_W_A = QKV_W_A + VAL_W_A
CONV_W = 4
N_LAYERS_A = 2
N_LAYERS_B = 2
BRANCH_WINDOWS = (128, 512, 2048)
BRANCH_DILATIONS = (1, 4, 16)
N_BRANCH = 3
HEADS_B = 8
KV_HEADS_B = 2
GQA_B = HEADS_B // KV_HEADS_B
HEAD_DIM_B = 64
Q_W_BRANCH = HEADS_B * HEAD_DIM_B
KV_W_BRANCH = 2 * KV_HEADS_B * HEAD_DIM_B
ALIBI_SLOPES = tuple(2.0 ** (-8.0 * h / HEADS_B) for h in range(1, HEADS_B + 1))
EPS = 1e-6

LANES = 128
SUBLANES = 8
VMEM_LIMIT_BYTES = 56 * 1024 * 1024
ROW_TILE = 1024
MLP_FF_TILE = 512
GDN_CHUNK = 128
GDN_CHUNKS_PER_STEP = 2
ATT_BLOCK = 128

F32 = jnp.float32
BF16 = jnp.bfloat16


def _params(*sem):
    return pltpu.CompilerParams(dimension_semantics=sem, vmem_limit_bytes=VMEM_LIMIT_BYTES)


def _rms(x, gain):
    return x * lax.rsqrt(jnp.mean(x * x, axis=-1, keepdims=True) + EPS) * gain


def _sigmoid(x):
    return 1.0 / (1.0 + jnp.exp(-x))


def _softplus(x):
    return jnp.maximum(x, 0.0) + jnp.log(1.0 + jnp.exp(-jnp.abs(x)))


def _dot(a, b):
    return jnp.dot(a.astype(BF16), b.astype(BF16), preferred_element_type=F32)


def _dot_nt(a, b):
    return lax.dot_general(a.astype(BF16), b.astype(BF16), (((1,), (1,)), ((), ())),
                           preferred_element_type=F32)


def _dot_tn(a, b):
    return lax.dot_general(a.astype(BF16), b.astype(BF16), (((0,), (0,)), ((), ())),
                           preferred_element_type=F32)


def _dot_f32(a, b):
    return jnp.dot(a, b, precision=lax.Precision.HIGHEST, preferred_element_type=F32)


def _norm_matmul_kernel(x_ref, g_ref, w_ref, *rest, groups):
    *o_refs, xn_ref = rest
    j = pl.program_id(1)

    @pl.when(j == 0)
    def _():
        xn_ref[...] = _rms(x_ref[...], g_ref[...]).astype(BF16)

    y = jnp.dot(xn_ref[...], w_ref[...], preferred_element_type=F32)
    if groups == 1:
        for o_ref in o_refs:
            o_ref[...] = y.astype(o_ref.dtype)
    else:
        for g in range(groups):
            @pl.when(j == g)
            def _():
                for o_ref in o_refs[g::groups]:
                    o_ref[...] = y.astype(o_ref.dtype)


def _norm_matmul(x, gain, w, tn, split, dtypes):
    t, d = x.shape
    f = w.shape[1]
    tm = min(ROW_TILE, t)
    nj = f // tn
    if split:
        out_specs = [pl.BlockSpec((tm, tn), lambda i, j: (i, 0))] * (nj * len(dtypes))
        out_shape = [jax.ShapeDtypeStruct((t, tn), dt) for dt in dtypes for _ in range(nj)]
    else:
        out_specs = [pl.BlockSpec((tm, tn), lambda i, j: (i, j))] * len(dtypes)
        out_shape = [jax.ShapeDtypeStruct((t, f), dt) for dt in dtypes]
    outs = pl.pallas_call(
        functools.partial(_norm_matmul_kernel, groups=nj if split else 1),
        grid=(t // tm, nj),
        in_specs=[pl.BlockSpec((tm, d), lambda i, j: (i, 0)),
                  pl.BlockSpec((1, d), lambda i, j: (0, 0)),
                  pl.BlockSpec((d, tn), lambda i, j: (0, j))],
        out_specs=out_specs,
        out_shape=out_shape,
        scratch_shapes=[pltpu.VMEM((tm, d), BF16)],
        compiler_params=_params("parallel", "arbitrary"),
        name="norm_matmul",
    )(x, gain.reshape(1, d), w.astype(BF16))
    if split:
        return [outs[k * nj:(k + 1) * nj] for k in range(len(dtypes))]
    return outs


def _branch_proj_kernel(x_ref, g_ref, w_ref, *rest, keep_f32, scale):
    *o_refs, xn_ref, y_ref = rest
    chunks, tm, _ = y_ref.shape
    tn = chunks * LANES
    j = pl.program_id(1)

    @pl.when(j == 0)
    def _():
        xn_ref[...] = _rms(x_ref[...], g_ref[...]).astype(BF16)

    y = jnp.dot(xn_ref[...], w_ref[...], preferred_element_type=F32)
    if scale != 1.0:
        y = y * scale
    for b, dil in enumerate(BRANCH_DILATIONS):
        @pl.when(j == b)
        def _():
            if keep_f32:
                o_refs[N_BRANCH + b][...] = y
            if dil == 1:
                o_refs[b][...] = y.astype(BF16)
            else:
                for c in range(chunks):
                    y_ref[c] = y[:, c * LANES:(c + 1) * LANES]
                for r in range(dil):
                    for c in range(chunks):
                        lo = r * tn + c * LANES
                        o_refs[b][:, lo:lo + LANES] = y_ref[c, pl.ds(r, tm // dil, stride=dil), :].astype(BF16)


def _branch_proj(x, gain, w, keep_f32, scale=1.0):
    t, d = x.shape
    tn = w.shape[1] // N_BRANCH
    tm = min(ROW_TILE, t)
    row = lambda i, j: (i, 0)
    out_specs = [pl.BlockSpec((tm // dil, dil * tn), row) for dil in BRANCH_DILATIONS]
    out_shape = [jax.ShapeDtypeStruct((t // dil, dil * tn), BF16) for dil in BRANCH_DILATIONS]
    if keep_f32:
        out_specs += [pl.BlockSpec((tm, tn), row)] * N_BRANCH
        out_shape += [jax.ShapeDtypeStruct((t, tn), F32)] * N_BRANCH
    outs = pl.pallas_call(
        functools.partial(_branch_proj_kernel, keep_f32=keep_f32, scale=scale),
        grid=(t // tm, N_BRANCH),
        in_specs=[pl.BlockSpec((tm, d), row),
                  pl.BlockSpec((1, d), lambda i, j: (0, 0)),
                  pl.BlockSpec((d, tn), lambda i, j: (0, j))],
        out_specs=out_specs,
        out_shape=out_shape,
        scratch_shapes=[pltpu.VMEM((tm, d), BF16), pltpu.VMEM((tn // LANES, tm, LANES), F32)],
        compiler_params=_params("parallel", "arbitrary"),
        name="branch_proj",
    )(x, gain.reshape(1, d), w.astype(BF16))
    return outs[:N_BRANCH], outs[N_BRANCH:]


def _inproj_kernel(x_ref, g_ref, w_ref, wg_ref, o_ref, og_ref, xn_ref):
    @pl.when(pl.program_id(1) == 0)
    def _():
        xn = _rms(x_ref[...], g_ref[...]).astype(BF16)
        xn_ref[...] = xn
        og_ref[...] = jnp.dot(xn, wg_ref[...], preferred_element_type=F32)

    o_ref[...] = jnp.dot(xn_ref[...], w_ref[...], preferred_element_type=F32)


def _inproj(x, gain, w_in):
    t, d = x.shape
    tm = min(ROW_TILE, t)
    tn = 1024
    w_main = w_in[:, :QKVZ_W_A].astype(BF16)
    w_gate = jnp.pad(w_in[:, QKVZ_W_A:], ((0, 0), (0, LANES - 2 * HEADS_A))).astype(BF16)
    return pl.pallas_call(
        _inproj_kernel,
        grid=(t // tm, QKVZ_W_A // tn),
        in_specs=[pl.BlockSpec((tm, d), lambda i, j: (i, 0)),
                  pl.BlockSpec((1, d), lambda i, j: (0, 0)),
                  pl.BlockSpec((d, tn), lambda i, j: (0, j)),
                  pl.BlockSpec((d, LANES), lambda i, j: (0, 0))],
        out_specs=[pl.BlockSpec((tm, tn), lambda i, j: (i, j)),
                   pl.BlockSpec((tm, LANES), lambda i, j: (i, 0))],
        out_shape=[jax.ShapeDtypeStruct((t, QKVZ_W_A), F32),
                   jax.ShapeDtypeStruct((t, LANES), F32)],
        scratch_shapes=[pltpu.VMEM((tm, d), BF16)],
        compiler_params=_params("parallel", "arbitrary"),
        name="gdn_inproj",
    )(x, gain.reshape(1, d), w_main, w_gate)


def _inproj_conv_kernel(x_ref, g_ref, w_ref, wg_ref, cw_ref, c0_ref, o_ref, og_ref, tail_ref, xn_ref, halo_ref,
                        *, tiles_per_seq):
    i = pl.program_id(0)
    j = pl.program_id(1)
    halo = SUBLANES
    qkv_tiles = QKV_W_A // KEY_W_A

    @pl.when(j == 0)
    def _():
        xn = _rms(x_ref[...], g_ref[...]).astype(BF16)
        xn_ref[...] = xn
        og_ref[...] = jnp.dot(xn, wg_ref[...], preferred_element_type=F32)

    @pl.when(j < qkv_tiles)
    def _():
        y = jnp.dot(xn_ref[...], w_ref[...], preferred_element_type=F32)
        tm = y.shape[0]
        prev = jnp.where(i % tiles_per_seq == 0, c0_ref[...], halo_ref[j])
        ext = jnp.concatenate([prev, y], axis=0)
        conv = y * cw_ref[CONV_W - 1:CONV_W, :]
        for back in range(1, CONV_W):
            conv = conv + pltpu.roll(ext, back, axis=0)[halo:] * cw_ref[CONV_W - 1 - back:CONV_W - back, :]
        halo_ref[j] = y[tm - halo:]
        tail_ref[...] = y[tm - halo:]
        act = conv * _sigmoid(conv)
        col_head = lax.broadcasted_iota(jnp.int32, (y.shape[1], LANES), 0) // DK_A
        lane = lax.broadcasted_iota(jnp.int32, (y.shape[1], LANES), 1)
        sums = _dot(act * act, (col_head == lane).astype(BF16))
        unit = lax.rsqrt(sums + EPS) * jnp.where(j == 0, DK_A ** -0.5, 1.0)
        factor = jnp.where(j < 2, unit, 1.0)
        for h in range(HEADS_A):
            sl = slice(h * DK_A, (h + 1) * DK_A)
            o_ref[:, sl] = (act[:, sl] * factor[:, h:h + 1]).astype(o_ref.dtype)

    @pl.when(j == qkv_tiles)
    def _():
        y = jnp.dot(xn_ref[...], w_ref[...], preferred_element_type=F32)
        o_ref[...] = (y * _sigmoid(y)).astype(o_ref.dtype)


def _inproj_conv(x, gain, w_in, w_conv, conv0, n, seq_len):
    t, d = x.shape
    tm = min(ROW_TILE, seq_len)
    tn = KEY_W_A
    halo = SUBLANES
    tiles_per_seq = seq_len // tm
    last_qkv = QKV_W_A // tn - 1
    w_main = w_in[:, :QKVZ_W_A].astype(BF16)
    w_gate = jnp.pad(w_in[:, QKVZ_W_A:], ((0, 0), (0, LANES - 2 * HEADS_A))).astype(BF16)
    c0 = jnp.pad(conv0, ((0, 0), (halo - (CONV_W - 1), 0), (0, 0)))
    conv_col = lambda i, j: (0, jnp.minimum(j, last_qkv))
    seq_col = lambda i, j: (i // tiles_per_seq, 0, jnp.minimum(j, last_qkv))
    return pl.pallas_call(
        functools.partial(_inproj_conv_kernel, tiles_per_seq=tiles_per_seq),
        grid=(t // tm, QKVZ_W_A // tn),
        in_specs=[pl.BlockSpec((tm, d), lambda i, j: (i, 0)),
                  pl.BlockSpec((1, d), lambda i, j: (0, 0)),
                  pl.BlockSpec((d, tn), lambda i, j: (0, j)),
                  pl.BlockSpec((d, LANES), lambda i, j: (0, 0)),
                  pl.BlockSpec((CONV_W, tn), conv_col),
                  pl.BlockSpec((None, halo, tn), seq_col)],
        out_specs=[pl.BlockSpec((tm, tn), lambda i, j: (i, j)),
                   pl.BlockSpec((tm, LANES), lambda i, j: (i, 0)),
                   pl.BlockSpec((None, halo, tn), lambda i, j: (i, 0, jnp.minimum(j, last_qkv)))],
        out_shape=[jax.ShapeDtypeStruct((t, QKVZ_W_A), BF16),
                   jax.ShapeDtypeStruct((t, LANES), F32),
                   jax.ShapeDtypeStruct((t // tm, halo, QKV_W_A), F32)],
        scratch_shapes=[pltpu.VMEM((tm, d), BF16), pltpu.VMEM((QKV_W_A // tn, halo, tn), F32)],
        compiler_params=_params("arbitrary", "arbitrary"),
        name="gdn_inproj_conv",
    )(x, gain.reshape(1, d), w_main, w_gate, w_conv, c0)


def _mlp_kernel(*refs, n_mix, final_norm):
    x_ref = refs[0]
    mix_refs = refs[1:1 + n_mix]
    wo_ref, g_ref, wup_ref, wdn_ref, gf_ref, y_ref, xn_ref = refs[1 + n_mix:1 + n_mix + 7]
    nat_refs = refs[1 + n_mix + 7:]
    f = pl.program_id(1)

    @pl.when(f == 0)
    def _():
        if n_mix == 1:
            mixed = mix_refs[0][...]
        else:
            tm = x_ref.shape[0]
            chunks = Q_W_BRANCH // LANES
            for b, dil in enumerate(BRANCH_DILATIONS):
                o_blk, lse_blk = mix_refs[b], mix_refs[N_BRANCH + b]
                o_nat, lse_nat = nat_refs[b], nat_refs[N_BRANCH + b]
                rows = tm // dil
                for r in range(dil):
                    dst = pl.ds(r, rows, stride=dil) if dil > 1 else pl.ds(0, rows)
                    for c in range(chunks):
                        lo = r * Q_W_BRANCH + c * LANES
                        o_nat[c, dst, :] = o_blk[:, lo:lo + LANES].astype(F32)
                    lse_nat[dst, :] = lse_blk[r]
            lses = [r[...] for r in nat_refs[N_BRANCH:]]
            m = functools.reduce(jnp.maximum, lses)
            ws = [jnp.exp(l - m) for l in lses]
            total = sum(ws)
            low = lax.broadcasted_iota(jnp.int32, (1, 2 * HEAD_DIM_B), 1) < HEAD_DIM_B
            mixed = 0.0
            for w, o_nat in zip(ws, nat_refs[:N_BRANCH]):
                w = w / total
                mixed = mixed + jnp.concatenate(
                    [jnp.where(low, w[:, 2 * p:2 * p + 1], w[:, 2 * p + 1:2 * p + 2]) * o_nat[p]
                     for p in range(HEADS_B // 2)], axis=1)
        x1 = x_ref[...] + _dot(mixed, wo_ref[...])
        y_ref[...] = x1
        xn_ref[...] = _rms(x1, g_ref[...]).astype(BF16)

    u = jnp.maximum(jnp.dot(xn_ref[...], wup_ref[...], preferred_element_type=F32), 0.0)
    y_ref[...] += _dot(u * u, wdn_ref[...])

    if final_norm:
        @pl.when(f == pl.num_programs(1) - 1)
        def _():
            y_ref[...] = _rms(y_ref[...], gf_ref[...])


def _mlp_block(x, mix, w_o, gain, w_up, w_down, layer, gain_final, final_norm):
    t, d = x.shape
    k = w_o.shape[0]
    dff = w_up.shape[2]
    tm = min(ROW_TILE, t)
    tf = MLP_FF_TILE
    n_mix = len(mix)
    row = lambda i, f: (i, 0)
    const = lambda i, f: (0, 0)
    scratch = [pltpu.VMEM((tm, d), BF16)]
    if n_mix == 1:
        mix_specs = [pl.BlockSpec((tm, k), row)]
    else:
        mix_specs = [pl.BlockSpec((tm // dil, dil * Q_W_BRANCH), row) for dil in BRANCH_DILATIONS]
        for dil, lse in zip(BRANCH_DILATIONS, mix[N_BRANCH:]):
            per_seq = lse.shape[2] // (tm // dil)
            mix_specs.append(pl.BlockSpec((None, dil, tm // dil, LANES),
                                          lambda i, f, per_seq=per_seq: (i // per_seq, 0, i % per_seq, 0)))
        scratch += ([pltpu.VMEM((Q_W_BRANCH // LANES, tm, LANES), F32)] * N_BRANCH
                    + [pltpu.VMEM((tm, LANES), F32)] * N_BRANCH)
    return pl.pallas_call(
        functools.partial(_mlp_kernel, n_mix=n_mix, final_norm=final_norm),
        grid=(t // tm, dff // tf),
        in_specs=([pl.BlockSpec((tm, d), row)] + mix_specs
                  + [pl.BlockSpec((k, d), const),
                     pl.BlockSpec((1, d), const),
                     pl.BlockSpec((None, d, tf), lambda i, f: (layer, 0, f)),
                     pl.BlockSpec((None, tf, d), lambda i, f: (layer, f, 0)),
                     pl.BlockSpec((1, d), const)]),
        out_specs=pl.BlockSpec((tm, d), row),
        out_shape=jax.ShapeDtypeStruct((t, d), F32),
        scratch_shapes=scratch,
        compiler_params=_params("parallel", "arbitrary"),
        name="mix_out_mlp",
    )(x, *mix, w_o.astype(BF16), gain.reshape(1, d), w_up.astype(BF16), w_down.astype(BF16),
      gain_final.reshape(1, d))


def _inverse_masks(c):
    row = lax.broadcasted_iota(jnp.int32, (c, c), 0)
    col = lax.broadcasted_iota(jnp.int32, (c, c), 1)
    masks = []
    shift = 0
    while (1 << shift) < c:
        same_pair = (row >> (shift + 1)) == (col >> (shift + 1))
        other_half = (row >> shift) != (col >> shift)
        masks.append((row > col) & same_pair & other_half)
        shift += 1
    return masks


def _unit_lower_inverses(mats, c):
    masks = _inverse_masks(c)
    row = lax.broadcasted_iota(jnp.int32, (c, c), 0)
    col = lax.broadcasted_iota(jnp.int32, (c, c), 1)
    eye = (row == col).astype(F32)
    invs = [eye - jnp.where(masks[0], a, 0.0) for a in mats]
    for mask in masks[1:]:
        xs = [_dot(inv, jnp.where(mask, a, 0.0)) for inv, a in zip(invs, mats)]
        invs = [inv - _dot(x, inv) for inv, x in zip(invs, xs)]
    return invs


def _gdn_kernel(q_ref, k_ref, v_ref, z_ref, ba_ref, bat_ref, s0_ref, alr_ref, dtr_ref, alc_ref, dtc_ref, gn_ref,
                o_ref, s_ref, *, chunk, chunks_per_step):
    c = chunk

    @pl.when(pl.program_id(1) == 0)
    def _():
        s_ref[...] = s0_ref[...]

    row = lax.broadcasted_iota(jnp.int32, (c, c), 0)
    col = lax.broadcasted_iota(jnp.int32, (c, c), 1)
    causal = row >= col
    strict = row > col
    lower_ones = causal.astype(F32)
    upper_ones = (row <= col).astype(F32)
    heads = range(HEADS_A)
    pairs = [(ck, h) for ck in range(chunks_per_step) for h in heads]
    rows_of = [slice(ck * c, (ck + 1) * c) for ck in range(chunks_per_step)]
    cols_of = [slice(h * DK_A, (h + 1) * DK_A) for h in heads]

    beta_cols, cum_cols, cum_rows = [], [], []
    for ck in range(chunks_per_step):
        ba = ba_ref[rows_of[ck], :]
        bat = bat_ref[ck]
        beta_cols.append(_sigmoid(ba))
        g_cols = -jnp.exp(alr_ref[...]) * _softplus(ba + dtr_ref[...])
        g_rows = -jnp.exp(alc_ref[...]) * _softplus(bat + dtc_ref[...])
        cum_cols.append(_dot_f32(lower_ones, g_cols))
        cum_rows.append(_dot_f32(g_rows, upper_ones))

    qs = [q_ref[rows_of[ck], cols_of[h]] for ck, h in pairs]
    ks = [k_ref[rows_of[ck], cols_of[h]] for ck, h in pairs]
    kfs = [k.astype(F32) for k in ks]
    betas = [beta_cols[ck][:, h:h + 1] for ck, h in pairs]
    gcs = [cum_cols[ck][:, HEADS_A + h:HEADS_A + h + 1] for ck, h in pairs]
    grs = [cum_rows[ck][HEADS_A + h:HEADS_A + h + 1, :] for ck, h in pairs]
    decays = [jnp.where(causal, jnp.exp(jnp.where(causal, gc - gr, 0.0)), 0.0) for gc, gr in zip(gcs, grs)]
    kbs = [k * b for k, b in zip(kfs, betas)]
    kq = [_dot_nt(jnp.concatenate([kb.astype(BF16), q], axis=0), k) for kb, q, k in zip(kbs, qs, ks)]
    mats = [jnp.where(strict, x[:c] * d, 0.0) for x, d in zip(kq, decays)]
    attns = [x[c:] * d for x, d in zip(kq, decays)]
    t_invs = _unit_lower_inverses(mats, c)
    e_gcs = [jnp.exp(gc) for gc in gcs]
    uws = [_dot(t, jnp.concatenate([v_ref[rows_of[ck], cols_of[h]].astype(F32) * b, kb * e], axis=1))
           for t, (ck, h), b, kb, e in zip(t_invs, pairs, betas, kbs, e_gcs)]
    q_decs = [q.astype(F32) * e for q, e in zip(qs, e_gcs)]

    states = [s_ref[h] for h in heads]
    for ck in range(chunks_per_step):
        at = slice(ck * HEADS_A, (ck + 1) * HEADS_A)
        wq_s = [_dot(jnp.concatenate([uw[:, DV_A:], qd], axis=0), s)
                for uw, qd, s in zip(uws[at], q_decs[at], states)]
        v_news = [uw[:, :DV_A] - x[:c] for uw, x in zip(uws[at], wq_s)]
        outs = [x[c:] + _dot(attn, vn) for x, attn, vn in zip(wq_s, attns[at], v_news)]
        g_lasts = [gc[c - 1:c, :] for gc in gcs[at]]
        states = [s * jnp.exp(gl) + _dot_tn(kf * jnp.exp(gl - gc), vn)
                  for s, gl, kf, gc, vn in zip(states, g_lasts, kfs[at], gcs[at], v_news)]
        for h in heads:
            gate = z_ref[rows_of[ck], cols_of[h]].astype(F32)
            o_ref[rows_of[ck], cols_of[h]] = (_rms(outs[h], gn_ref[...]) * gate).astype(o_ref.dtype)
    for h in heads:
        s_ref[h] = states[h]


def _gdn(act, ba, n, seq_len, chunk, s0, a_log, dt_bias, g_norm):
    nc = seq_len // chunk
    per_step = GDN_CHUNKS_PER_STEP
    steps = nc // per_step
    rows = per_step * chunk
    t = n * seq_len
    bat = ba[:, :2 * HEADS_A].reshape(n * nc, chunk, 2 * HEADS_A).transpose(0, 2, 1)
    pad_row = lambda p: jnp.pad(p.reshape(1, HEADS_A), ((0, 0), (HEADS_A, LANES - 2 * HEADS_A)))
    pad_col = lambda p: jnp.pad(p.reshape(HEADS_A, 1), ((HEADS_A, 0), (0, 0)))
    blk = lambda i, j: (i * steps + j, 0)
    const = lambda i, j: (0, 0)
    col_blk = lambda col: pl.BlockSpec((rows, KEY_W_A), lambda i, j: (i * steps + j, col))
    return pl.pallas_call(
        functools.partial(_gdn_kernel, chunk=chunk, chunks_per_step=per_step),
        grid=(n, steps),
        in_specs=[col_blk(0), col_blk(1), col_blk(2), col_blk(3),
                  pl.BlockSpec((rows, LANES), blk),
                  pl.BlockSpec((per_step, 2 * HEADS_A, chunk), lambda i, j: (i * steps + j, 0, 0)),
                  pl.BlockSpec((None, HEADS_A, DK_A, DV_A), lambda i, j: (i, 0, 0, 0)),
                  pl.BlockSpec((1, LANES), const),
                  pl.BlockSpec((1, LANES), const),
                  pl.BlockSpec((2 * HEADS_A, 1), const),
                  pl.BlockSpec((2 * HEADS_A, 1), const),
                  pl.BlockSpec((1, DV_A), const)],
        out_specs=[pl.BlockSpec((rows, VAL_W_A), blk),
                   pl.BlockSpec((None, HEADS_A, DK_A, DV_A), lambda i, j: (i, 0, 0, 0))],
        out_shape=[jax.ShapeDtypeStruct((t, VAL_W_A), BF16),
                   jax.ShapeDtypeStruct((n, HEADS_A, DK_A, DV_A), F32)],
        compiler_params=_params("parallel", "arbitrary"),
        name="gdn_delta_rule",
    )(act, act, act, act, ba, bat, s0, pad_row(a_log), pad_row(dt_bias), pad_col(a_log), pad_col(dt_bias),
      g_norm.reshape(1, DV_A))


def _gdn_step_kernel(qkv_ref, z_ref, ba_ref, st_ref, cw_ref, s0_ref, alr_ref, dtr_ref, gn_ref, o_ref, s_ref):
    nb = qkv_ref.shape[0]
    conv = qkv_ref[...] * cw_ref[CONV_W - 1:CONV_W, :]
    for j in range(CONV_W - 1):
        conv = conv + st_ref[j] * cw_ref[j:j + 1, :]
    act = conv * _sigmoid(conv)
    ba = ba_ref[...]
    beta_cols = _sigmoid(ba)
    decay_cols = jnp.exp(-jnp.exp(alr_ref[...]) * _softplus(ba + dtr_ref[...]))
    fill = jnp.zeros((DK_A - nb, DK_A), F32)
    for h in range(HEADS_A):
        lo = h * DK_A
        q = act[:, lo:lo + DK_A]
        k = act[:, KEY_W_A + lo:KEY_W_A + lo + DK_A]
        v = act[:, 2 * KEY_W_A + lo:2 * KEY_W_A + lo + DV_A]
        q = q * lax.rsqrt(jnp.sum(q * q, axis=-1, keepdims=True) + EPS) * (DK_A ** -0.5)
        k = k * lax.rsqrt(jnp.sum(k * k, axis=-1, keepdims=True) + EPS)
        q_t = jnp.concatenate([q, fill], axis=0).T
        k_t = jnp.concatenate([k, fill], axis=0).T
        z = z_ref[:, lo:lo + DV_A]
        gate = z * _sigmoid(z)
        seqs = range(nb)
        k_cols = [k_t[:, i:i + 1] for i in seqs]
        decays = [decay_cols[i:i + 1, HEADS_A + h:HEADS_A + h + 1] for i in seqs]
        k_s = [jnp.sum(k_cols[i] * s0_ref[i, h], axis=0, keepdims=True) for i in seqs]
        v_new = [beta_cols[i:i + 1, h:h + 1] * (v[i:i + 1, :] - decays[i] * k_s[i]) for i in seqs]
        s_new = [decays[i] * s0_ref[i, h] + k_cols[i] * v_new[i] for i in seqs]
        for i in seqs:
            s_ref[i, h] = s_new[i]
        outs = [jnp.sum(q_t[:, i:i + 1] * s_new[i], axis=0, keepdims=True) for i in seqs]
        for i in seqs:
            o_ref[i:i + 1, lo:lo + DV_A] = _rms(outs[i], gn_ref[...]) * gate[i:i + 1, :]


def _gdn_step(qkvz, ba, s0_layers, conv0_layers, layer, w_conv, a_log, dt_bias, g_norm):
    n = qkvz.shape[0]
    nb = SUBLANES
    pad_row = lambda p: jnp.pad(p.reshape(1, HEADS_A), ((0, 0), (HEADS_A, LANES - 2 * HEADS_A)))
    blk = lambda i: (i, 0)
    const = lambda i: (0, 0)
    return pl.pallas_call(
        _gdn_step_kernel,
        grid=(n // nb,),
        in_specs=[pl.BlockSpec((nb, QKV_W_A), blk),
                  pl.BlockSpec((nb, VAL_W_A), lambda i: (i, QKV_W_A // VAL_W_A)),
                  pl.BlockSpec((nb, LANES), blk),
                  pl.BlockSpec((None, CONV_W - 1, nb, QKV_W_A), lambda i: (layer, 0, i, 0)),
                  pl.BlockSpec((CONV_W, QKV_W_A), const),
                  pl.BlockSpec((None, nb, HEADS_A, DK_A, DV_A), lambda i: (layer, i, 0, 0, 0)),
                  pl.BlockSpec((1, LANES), const),
                  pl.BlockSpec((1, LANES), const),
                  pl.BlockSpec((1, DV_A), const)],
        out_specs=[pl.BlockSpec((nb, VAL_W_A), blk),
                   pl.BlockSpec((nb, HEADS_A, DK_A, DV_A), lambda i: (i, 0, 0, 0))],
        out_shape=[jax.ShapeDtypeStruct((n, VAL_W_A), F32),
                   jax.ShapeDtypeStruct((n, HEADS_A, DK_A, DV_A), F32)],
        compiler_params=_params("parallel"),
        name="gdn_step",
    )(qkvz, qkvz, ba, conv0_layers.transpose(0, 2, 1, 3), w_conv, s0_layers, pad_row(a_log), pad_row(dt_bias),
      g_norm.reshape(1, DV_A))


def _attn_prompt_kernel(q_ref, kvc_ref, kvp_ref, o_ref, lse_ref, *, dilation, sub_blocks):
    blk = ATT_BLOCK
    hd = HEAD_DIM_B
    a = pl.program_id(2)
    qi = lax.broadcasted_iota(jnp.int32, (blk, 2 * blk), 0)
    kj = lax.broadcasted_iota(jnp.int32, (blk, 2 * blk), 1)
    steps = blk + qi - kj
    in_window = (steps >= 0) & (steps <= blk)
    started = in_window & ((kj >= blk) | (a > 0))
    dist = (steps * dilation).astype(F32)
    bias_first = [jnp.where(started, -slope * dist, -jnp.inf) for slope in ALIBI_SLOPES]
    bias_later = [jnp.where(in_window, -slope * dist, -jnp.inf) for slope in ALIBI_SLOPES]
    low = lax.broadcasted_iota(jnp.int32, (1, 2 * hd), 1) < hd
    kv_all = jnp.concatenate([kvp_ref[...], kvc_ref[...]], axis=0)
    padded = []
    for kvh in range(KV_HEADS_B):
        own = low if kvh == 0 else ~low
        both = []
        for x in (kv_all[:, :2 * hd], kv_all[:, 2 * hd:]):
            kept = jnp.where(own, x, jnp.zeros_like(x))
            moved = pltpu.roll(kept, hd, axis=1)
            both += [kept, moved] if kvh == 0 else [moved, kept]
        padded.append(both)
    head_lane = lax.broadcasted_iota(jnp.int32, (1, LANES), 1)
    for j in range(sub_blocks):
        lse_all = jnp.zeros((blk, LANES), F32)
        bias = bias_first if j == 0 else bias_later
        rows = slice(j * blk, (j + 1) * blk)
        keys = slice(j * blk, (j + 2) * blk)
        for kvh in range(KV_HEADS_B):
            k_lo, k_hi, v_lo, v_hi = (x[keys] for x in padded[kvh])
            heads = [kvh * GQA_B + g for g in range(GQA_B)]
            qs = [q_ref[rows, (h // 2) * 2 * hd:(h // 2 + 1) * 2 * hd] for h in heads]
            ss = [_dot_nt(q, k_hi if h % 2 else k_lo) + bias[h] for q, h in zip(qs, heads)]
            ms = [jnp.max(s, axis=-1, keepdims=True) for s in ss]
            ps = [jnp.exp(s - m) for s, m in zip(ss, ms)]
            ls = [jnp.sum(p, axis=-1, keepdims=True) for p in ps]
            for g in range(0, GQA_B, 2):
                pair = (heads[g] // 2) * 2 * hd
                o = _dot(ps[g], v_lo) + _dot(ps[g + 1], v_hi)
                o_ref[rows, pair:pair + 2 * hd] = (o / jnp.where(low, ls[g], ls[g + 1])).astype(o_ref.dtype)
            for g, h in enumerate(heads):
                lse_all = jnp.where(head_lane == h, ms[g] + jnp.log(ls[g]), lse_all)
        lse_ref[rows, :] = lse_all


def _attn_prompt_branch(q, kv, n, seq_len, dilation):
    ls = seq_len // dilation
    sub_blocks = min(4, ls // ATT_BLOCK)
    qb = sub_blocks * ATT_BLOCK
    qv = q.reshape(n, ls, dilation * Q_W_BRANCH)
    kvv = kv.reshape(n, ls, dilation * KV_W_BRANCH)
    cur = lambda i, r, a: (i, a, r)
    prev = lambda i, r, a: (i, jnp.maximum(a * sub_blocks - 1, 0), r)
    o, lse = pl.pallas_call(
        functools.partial(_attn_prompt_kernel, dilation=dilation, sub_blocks=sub_blocks),
        grid=(n, dilation, ls // qb),
        in_specs=[pl.BlockSpec((None, qb, Q_W_BRANCH), cur),
                  pl.BlockSpec((None, qb, KV_W_BRANCH), cur),
                  pl.BlockSpec((None, ATT_BLOCK, KV_W_BRANCH), prev)],
        out_specs=[pl.BlockSpec((None, qb, Q_W_BRANCH), cur),
                   pl.BlockSpec((None, None, qb, LANES), lambda i, r, a: (i, r, a, 0))],
        out_shape=[jax.ShapeDtypeStruct(qv.shape, BF16),
                   jax.ShapeDtypeStruct((n, dilation, ls, LANES), F32)],
        compiler_params=_params("parallel", "parallel", "arbitrary"),
        name=f"attn_prompt_d{dilation}",
    )(qv, kvv, kvv)
    return o.reshape(n * ls, dilation * Q_W_BRANCH), lse


def _cache_update_kernel(c_ref, kvn_ref, o_ref, g_ref, *, window, dilation):
    nb = c_ref.shape[0]
    nk = ATT_BLOCK
    last = lax.broadcasted_iota(jnp.int32, (1, window), 1) == window - 1
    new_t = jnp.concatenate([kvn_ref[...], jnp.zeros((LANES - nb, KV_W_BRANCH), F32)], axis=0).T
    if dilation > 1:
        src = lax.broadcasted_iota(jnp.int32, (window, nk), 0)
        dst = lax.broadcasted_iota(jnp.int32, (window, nk), 1)
        pick = (src == dst * dilation).astype(BF16)
    for i in range(nb):
        x = c_ref[i].reshape(KV_W_BRANCH, window)
        shifted = jnp.where(last, new_t[:, i:i + 1], pltpu.roll(x, window - 1, axis=1))
        o_ref[i] = shifted.reshape(o_ref.shape[1:])
        seen = x.astype(BF16) if dilation == 1 else _dot(x, pick).astype(BF16)
        g_ref[i] = seen.reshape(g_ref.shape[1:])


def _cache_update(cache_t, kv_new, window, dilation):
    n = cache_t.shape[0]
    nb = max(1, min(SUBLANES, 4096 // window))
    blk = (nb, 2, KV_HEADS_B, HEAD_DIM_B, window)
    seen_blk = (nb, 2, KV_HEADS_B * HEAD_DIM_B, ATT_BLOCK)
    return pl.pallas_call(
        functools.partial(_cache_update_kernel, window=window, dilation=dilation),
        grid=(n // nb,),
        in_specs=[pl.BlockSpec(blk, lambda i: (i, 0, 0, 0, 0)),
                  pl.BlockSpec((None, nb, KV_W_BRANCH), lambda i: (i, 0, 0))],
        out_specs=[pl.BlockSpec(blk, lambda i: (i, 0, 0, 0, 0)),
                   pl.BlockSpec(seen_blk, lambda i: (i, 0, 0, 0))],
        out_shape=[jax.ShapeDtypeStruct(cache_t.shape, F32),
                   jax.ShapeDtypeStruct((n,) + seen_blk[1:], BF16)],
        compiler_params=_params("parallel"),
        name=f"cache_update_w{window}",
    )(cache_t, kv_new.reshape(n // nb, nb, KV_W_BRANCH))


def _attn_sample_kernel(q_ref, kvn_ref, g0_ref, g1_ref, g2_ref, o_ref):
    nk = ATT_BLOCK
    nb = q_ref.shape[0]
    kv_w = KV_HEADS_B * HEAD_DIM_B
    head = lax.broadcasted_iota(jnp.int32, (HEADS_B, 1), 0)
    slopes = jnp.exp2(-8.0 * (head + 1).astype(F32) / HEADS_B)
    key = lax.broadcasted_iota(jnp.int32, (1, nk), 1)
    lane = lax.broadcasted_iota(jnp.int32, (HEADS_B, kv_w), 1)
    own_half = (lane // HEAD_DIM_B) == (head // GQA_B)
    scale = HEAD_DIM_B ** -0.5
    g_refs = (g0_ref, g1_ref, g2_ref)
    pairs = [(i, b) for i in range(nb) for b in range(N_BRANCH)]
    dists = [((nk - key) * d).astype(F32) for d in BRANCH_DILATIONS]
    qs = [q_ref[i, b * HEADS_B:(b + 1) * HEADS_B, :] * scale for i, b in pairs]
    s_old = [_dot(q, g_refs[b][i, 0]) - slopes * dists[b] for q, (i, b) in zip(qs, pairs)]
    s_new = [jnp.sum(q * kvn_ref[b, i:i + 1, :kv_w], axis=-1, keepdims=True) for q, (i, b) in zip(qs, pairs)]
    ms = [jnp.maximum(jnp.max(so, axis=-1, keepdims=True), sn) for so, sn in zip(s_old, s_new)]
    p_old = [jnp.exp(so - m) for so, m in zip(s_old, ms)]
    p_new = [jnp.exp(sn - m) for sn, m in zip(s_new, ms)]
    ls = [jnp.sum(po, axis=-1, keepdims=True) + pn for po, pn in zip(p_old, p_new)]
    outs = [(_dot_nt(po, g_refs[b][i, 1]) + pn * kvn_ref[b, i:i + 1, kv_w:]) / l
            for po, pn, l, (i, b) in zip(p_old, p_new, ls, pairs)]
    lses = [m + jnp.log(l) for m, l in zip(ms, ls)]
    for i in range(nb):
        sl = slice(i * N_BRANCH, (i + 1) * N_BRANCH)
        m = functools.reduce(jnp.maximum, lses[sl])
        ws = [jnp.exp(l - m) for l in lses[sl]]
        merged = sum(w * o for w, o in zip(ws, outs[sl])) / sum(ws)
        o_ref[i] = jnp.where(own_half, merged, 0.0)


def _spread_heads(w, axis):
    w = jnp.moveaxis(w, axis, -1)
    lead = w.shape[:-1]
    w = w.reshape(lead + (-1, KV_HEADS_B, GQA_B, HEAD_DIM_B))
    halves = [jnp.pad(w[..., g, :, :], [(0, 0)] * (len(lead) + 2) + [(g * HEAD_DIM_B, (KV_HEADS_B - 1 - g) * HEAD_DIM_B)])
              for g in range(KV_HEADS_B)]
    out = jnp.stack(halves, axis=-3)
    return jnp.moveaxis(out.reshape(lead + (-1,)), -1, axis)


def _attn_sample(q, kv_new, seen):
    n = q.shape[0]
    nb = SUBLANES
    rows = N_BRANCH * HEADS_B
    seen_blk = (nb, 2, KV_HEADS_B * HEAD_DIM_B, ATT_BLOCK)
    out = pl.pallas_call(
        _attn_sample_kernel,
        grid=(n // nb,),
        in_specs=[pl.BlockSpec((nb, rows, LANES), lambda i: (i, 0, 0)),
                  pl.BlockSpec((N_BRANCH, nb, KV_W_BRANCH), lambda i: (0, i, 0))]
                 + [pl.BlockSpec(seen_blk, lambda i: (i, 0, 0, 0))] * N_BRANCH,
        out_specs=pl.BlockSpec((nb, HEADS_B, LANES), lambda i: (i, 0, 0)),
        out_shape=jax.ShapeDtypeStruct((n, HEADS_B, LANES), F32),
        compiler_params=_params("parallel"),
        name="attn_sample",
    )(q.reshape(n, rows, LANES), kv_new, *seen)
    return out.reshape(n, HEADS_B * LANES)


def _trunk(x, rec0, conv0, kv_bufs, w):
    n, seq_len, d = x.shape
    fresh = kv_bufs is None
    assert seq_len == 1 or (fresh and seq_len % (ATT_BLOCK * max(BRANCH_DILATIONS)) == 0)
    x = x.reshape(n * seq_len, d)
    rec_new, conv_new = [], []
    for l in range(N_LAYERS_A):
        if fresh:
            act, ba, tail = _inproj_conv(x, w['norm_mix'][l], w['w_in_a'][l], w['conv_a'][l], conv0[l], n, seq_len)
            last_tile = tail.reshape(n, -1, SUBLANES, QKV_W_A)[:, -1]
            conv_new.append(last_tile[:, SUBLANES - (CONV_W - 1):])
            o, s_new = _gdn(act, ba, n, seq_len, GDN_CHUNK, rec0[l], w['a_log'][l], w['dt_bias'][l],
                            w['norm_o_a'][l])
        else:
            qkvz, ba = _inproj(x, w['norm_mix'][l], w['w_in_a'][l])
            conv_new.append(jnp.concatenate([conv0[l][:, 1:], qkvz[:, None, :QKV_W_A]], axis=1))
            o, s_new = _gdn_step(qkvz, ba, rec0, conv0, l, w['conv_a'][l], w['a_log'][l], w['dt_bias'][l],
                                 w['norm_o_a'][l])
        rec_new.append(s_new)
        x = _mlp_block(x, [o], w['w_out_a'][l], w['norm_mlp'][l], w['w_up'], w['w_down'], l,
                       w['norm_final'], False)

    if fresh:
        kv_lo, kv = _branch_proj(x, w['norm_kv'], w['w_kv'], True)
    else:
        kv, = _norm_matmul(x, w['norm_kv'], w['w_kv'], KV_W_BRANCH, True, (F32,))
    bufs_new, seen = [], []
    for b in range(N_BRANCH):
        if fresh:
            keep = min(BRANCH_WINDOWS[b], seq_len)
            last = kv[b].reshape(n, seq_len, KV_W_BRANCH)[:, seq_len - keep:]
            bufs_new.append(last.reshape(n, keep, 2, KV_HEADS_B, HEAD_DIM_B))
        else:
            assert kv_bufs[b].shape[1] == BRANCH_WINDOWS[b]
            shifted, seen_b = _cache_update(jnp.transpose(kv_bufs[b], (0, 2, 3, 4, 1)), kv[b],
                                            BRANCH_WINDOWS[b], BRANCH_DILATIONS[b])
            bufs_new.append(jnp.transpose(shifted, (0, 4, 1, 2, 3)))
            seen.append(seen_b)

    for l in range(N_LAYERS_A, N_LAYERS_A + N_LAYERS_B):
        lb = l - N_LAYERS_A
        if fresh:
            q, _ = _branch_proj(x, w['norm_mix'][l], w['w_q_b'][lb], False, HEAD_DIM_B ** -0.5)
            parts = [_attn_prompt_branch(q[b], kv_lo[b], n, seq_len, BRANCH_DILATIONS[b]) for b in range(N_BRANCH)]
            mix = [p[0] for p in parts] + [p[1] for p in parts]
            w_o = w['w_o_b'][lb]
        else:
            q, = _norm_matmul(x, w['norm_mix'][l], _spread_heads(w['w_q_b'][lb], 1), 1024, False, (F32,))
            mix = [_attn_sample(q, jnp.stack(kv), seen)]
            w_o = _spread_heads(w['w_o_b'][lb], 0)
        x = _mlp_block(x, mix, w_o, w['norm_mlp'][l], w['w_up'], w['w_down'], l,
                       w['norm_final'], l == N_LAYERS_A + N_LAYERS_B - 1)
    return (x.reshape(n, seq_len, d), jnp.stack(rec_new), jnp.stack(conv_new), *bufs_new)


def kernel(x_prompt, x_sample, state_a_rec, state_a_conv, cache_b0_kv, cache_b1_kv, cache_b2_kv, norm_mix, norm_mlp, w_in_a, conv_a, a_log, dt_bias, norm_o_a, w_out_a, norm_kv, w_kv, w_q_b, w_o_b, w_up, w_down, norm_final):
    w = {'norm_mix': norm_mix, 'norm_mlp': norm_mlp, 'w_in_a': w_in_a, 'conv_a': conv_a, 'a_log': a_log,
         'dt_bias': dt_bias, 'norm_o_a': norm_o_a, 'w_out_a': w_out_a, 'norm_kv': norm_kv, 'w_kv': w_kv,
         'w_q_b': w_q_b, 'w_o_b': w_o_b, 'w_up': w_up, 'w_down': w_down, 'norm_final': norm_final}
    n_p = x_prompt.shape[0]
    p_rec0 = jnp.zeros((N_LAYERS_A, n_p, HEADS_A, DK_A, DV_A), state_a_rec.dtype)
    p_conv0 = jnp.zeros((N_LAYERS_A, n_p, CONV_W - 1, QKV_W_A), state_a_conv.dtype)
    prompt = _trunk(x_prompt, p_rec0, p_conv0, None, w)
    sample = _trunk(x_sample, state_a_rec, state_a_conv, [cache_b0_kv, cache_b1_kv, cache_b2_kv], w)
    return (prompt[0], sample[0], *prompt[1:], *sample[1:])
```

```python
import functools

import jax
import jax.numpy as jnp
from jax import lax
from jax.experimental import pallas as pl
from jax.experimental.pallas import tpu as pltpu

D_MODEL = 1024
HEADS_A = 8
DK_A = 128
DV_A = 128
KEY_W_A = HEADS_A * DK_A
VAL_W_A = HEADS_A * DV_A
QKV_W_A = 2 * KEY_W_A + VAL_W_A
QKVZ_W_A = QKV_W_A + VAL_W_A
CONV_W = 4
N_LAYERS_A = 2
N_LAYERS_B = 2
BRANCH_WINDOWS = (128, 512, 2048)
BRANCH_DILATIONS = (1, 4, 16)
N_BRANCH = 3
HEADS_B = 8
KV_HEADS_B = 2
GQA_B = HEADS_B // KV_HEADS_B
HEAD_DIM_B = 64
Q_W_BRANCH = HEADS_B * HEAD_DIM_B
KV_W_BRANCH = 2 * KV_HEADS_B * HEAD_DIM_B
ALIBI_SLOPES = tuple(2.0 ** (-8.0 * h / HEADS_B) for h in range(1, HEADS_B + 1))
EPS = 1e-6

LANES = 128
SUBLANES = 8
VMEM_LIMIT_BYTES = 56 * 1024 * 1024
ROW_TILE = 1024
MLP_FF_TILE = 512
GDN_CHUNK = 128
GDN_CHUNKS_PER_STEP = 2
ATT_BLOCK = 128

F32 = jnp.float32
BF16 = jnp.bfloat16


def _params(*sem):
    return pltpu.CompilerParams(dimension_semantics=sem, vmem_limit_bytes=VMEM_LIMIT_BYTES)


def _rms(x, gain):
    return x * lax.rsqrt(jnp.mean(x * x, axis=-1, keepdims=True) + EPS) * gain


def _sigmoid(x):
    return 1.0 / (1.0 + jnp.exp(-x))


def _softplus(x):
    return jnp.maximum(x, 0.0) + jnp.log(1.0 + jnp.exp(-jnp.abs(x)))


def _dot(a, b):
    return jnp.dot(a.astype(BF16), b.astype(BF16), preferred_element_type=F32)


def _dot_nt(a, b):
    return lax.dot_general(a.astype(BF16), b.astype(BF16), (((1,), (1,)), ((), ())),
                           preferred_element_type=F32)


def _dot_tn(a, b):
    return lax.dot_general(a.astype(BF16), b.astype(BF16), (((0,), (0,)), ((), ())),
                           preferred_element_type=F32)


def _dot_f32(a, b):
    return jnp.dot(a, b, precision=lax.Precision.HIGHEST, preferred_element_type=F32)


def _norm_matmul_kernel(x_ref, g_ref, w_ref, *rest, groups):
    *o_refs, xn_ref = rest
    j = pl.program_id(1)

    @pl.when(j == 0)
    def _():
        xn_ref[...] = _rms(x_ref[...], g_ref[...]).astype(BF16)

    y = jnp.dot(xn_ref[...], w_ref[...], preferred_element_type=F32)
    if groups == 1:
        for o_ref in o_refs:
            o_ref[...] = y.astype(o_ref.dtype)
    else:
        for g in range(groups):
            @pl.when(j == g)
            def _():
                for o_ref in o_refs[g::groups]:
                    o_ref[...] = y.astype(o_ref.dtype)


def _norm_matmul(x, gain, w, tn, split, dtypes):
    t, d = x.shape
    f = w.shape[1]
    tm = min(ROW_TILE, t)
    nj = f // tn
    if split:
        out_specs = [pl.BlockSpec((tm, tn), lambda i, j: (i, 0))] * (nj * len(dtypes))
        out_shape = [jax.ShapeDtypeStruct((t, tn), dt) for dt in dtypes for _ in range(nj)]
    else:
        out_specs = [pl.BlockSpec((tm, tn), lambda i, j: (i, j))] * len(dtypes)
        out_shape = [jax.ShapeDtypeStruct((t, f), dt) for dt in dtypes]
    outs = pl.pallas_call(
        functools.partial(_norm_matmul_kernel, groups=nj if split else 1),
        grid=(t // tm, nj),
        in_specs=[pl.BlockSpec((tm, d), lambda i, j: (i, 0)),
                  pl.BlockSpec((1, d), lambda i, j: (0, 0)),
                  pl.BlockSpec((d, tn), lambda i, j: (0, j))],
        out_specs=out_specs,
        out_shape=out_shape,
        scratch_shapes=[pltpu.VMEM((tm, d), BF16)],
        compiler_params=_params("parallel", "arbitrary"),
        name="norm_matmul",
    )(x, gain.reshape(1, d), w.astype(BF16))
    if split:
        return [outs[k * nj:(k + 1) * nj] for k in range(len(dtypes))]
    return outs


def _branch_proj_kernel(x_ref, g_ref, w_ref, *rest, keep_f32, scale):
    *o_refs, xn_ref, y_ref = rest
    chunks, tm, _ = y_ref.shape
    tn = chunks * LANES
    j = pl.program_id(1)

    @pl.when(j == 0)
    def _():
        xn_ref[...] = _rms(x_ref[...], g_ref[...]).astype(BF16)

    y = jnp.dot(xn_ref[...], w_ref[...], preferred_element_type=F32)
    if scale != 1.0:
        y = y * scale
    for b, dil in enumerate(BRANCH_DILATIONS):
        @pl.when(j == b)
        def _():
            if keep_f32:
                o_refs[N_BRANCH + b][...] = y
            if dil == 1:
                o_refs[b][...] = y.astype(BF16)
            else:
                for c in range(chunks):
                    y_ref[c] = y[:, c * LANES:(c + 1) * LANES]
                for r in range(dil):
                    for c in range(chunks):
                        lo = r * tn + c * LANES
                        o_refs[b][:, lo:lo + LANES] = y_ref[c, pl.ds(r, tm // dil, stride=dil), :].astype(BF16)


def _branch_proj(x, gain, w, keep_f32, scale=1.0):
    t, d = x.shape
    tn = w.shape[1] // N_BRANCH
    tm = min(ROW_TILE, t)
    row = lambda i, j: (i, 0)
    out_specs = [pl.BlockSpec((tm // dil, dil * tn), row) for dil in BRANCH_DILATIONS]
    out_shape = [jax.ShapeDtypeStruct((t // dil, dil * tn), BF16) for dil in BRANCH_DILATIONS]
    if keep_f32:
        out_specs += [pl.BlockSpec((tm, tn), row)] * N_BRANCH
        out_shape += [jax.ShapeDtypeStruct((t, tn), F32)] * N_BRANCH
    outs = pl.pallas_call(
        functools.partial(_branch_proj_kernel, keep_f32=keep_f32, scale=scale),
        grid=(t // tm, N_BRANCH),
        in_specs=[pl.BlockSpec((tm, d), row),
                  pl.BlockSpec((1, d), lambda i, j: (0, 0)),
                  pl.BlockSpec((d, tn), lambda i, j: (0, j))],
        out_specs=out_specs,
        out_shape=out_shape,
        scratch_shapes=[pltpu.VMEM((tm, d), BF16), pltpu.VMEM((tn // LANES, tm, LANES), F32)],
        compiler_params=_params("parallel", "arbitrary"),
        name="branch_proj",
    )(x, gain.reshape(1, d), w.astype(BF16))
    return outs[:N_BRANCH], outs[N_BRANCH:]


def _inproj_kernel(x_ref, g_ref, w_ref, wg_ref, o_ref, og_ref, xn_ref):
    @pl.when(pl.program_id(1) == 0)
    def _():
        xn = _rms(x_ref[...], g_ref[...]).astype(BF16)
        xn_ref[...] = xn
        og_ref[...] = jnp.dot(xn, wg_ref[...], preferred_element_type=F32)

    o_ref[...] = jnp.dot(xn_ref[...], w_ref[...], preferred_element_type=F32)


def _inproj(x, gain, w_in):
    t, d = x.shape
    tm = min(ROW_TILE, t)
    tn = 1024
    w_main = w_in[:, :QKVZ_W_A].astype(BF16)
    w_gate = jnp.pad(w_in[:, QKVZ_W_A:], ((0, 0), (0, LANES - 2 * HEADS_A))).astype(BF16)
    return pl.pallas_call(
        _inproj_kernel,
        grid=(t // tm, QKVZ_W_A // tn),
        in_specs=[pl.BlockSpec((tm, d), lambda i, j: (i, 0)),
                  pl.BlockSpec((1, d), lambda i, j: (0, 0)),
                  pl.BlockSpec((d, tn), lambda i, j: (0, j)),
                  pl.BlockSpec((d, LANES), lambda i, j: (0, 0))],
        out_specs=[pl.BlockSpec((tm, tn), lambda i, j: (i, j)),
                   pl.BlockSpec((tm, LANES), lambda i, j: (i, 0))],
        out_shape=[jax.ShapeDtypeStruct((t, QKVZ_W_A), F32),
                   jax.ShapeDtypeStruct((t, LANES), F32)],
        scratch_shapes=[pltpu.VMEM((tm, d), BF16)],
        compiler_params=_params("parallel", "arbitrary"),
        name="gdn_inproj",
    )(x, gain.reshape(1, d), w_main, w_gate)


def _inproj_conv_kernel(x_ref, g_ref, w_ref, wg_ref, cw_ref, c0_ref, o_ref, og_ref, tail_ref, xn_ref, halo_ref,
                        *, tiles_per_seq):
    i = pl.program_id(0)
    j = pl.program_id(1)
    halo = SUBLANES
    qkv_tiles = QKV_W_A // KEY_W_A

    @pl.when(j == 0)
    def _():
        xn = _rms(x_ref[...], g_ref[...]).astype(BF16)
        xn_ref[...] = xn
        og_ref[...] = jnp.dot(xn, wg_ref[...], preferred_element_type=F32)

    @pl.when(j < qkv_tiles)
    def _():
        y = jnp.dot(xn_ref[...], w_ref[...], preferred_element_type=F32)
        tm = y.shape[0]
        prev = jnp.where(i % tiles_per_seq == 0, c0_ref[...], halo_ref[j])
        ext = jnp.concatenate([prev, y], axis=0)
        conv = y * cw_ref[CONV_W - 1:CONV_W, :]
        for back in range(1, CONV_W):
            conv = conv + pltpu.roll(ext, back, axis=0)[halo:] * cw_ref[CONV_W - 1 - back:CONV_W - back, :]
        halo_ref[j] = y[tm - halo:]
        tail_ref[...] = y[tm - halo:]
        act = conv * _sigmoid(conv)
        col_head = lax.broadcasted_iota(jnp.int32, (y.shape[1], LANES), 0) // DK_A
        lane = lax.broadcasted_iota(jnp.int32, (y.shape[1], LANES), 1)
        sums = _dot(act * act, (col_head == lane).astype(BF16))
        unit = lax.rsqrt(sums + EPS) * jnp.where(j == 0, DK_A ** -0.5, 1.0)
        factor = jnp.where(j < 2, unit, 1.0)
        for h in range(HEADS_A):
            sl = slice(h * DK_A, (h + 1) * DK_A)
            o_ref[:, sl] = (act[:, sl] * factor[:, h:h + 1]).astype(o_ref.dtype)

    @pl.when(j == qkv_tiles)
    def _():
        y = jnp.dot(xn_ref[...], w_ref[...], preferred_element_type=F32)
        o_ref[...] = (y * _sigmoid(y)).astype(o_ref.dtype)


def _inproj_conv(x, gain, w_in, w_conv, conv0, n, seq_len):
    t, d = x.shape
    tm = min(ROW_TILE, seq_len)
    tn = KEY_W_A
    halo = SUBLANES
    tiles_per_seq = seq_len // tm
    last_qkv = QKV_W_A // tn - 1
    w_main = w_in[:, :QKVZ_W_A].astype(BF16)
    w_gate = jnp.pad(w_in[:, QKVZ_W_A:], ((0, 0), (0, LANES - 2 * HEADS_A))).astype(BF16)
    c0 = jnp.pad(conv0, ((0, 0), (halo - (CONV_W - 1), 0), (0, 0)))
    conv_col = lambda i, j: (0, jnp.minimum(j, last_qkv))
    seq_col = lambda i, j: (i // tiles_per_seq, 0, jnp.minimum(j, last_qkv))
    return pl.pallas_call(
        functools.partial(_inproj_conv_kernel, tiles_per_seq=tiles_per_seq),
        grid=(t // tm, QKVZ_W_A // tn),
        in_specs=[pl.BlockSpec((tm, d), lambda i, j: (i, 0)),
                  pl.BlockSpec((1, d), lambda i, j: (0, 0)),
                  pl.BlockSpec((d, tn), lambda i, j: (0, j)),
                  pl.BlockSpec((d, LANES), lambda i, j: (0, 0)),
                  pl.BlockSpec((CONV_W, tn), conv_col),
                  pl.BlockSpec((None, halo, tn), seq_col)],
        out_specs=[pl.BlockSpec((tm, tn), lambda i, j: (i, j)),
                   pl.BlockSpec((tm, LANES), lambda i, j: (i, 0)),
                   pl.BlockSpec((None, halo, tn), lambda i, j: (i, 0, jnp.minimum(j, last_qkv)))],
        out_shape=[jax.ShapeDtypeStruct((t, QKVZ_W_A), BF16),
                   jax.ShapeDtypeStruct((t, LANES), F32),
                   jax.ShapeDtypeStruct((t // tm, halo, QKV_W_A), F32)],
        scratch_shapes=[pltpu.VMEM((tm, d), BF16), pltpu.VMEM((QKV_W_A // tn, halo, tn), F32)],
        compiler_params=_params("arbitrary", "arbitrary"),
        name="gdn_inproj_conv",
    )(x, gain.reshape(1, d), w_main, w_gate, w_conv, c0)


def _mlp_kernel(*refs, n_mix, final_norm):
    x_ref = refs[0]
    mix_refs = refs[1:1 + n_mix]
    wo_ref, g_ref, wup_ref, wdn_ref, gf_ref, y_ref, xn_ref = refs[1 + n_mix:1 + n_mix + 7]
    nat_refs = refs[1 + n_mix + 7:]
    f = pl.program_id(1)

    @pl.when(f == 0)
    def _():
        if n_mix == 1:
            mixed = mix_refs[0][...]
        else:
            tm = x_ref.shape[0]
            chunks = Q_W_BRANCH // LANES
            for b, dil in enumerate(BRANCH_DILATIONS):
                o_blk, lse_blk = mix_refs[b], mix_refs[N_BRANCH + b]
                o_nat, lse_nat = nat_refs[b], nat_refs[N_BRANCH + b]
                rows = tm // dil
                for r in range(dil):
                    dst = pl.ds(r, rows, stride=dil) if dil > 1 else pl.ds(0, rows)
                    for c in range(chunks):
                        lo = r * Q_W_BRANCH + c * LANES
                        o_nat[c, dst, :] = o_blk[:, lo:lo + LANES].astype(F32)
                    lse_nat[dst, :] = lse_blk[r]
            lses = [r[...] for r in nat_refs[N_BRANCH:]]
            m = functools.reduce(jnp.maximum, lses)
            ws = [jnp.exp(l - m) for l in lses]
            total = sum(ws)
            low = lax.broadcasted_iota(jnp.int32, (1, 2 * HEAD_DIM_B), 1) < HEAD_DIM_B
            mixed = 0.0
            for w, o_nat in zip(ws, nat_refs[:N_BRANCH]):
                w = w / total
                mixed = mixed + jnp.concatenate(
                    [jnp.where(low, w[:, 2 * p:2 * p + 1], w[:, 2 * p + 1:2 * p + 2]) * o_nat[p]
                     for p in range(HEADS_B // 2)], axis=1)
        x1 = x_ref[...] + _dot(mixed, wo_ref[...])
        y_ref[...] = x1
        xn_ref[...] = _rms(x1, g_ref[...]).astype(BF16)

    u = jnp.maximum(jnp.dot(xn_ref[...], wup_ref[...], preferred_element_type=F32), 0.0)
    y_ref[...] += _dot(u * u, wdn_ref[...])

    if final_norm:
        @pl.when(f == pl.num_programs(1) - 1)
        def _():
            y_ref[...] = _rms(y_ref[...], gf_ref[...])


def _mlp_block(x, mix, w_o, gain, w_up, w_down, layer, gain_final, final_norm):
    t, d = x.shape
    k = w_o.shape[0]
    dff = w_up.shape[2]
    tm = min(ROW_TILE, t)
    tf = MLP_FF_TILE
    n_mix = len(mix)
    row = lambda i, f: (i, 0)
    const = lambda i, f: (0, 0)
    scratch = [pltpu.VMEM((tm, d), BF16)]
    if n_mix == 1:
        mix_specs = [pl.BlockSpec((tm, k), row)]
    else:
        mix_specs = [pl.BlockSpec((tm // dil, dil * Q_W_BRANCH), row) for dil in BRANCH_DILATIONS]
        for dil, lse in zip(BRANCH_DILATIONS, mix[N_BRANCH:]):
            per_seq = lse.shape[2] // (tm // dil)
            mix_specs.append(pl.BlockSpec((None, dil, tm // dil, LANES),
                                          lambda i, f, per_seq=per_seq: (i // per_seq, 0, i % per_seq, 0)))
        scratch += ([pltpu.VMEM((Q_W_BRANCH // LANES, tm, LANES), F32)] * N_BRANCH
                    + [pltpu.VMEM((tm, LANES), F32)] * N_BRANCH)
    return pl.pallas_call(
        functools.partial(_mlp_kernel, n_mix=n_mix, final_norm=final_norm),
        grid=(t // tm, dff // tf),
        in_specs=([pl.BlockSpec((tm, d), row)] + mix_specs
                  + [pl.BlockSpec((k, d), const),
                     pl.BlockSpec((1, d), const),
                     pl.BlockSpec((None, d, tf), lambda i, f: (layer, 0, f)),
                     pl.BlockSpec((None, tf, d), lambda i, f: (layer, f, 0)),
                     pl.BlockSpec((1, d), const)]),
        out_specs=pl.BlockSpec((tm, d), row),
        out_shape=jax.ShapeDtypeStruct((t, d), F32),
        scratch_shapes=scratch,
        compiler_params=_params("parallel", "arbitrary"),
        name="mix_out_mlp",
    )(x, *mix, w_o.astype(BF16), gain.reshape(1, d), w_up.astype(BF16), w_down.astype(BF16),
      gain_final.reshape(1, d))


def _inverse_masks(c):
    row = lax.broadcasted_iota(jnp.int32, (c, c), 0)
    col = lax.broadcasted_iota(jnp.int32, (c, c), 1)
    masks = []
    shift = 0
    while (1 << shift) < c:
        same_pair = (row >> (shift + 1)) == (col >> (shift + 1))
        other_half = (row >> shift) != (col >> shift)
        masks.append((row > col) & same_pair & other_half)
        shift += 1
    return masks


def _unit_lower_inverses(mats, c):
    masks = _inverse_masks(c)
    row = lax.broadcasted_iota(jnp.int32, (c, c), 0)
    col = lax.broadcasted_iota(jnp.int32, (c, c), 1)
    eye = (row == col).astype(F32)
    invs = [eye - jnp.where(masks[0], a, 0.0) for a in mats]
    for mask in masks[1:]:
        xs = [_dot(inv, jnp.where(mask, a, 0.0)) for inv, a in zip(invs, mats)]
        invs = [inv - _dot(x, inv) for inv, x in zip(invs, xs)]
    return invs


def _gdn_kernel(q_ref, k_ref, v_ref, z_ref, ba_ref, bat_ref, s0_ref, alr_ref, dtr_ref, alc_ref, dtc_ref, gn_ref,
                o_ref, s_ref, *, chunk, chunks_per_step):
    c = chunk

    @pl.when(pl.program_id(1) == 0)
    def _():
        s_ref[...] = s0_ref[...]

    row = lax.broadcasted_iota(jnp.int32, (c, c), 0)
    col = lax.broadcasted_iota(jnp.int32, (c, c), 1)
    causal = row >= col
    strict = row > col
    lower_ones = causal.astype(F32)
    upper_ones = (row <= col).astype(F32)
    heads = range(HEADS_A)
    pairs = [(ck, h) for ck in range(chunks_per_step) for h in heads]
    rows_of = [slice(ck * c, (ck + 1) * c) for ck in range(chunks_per_step)]
    cols_of = [slice(h * DK_A, (h + 1) * DK_A) for h in heads]

    beta_cols, cum_cols, cum_rows = [], [], []
    for ck in range(chunks_per_step):
        ba = ba_ref[rows_of[ck], :]
        bat = bat_ref[ck]
        beta_cols.append(_sigmoid(ba))
        g_cols = -jnp.exp(alr_ref[...]) * _softplus(ba + dtr_ref[...])
        g_rows = -jnp.exp(alc_ref[...]) * _softplus(bat + dtc_ref[...])
        cum_cols.append(_dot_f32(lower_ones, g_cols))
        cum_rows.append(_dot_f32(g_rows, upper_ones))

    qs = [q_ref[rows_of[ck], cols_of[h]] for ck, h in pairs]
    ks = [k_ref[rows_of[ck], cols_of[h]] for ck, h in pairs]
    kfs = [k.astype(F32) for k in ks]
    betas = [beta_cols[ck][:, h:h + 1] for ck, h in pairs]
    gcs = [cum_cols[ck][:, HEADS_A + h:HEADS_A + h + 1] for ck, h in pairs]
    grs = [cum_rows[ck][HEADS_A + h:HEADS_A + h + 1, :] for ck, h in pairs]
    decays = [jnp.where(causal, jnp.exp(jnp.where(causal, gc - gr, 0.0)), 0.0) for gc, gr in zip(gcs, grs)]
    kbs = [k * b for k, b in zip(kfs, betas)]
    kq = [_dot_nt(jnp.concatenate([kb.astype(BF16), q], axis=0), k) for kb, q, k in zip(kbs, qs, ks)]
    mats = [jnp.where(strict, x[:c] * d, 0.0) for x, d in zip(kq, decays)]
    attns = [x[c:] * d for x, d in zip(kq, decays)]
    t_invs = _unit_lower_inverses(mats, c)
    e_gcs = [jnp.exp(gc) for gc in gcs]
    uws = [_dot(t, jnp.concatenate([v_ref[rows_of[ck], cols_of[h]].astype(F32) * b, kb * e], axis=1))
           for t, (ck, h), b, kb, e in zip(t_invs, pairs, betas, kbs, e_gcs)]
    q_decs = [q.astype(F32) * e for q, e in zip(qs, e_gcs)]

    states = [s_ref[h] for h in heads]
    for ck in range(chunks_per_step):
        at = slice(ck * HEADS_A, (ck + 1) * HEADS_A)
        wq_s = [_dot(jnp.concatenate([uw[:, DV_A:], qd], axis=0), s)
                for uw, qd, s in zip(uws[at], q_decs[at], states)]
        v_news = [uw[:, :DV_A] - x[:c] for uw, x in zip(uws[at], wq_s)]
        outs = [x[c:] + _dot(attn, vn) for x, attn, vn in zip(wq_s, attns[at], v_news)]
        g_lasts = [gc[c - 1:c, :] for gc in gcs[at]]
        states = [s * jnp.exp(gl) + _dot_tn(kf * jnp.exp(gl - gc), vn)
                  for s, gl, kf, gc, vn in zip(states, g_lasts, kfs[at], gcs[at], v_news)]
        for h in heads:
            gate = z_ref[rows_of[ck], cols_of[h]].astype(F32)
            o_ref[rows_of[ck], cols_of[h]] = (_rms(outs[h], gn_ref[...]) * gate).astype(o_ref.dtype)
    for h in heads:
        s_ref[h] = states[h]


def _gdn(act, ba, n, seq_len, chunk, s0, a_log, dt_bias, g_norm):
    nc = seq_len // chunk
    per_step = GDN_CHUNKS_PER_STEP
    steps = nc // per_step
    rows = per_step * chunk
    t = n * seq_len
    bat = ba[:, :2 * HEADS_A].reshape(n * nc, chunk, 2 * HEADS_A).transpose(0, 2, 1)
    pad_row = lambda p: jnp.pad(p.reshape(1, HEADS_A), ((0, 0), (HEADS_A, LANES - 2 * HEADS_A)))
    pad_col = lambda p: jnp.pad(p.reshape(HEADS_A, 1), ((HEADS_A, 0), (0, 0)))
    blk = lambda i, j: (i * steps + j, 0)
    const = lambda i, j: (0, 0)
    col_blk = lambda col: pl.BlockSpec((rows, KEY_W_A), lambda i, j: (i * steps + j, col))
    return pl.pallas_call(
        functools.partial(_gdn_kernel, chunk=chunk, chunks_per_step=per_step),
        grid=(n, steps),
        in_specs=[col_blk(0), col_blk(1), col_blk(2), col_blk(3),
                  pl.BlockSpec((rows, LANES), blk),
                  pl.BlockSpec((per_step, 2 * HEADS_A, chunk), lambda i, j: (i * steps + j, 0, 0)),
                  pl.BlockSpec((None, HEADS_A, DK_A, DV_A), lambda i, j: (i, 0, 0, 0)),
                  pl.BlockSpec((1, LANES), const),
                  pl.BlockSpec((1, LANES), const),
                  pl.BlockSpec((2 * HEADS_A, 1), const),
                  pl.BlockSpec((2 * HEADS_A, 1), const),
                  pl.BlockSpec((1, DV_A), const)],
        out_specs=[pl.BlockSpec((rows, VAL_W_A), blk),
                   pl.BlockSpec((None, HEADS_A, DK_A, DV_A), lambda i, j: (i, 0, 0, 0))],
        out_shape=[jax.ShapeDtypeStruct((t, VAL_W_A), BF16),
                   jax.ShapeDtypeStruct((n, HEADS_A, DK_A, DV_A), F32)],
        compiler_params=_params("parallel", "arbitrary"),
        name="gdn_delta_rule",
    )(act, act, act, act, ba, bat, s0, pad_row(a_log), pad_row(dt_bias), pad_col(a_log), pad_col(dt_bias),
      g_norm.reshape(1, DV_A))


def _gdn_step_kernel(qkv_ref, z_ref, ba_ref, st_ref, cw_ref, s0_ref, alr_ref, dtr_ref, gn_ref, o_ref, s_ref):
    nb = qkv_ref.shape[0]
    conv = qkv_ref[...] * cw_ref[CONV_W - 1:CONV_W, :]
    for j in range(CONV_W - 1):
        conv = conv + st_ref[j] * cw_ref[j:j + 1, :]
    act = conv * _sigmoid(conv)
    ba = ba_ref[...]
    beta_cols = _sigmoid(ba)
    decay_cols = jnp.exp(-jnp.exp(alr_ref[...]) * _softplus(ba + dtr_ref[...]))
    fill = jnp.zeros((DK_A - nb, DK_A), F32)
    for h in range(HEADS_A):
        lo = h * DK_A
        q = act[:, lo:lo + DK_A]
        k = act[:, KEY_W_A + lo:KEY_W_A + lo + DK_A]
        v = act[:, 2 * KEY_W_A + lo:2 * KEY_W_A + lo + DV_A]
        q = q * lax.rsqrt(jnp.sum(q * q, axis=-1, keepdims=True) + EPS) * (DK_A ** -0.5)
        k = k * lax.rsqrt(jnp.sum(k * k, axis=-1, keepdims=True) + EPS)
        q_t = jnp.concatenate([q, fill], axis=0).T
        k_t = jnp.concatenate([k, fill], axis=0).T
        z = z_ref[:, lo:lo + DV_A]
        gate = z * _sigmoid(z)
        seqs = range(nb)
        k_cols = [k_t[:, i:i + 1] for i in seqs]
        decays = [decay_cols[i:i + 1, HEADS_A + h:HEADS_A + h + 1] for i in seqs]
        k_s = [jnp.sum(k_cols[i] * s0_ref[i, h], axis=0, keepdims=True) for i in seqs]
        v_new = [beta_cols[i:i + 1, h:h + 1] * (v[i:i + 1, :] - decays[i] * k_s[i]) for i in seqs]
        s_new = [decays[i] * s0_ref[i, h] + k_cols[i] * v_new[i] for i in seqs]
        for i in seqs:
            s_ref[i, h] = s_new[i]
        outs = [jnp.sum(q_t[:, i:i + 1] * s_new[i], axis=0, keepdims=True) for i in seqs]
        for i in seqs:
            o_ref[i:i + 1, lo:lo + DV_A] = _rms(outs[i], gn_ref[...]) * gate[i:i + 1, :]


def _gdn_step(qkvz, ba, s0_layers, conv0_layers, layer, w_conv, a_log, dt_bias, g_norm):
    n = qkvz.shape[0]
    nb = SUBLANES
    pad_row = lambda p: jnp.pad(p.reshape(1, HEADS_A), ((0, 0), (HEADS_A, LANES - 2 * HEADS_A)))
    blk = lambda i: (i, 0)
    const = lambda i: (0, 0)
    return pl.pallas_call(
        _gdn_step_kernel,
        grid=(n // nb,),
        in_specs=[pl.BlockSpec((nb, QKV_W_A), blk),
                  pl.BlockSpec((nb, VAL_W_A), lambda i: (i, QKV_W_A // VAL_W_A)),
                  pl.BlockSpec((nb, LANES), blk),
                  pl.BlockSpec((None, CONV_W - 1, nb, QKV_W_A), lambda i: (layer, 0, i, 0)),
                  pl.BlockSpec((CONV_W, QKV_W_A), const),
                  pl.BlockSpec((None, nb, HEADS_A, DK_A, DV_A), lambda i: (layer, i, 0, 0, 0)),
                  pl.BlockSpec((1, LANES), const),
                  pl.BlockSpec((1, LANES), const),
                  pl.BlockSpec((1, DV_A), const)],
        out_specs=[pl.BlockSpec((nb, VAL_W_A), blk),
                   pl.BlockSpec((nb, HEADS_A, DK_A, DV_A), lambda i: (i, 0, 0, 0))],
        out_shape=[jax.ShapeDtypeStruct((n, VAL_W_A), F32),
                   jax.ShapeDtypeStruct((n, HEADS_A, DK_A, DV_A), F32)],
        compiler_params=_params("parallel"),
        name="gdn_step",
    )(qkvz, qkvz, ba, conv0_layers.transpose(0, 2, 1, 3), w_conv, s0_layers, pad_row(a_log), pad_row(dt_bias),
      g_norm.reshape(1, DV_A))


def _attn_prompt_kernel(q_ref, kvc_ref, kvp_ref, o_ref, lse_ref, *, dilation, sub_blocks):
    blk = ATT_BLOCK
    hd = HEAD_DIM_B
    a = pl.program_id(2)
    qi = lax.broadcasted_iota(jnp.int32, (blk, 2 * blk), 0)
    kj = lax.broadcasted_iota(jnp.int32, (blk, 2 * blk), 1)
    steps = blk + qi - kj
    in_window = (steps >= 0) & (steps <= blk)
    started = in_window & ((kj >= blk) | (a > 0))
    dist = (steps * dilation).astype(F32)
    bias_first = [jnp.where(started, -slope * dist, -jnp.inf) for slope in ALIBI_SLOPES]
    bias_later = [jnp.where(in_window, -slope * dist, -jnp.inf) for slope in ALIBI_SLOPES]
    low = lax.broadcasted_iota(jnp.int32, (1, 2 * hd), 1) < hd
    kv_all = jnp.concatenate([kvp_ref[...], kvc_ref[...]], axis=0)
    padded = []
    for kvh in range(KV_HEADS_B):
        own = low if kvh == 0 else ~low
        both = []
        for x in (kv_all[:, :2 * hd], kv_all[:, 2 * hd:]):
            kept = jnp.where(own, x, jnp.zeros_like(x))
            moved = pltpu.roll(kept, hd, axis=1)
            both += [kept, moved] if kvh == 0 else [moved, kept]
        padded.append(both)
    head_lane = lax.broadcasted_iota(jnp.int32, (1, LANES), 1)
    for j in range(sub_blocks):
        lse_all = jnp.zeros((blk, LANES), F32)
        bias = bias_first if j == 0 else bias_later
        rows = slice(j * blk, (j + 1) * blk)
        keys = slice(j * blk, (j + 2) * blk)
        heads = range(HEADS_B)
        kv_of = [[x[keys] for x in padded[h // GQA_B]] for h in heads]
        qs = [q_ref[rows, (h // 2) * 2 * hd:(h // 2 + 1) * 2 * hd] for h in heads]
        ss = [_dot_nt(q, kv[h % 2]) + bias[h] for q, kv, h in zip(qs, kv_of, heads)]
        ms = [jnp.max(s, axis=-1, keepdims=True) for s in ss]
        ps = [jnp.exp(s - m) for s, m in zip(ss, ms)]
        ls = [jnp.sum(p, axis=-1, keepdims=True) for p in ps]
        for g in range(0, HEADS_B, 2):
            o = _dot(ps[g], kv_of[g][2]) + _dot(ps[g + 1], kv_of[g][3])
            o_ref[rows, g * hd:(g + 2) * hd] = (o / jnp.where(low, ls[g], ls[g + 1])).astype(o_ref.dtype)
        for h in heads:
            lse_all = jnp.where(head_lane == h, ms[h] + jnp.log(ls[h]), lse_all)
        lse_ref[rows, :] = lse_all


def _attn_prompt_branch(q, kv, n, seq_len, dilation):
    ls = seq_len // dilation
    sub_blocks = min(4, ls // ATT_BLOCK)
    qb = sub_blocks * ATT_BLOCK
    qv = q.reshape(n, ls, dilation * Q_W_BRANCH)
    kvv = kv.reshape(n, ls, dilation * KV_W_BRANCH)
    cur = lambda i, r, a: (i, a, r)
    prev = lambda i, r, a: (i, jnp.maximum(a * sub_blocks - 1, 0), r)
    o, lse = pl.pallas_call(
        functools.partial(_attn_prompt_kernel, dilation=dilation, sub_blocks=sub_blocks),
        grid=(n, dilation, ls // qb),
        in_specs=[pl.BlockSpec((None, qb, Q_W_BRANCH), cur),
                  pl.BlockSpec((None, qb, KV_W_BRANCH), cur),
                  pl.BlockSpec((None, ATT_BLOCK, KV_W_BRANCH), prev)],
        out_specs=[pl.BlockSpec((None, qb, Q_W_BRANCH), cur),
                   pl.BlockSpec((None, None, qb, LANES), lambda i, r, a: (i, r, a, 0))],
        out_shape=[jax.ShapeDtypeStruct(qv.shape, BF16),
                   jax.ShapeDtypeStruct((n, dilation, ls, LANES), F32)],
        compiler_params=_params("parallel", "parallel", "arbitrary"),
        name=f"attn_prompt_d{dilation}",
    )(qv, kvv, kvv)
    return o.reshape(n * ls, dilation * Q_W_BRANCH), lse


def _cache_update_kernel(c_ref, kvn_ref, o_ref, g_ref, *, window, dilation):
    nb = c_ref.shape[0]
    nk = ATT_BLOCK
    last = lax.broadcasted_iota(jnp.int32, (1, window), 1) == window - 1
    new_t = jnp.concatenate([kvn_ref[...], jnp.zeros((LANES - nb, KV_W_BRANCH), F32)], axis=0).T
    if dilation > 1:
        src = lax.broadcasted_iota(jnp.int32, (window, nk), 0)
        dst = lax.broadcasted_iota(jnp.int32, (window, nk), 1)
        pick = (src == dst * dilation).astype(BF16)
    for i in range(nb):
        x = c_ref[i].reshape(KV_W_BRANCH, window)
        shifted = jnp.where(last, new_t[:, i:i + 1], pltpu.roll(x, window - 1, axis=1))
        o_ref[i] = shifted.reshape(o_ref.shape[1:])
        seen = x.astype(BF16) if dilation == 1 else _dot(x, pick).astype(BF16)
        g_ref[i] = seen.reshape(g_ref.shape[1:])


def _cache_update(cache_t, kv_new, window, dilation):
    n = cache_t.shape[0]
    nb = max(1, min(SUBLANES, 4096 // window))
    blk = (nb, 2, KV_HEADS_B, HEAD_DIM_B, window)
    seen_blk = (nb, 2, KV_HEADS_B * HEAD_DIM_B, ATT_BLOCK)
    return pl.pallas_call(
        functools.partial(_cache_update_kernel, window=window, dilation=dilation),
        grid=(n // nb,),
        in_specs=[pl.BlockSpec(blk, lambda i: (i, 0, 0, 0, 0)),
                  pl.BlockSpec((None, nb, KV_W_BRANCH), lambda i: (i, 0, 0))],
        out_specs=[pl.BlockSpec(blk, lambda i: (i, 0, 0, 0, 0)),
                   pl.BlockSpec(seen_blk, lambda i: (i, 0, 0, 0))],
        out_shape=[jax.ShapeDtypeStruct(cache_t.shape, F32),
                   jax.ShapeDtypeStruct((n,) + seen_blk[1:], BF16)],
        compiler_params=_params("parallel"),
        name=f"cache_update_w{window}",
    )(cache_t, kv_new.reshape(n // nb, nb, KV_W_BRANCH))


def _attn_sample_kernel(q_ref, kvn_ref, g0_ref, g1_ref, g2_ref, o_ref):
    nk = ATT_BLOCK
    nb = q_ref.shape[0]
    kv_w = KV_HEADS_B * HEAD_DIM_B
    head = lax.broadcasted_iota(jnp.int32, (HEADS_B, 1), 0)
    slopes = jnp.exp2(-8.0 * (head + 1).astype(F32) / HEADS_B)
    key = lax.broadcasted_iota(jnp.int32, (1, nk), 1)
    lane = lax.broadcasted_iota(jnp.int32, (HEADS_B, kv_w), 1)
    own_half = (lane // HEAD_DIM_B) == (head // GQA_B)
    scale = HEAD_DIM_B ** -0.5
    g_refs = (g0_ref, g1_ref, g2_ref)
    pairs = [(i, b) for i in range(nb) for b in range(N_BRANCH)]
    dists = [((nk - key) * d).astype(F32) for d in BRANCH_DILATIONS]
    qs = [q_ref[i, b * HEADS_B:(b + 1) * HEADS_B, :] * scale for i, b in pairs]
    s_old = [_dot(q, g_refs[b][i, 0]) - slopes * dists[b] for q, (i, b) in zip(qs, pairs)]
    s_new = [jnp.sum(q * kvn_ref[b, i:i + 1, :kv_w], axis=-1, keepdims=True) for q, (i, b) in zip(qs, pairs)]
    ms = [jnp.maximum(jnp.max(so, axis=-1, keepdims=True), sn) for so, sn in zip(s_old, s_new)]
    p_old = [jnp.exp(so - m) for so, m in zip(s_old, ms)]
    p_new = [jnp.exp(sn - m) for sn, m in zip(s_new, ms)]
    ls = [jnp.sum(po, axis=-1, keepdims=True) + pn for po, pn in zip(p_old, p_new)]
    outs = [(_dot_nt(po, g_refs[b][i, 1]) + pn * kvn_ref[b, i:i + 1, kv_w:]) / l
            for po, pn, l, (i, b) in zip(p_old, p_new, ls, pairs)]
    lses = [m + jnp.log(l) for m, l in zip(ms, ls)]
    for i in range(nb):
        sl = slice(i * N_BRANCH, (i + 1) * N_BRANCH)
        m = functools.reduce(jnp.maximum, lses[sl])
        ws = [jnp.exp(l - m) for l in lses[sl]]
        merged = sum(w * o for w, o in zip(ws, outs[sl])) / sum(ws)
        o_ref[i] = jnp.where(own_half, merged, 0.0)


def _spread_heads(w, axis):
    w = jnp.moveaxis(w, axis, -1)
    lead = w.shape[:-1]
    w = w.reshape(lead + (-1, KV_HEADS_B, GQA_B, HEAD_DIM_B))
    halves = [jnp.pad(w[..., g, :, :], [(0, 0)] * (len(lead) + 2) + [(g * HEAD_DIM_B, (KV_HEADS_B - 1 - g) * HEAD_DIM_B)])
              for g in range(KV_HEADS_B)]
    out = jnp.stack(halves, axis=-3)
    return jnp.moveaxis(out.reshape(lead + (-1,)), -1, axis)


def _attn_sample(q, kv_new, seen):
    n = q.shape[0]
    nb = SUBLANES
    rows = N_BRANCH * HEADS_B
    seen_blk = (nb, 2, KV_HEADS_B * HEAD_DIM_B, ATT_BLOCK)
    out = pl.pallas_call(
        _attn_sample_kernel,
        grid=(n // nb,),
        in_specs=[pl.BlockSpec((nb, rows, LANES), lambda i: (i, 0, 0)),
                  pl.BlockSpec((N_BRANCH, nb, KV_W_BRANCH), lambda i: (0, i, 0))]
                 + [pl.BlockSpec(seen_blk, lambda i: (i, 0, 0, 0))] * N_BRANCH,
        out_specs=pl.BlockSpec((nb, HEADS_B, LANES), lambda i: (i, 0, 0)),
        out_shape=jax.ShapeDtypeStruct((n, HEADS_B, LANES), F32),
        compiler_params=_params("parallel"),
        name="attn_sample",
    )(q.reshape(n, rows, LANES), kv_new, *seen)
    return out.reshape(n, HEADS_B * LANES)


def _trunk(x, rec0, conv0, kv_bufs, w):
    n, seq_len, d = x.shape
    fresh = kv_bufs is None
    assert seq_len == 1 or (fresh and seq_len % (ATT_BLOCK * max(BRANCH_DILATIONS)) == 0)
    x = x.reshape(n * seq_len, d)
    rec_new, conv_new = [], []
    for l in range(N_LAYERS_A):
        if fresh:
            act, ba, tail = _inproj_conv(x, w['norm_mix'][l], w['w_in_a'][l], w['conv_a'][l], conv0[l], n, seq_len)
            last_tile = tail.reshape(n, -1, SUBLANES, QKV_W_A)[:, -1]
            conv_new.append(last_tile[:, SUBLANES - (CONV_W - 1):])
            o, s_new = _gdn(act, ba, n, seq_len, GDN_CHUNK, rec0[l], w['a_log'][l], w['dt_bias'][l],
                            w['norm_o_a'][l])
        else:
            qkvz, ba = _inproj(x, w['norm_mix'][l], w['w_in_a'][l])
            conv_new.append(jnp.concatenate([conv0[l][:, 1:], qkvz[:, None, :QKV_W_A]], axis=1))
            o, s_new = _gdn_step(qkvz, ba, rec0, conv0, l, w['conv_a'][l], w['a_log'][l], w['dt_bias'][l],
                                 w['norm_o_a'][l])
        rec_new.append(s_new)
        x = _mlp_block(x, [o], w['w_out_a'][l], w['norm_mlp'][l], w['w_up'], w['w_down'], l,
                       w['norm_final'], False)

    if fresh:
        kv_lo, kv = _branch_proj(x, w['norm_kv'], w['w_kv'], True)
    else:
        kv, = _norm_matmul(x, w['norm_kv'], w['w_kv'], KV_W_BRANCH, True, (F32,))
    bufs_new, seen = [], []
    for b in range(N_BRANCH):
        if fresh:
            keep = min(BRANCH_WINDOWS[b], seq_len)
            last = kv[b].reshape(n, seq_len, KV_W_BRANCH)[:, seq_len - keep:]
            bufs_new.append(last.reshape(n, keep, 2, KV_HEADS_B, HEAD_DIM_B))
        else:
            assert kv_bufs[b].shape[1] == BRANCH_WINDOWS[b]
            shifted, seen_b = _cache_update(jnp.transpose(kv_bufs[b], (0, 2, 3, 4, 1)), kv[b],
                                            BRANCH_WINDOWS[b], BRANCH_DILATIONS[b])
            bufs_new.append(jnp.transpose(shifted, (0, 4, 1, 2, 3)))
            seen.append(seen_b)

    for l in range(N_LAYERS_A, N_LAYERS_A + N_LAYERS_B):
        lb = l - N_LAYERS_A
        if fresh:
            q, _ = _branch_proj(x, w['norm_mix'][l], w['w_q_b'][lb], False, HEAD_DIM_B ** -0.5)
            parts = [_attn_prompt_branch(q[b], kv_lo[b], n, seq_len, BRANCH_DILATIONS[b]) for b in range(N_BRANCH)]
            mix = [p[0] for p in parts] + [p[1] for p in parts]
            w_o = w['w_o_b'][lb]
        else:
            q, = _norm_matmul(x, w['norm_mix'][l], _spread_heads(w['w_q_b'][lb], 1), 1024, False, (F32,))
            mix = [_attn_sample(q, jnp.stack(kv), seen)]
            w_o = _spread_heads(w['w_o_b'][lb], 0)
        x = _mlp_block(x, mix, w_o, w['norm_mlp'][l], w['w_up'], w['w_down'], l,
                       w['norm_final'], l == N_LAYERS_A + N_LAYERS_B - 1)
    return (x.reshape(n, seq_len, d), jnp.stack(rec_new), jnp.stack(conv_new), *bufs_new)


def kernel(x_prompt, x_sample, state_a_rec, state_a_conv, cache_b0_kv, cache_b1_kv, cache_b2_kv, norm_mix, norm_mlp, w_in_a, conv_a, a_log, dt_bias, norm_o_a, w_out_a, norm_kv, w_kv, w_q_b, w_o_b, w_up, w_down, norm_final):
    w = {'norm_mix': norm_mix, 'norm_mlp': norm_mlp, 'w_in_a': w_in_a, 'conv_a': conv_a, 'a_log': a_log,
         'dt_bias': dt_bias, 'norm_o_a': norm_o_a, 'w_out_a': w_out_a, 'norm_kv': norm_kv, 'w_kv': w_kv,
         'w_q_b': w_q_b, 'w_o_b': w_o_b, 'w_up': w_up, 'w_down': w_down, 'norm_final': norm_final}
    n_p = x_prompt.shape[0]
    p_rec0 = jnp.zeros((N_LAYERS_A, n_p, HEADS_A, DK_A, DV_A), state_a_rec.dtype)
    p_conv0 = jnp.zeros((N_LAYERS_A, n_p, CONV_W - 1, QKV_W_A), state_a_conv.dtype)
    prompt = _trunk(x_prompt, p_rec0, p_conv0, None, w)
    sample = _trunk(x_sample, state_a_rec, state_a_conv, [cache_b0_kv, cache_b1_kv, cache_b2_kv], w)
    return (prompt[0], sample[0], *prompt[1:], *sample[1:])
```

```python
import functools

import jax
import jax.numpy as jnp
from jax import lax
from jax.experimental import pallas as pl
from jax.experimental.pallas import tpu as pltpu

D_MODEL = 1024
HEADS_A = 8
DK_A = 128
DV_A = 128
KEY_W_A = HEADS_A * DK_A
VAL_W_A = HEADS_A * DV_A
QKV_W_A = 2 * KEY_W_A + VAL_W_A
QKVZ_W_A = QKV_W_A + VAL_W_A
CONV_W = 4
N_LAYERS_A = 2
N_LAYERS_B = 2
BRANCH_WINDOWS = (128, 512, 2048)
BRANCH_DILATIONS = (1, 4, 16)
N_BRANCH = 3
HEADS_B = 8
KV_HEADS_B = 2
GQA_B = HEADS_B // KV_HEADS_B
HEAD_DIM_B = 64
Q_W_BRANCH = HEADS_B * HEAD_DIM_B
KV_W_BRANCH = 2 * KV_HEADS_B * HEAD_DIM_B
ALIBI_SLOPES = tuple(2.0 ** (-8.0 * h / HEADS_B) for h in range(1, HEADS_B + 1))
EPS = 1e-6

LANES = 128
SUBLANES = 8
VMEM_LIMIT_BYTES = 56 * 1024 * 1024
ROW_TILE = 1024
MLP_FF_TILE = 512
GDN_CHUNK = 128
GDN_CHUNKS_PER_STEP = 2
ATT_BLOCK = 128
ATT_SUB_BLOCKS_PER_STAGE = 2

F32 = jnp.float32
BF16 = jnp.bfloat16


def _params(*sem):
    return pltpu.CompilerParams(dimension_semantics=sem, vmem_limit_bytes=VMEM_LIMIT_BYTES)


def _rms(x, gain):
    return x * lax.rsqrt(jnp.mean(x * x, axis=-1, keepdims=True) + EPS) * gain


def _sigmoid(x):
    return 1.0 / (1.0 + jnp.exp(-x))


def _softplus(x):
    return jnp.maximum(x, 0.0) + jnp.log(1.0 + jnp.exp(-jnp.abs(x)))


def _dot(a, b):
    return jnp.dot(a.astype(BF16), b.astype(BF16), preferred_element_type=F32)


def _dot_nt(a, b):
    return lax.dot_general(a.astype(BF16), b.astype(BF16), (((1,), (1,)), ((), ())),
                           preferred_element_type=F32)


def _dot_tn(a, b):
    return lax.dot_general(a.astype(BF16), b.astype(BF16), (((0,), (0,)), ((), ())),
                           preferred_element_type=F32)


def _dot_f32(a, b):
    return jnp.dot(a, b, precision=lax.Precision.HIGHEST, preferred_element_type=F32)


def _norm_matmul_kernel(x_ref, g_ref, w_ref, *rest, groups):
    *o_refs, xn_ref = rest
    j = pl.program_id(1)

    @pl.when(j == 0)
    def _():
        xn_ref[...] = _rms(x_ref[...], g_ref[...]).astype(BF16)

    y = jnp.dot(xn_ref[...], w_ref[...], preferred_element_type=F32)
    if groups == 1:
        for o_ref in o_refs:
            o_ref[...] = y.astype(o_ref.dtype)
    else:
        for g in range(groups):
            @pl.when(j == g)
            def _():
                for o_ref in o_refs[g::groups]:
                    o_ref[...] = y.astype(o_ref.dtype)


def _norm_matmul(x, gain, w, tn, split, dtypes):
    t, d = x.shape
    f = w.shape[1]
    tm = min(ROW_TILE, t)
    nj = f // tn
    if split:
        out_specs = [pl.BlockSpec((tm, tn), lambda i, j: (i, 0))] * (nj * len(dtypes))
        out_shape = [jax.ShapeDtypeStruct((t, tn), dt) for dt in dtypes for _ in range(nj)]
    else:
        out_specs = [pl.BlockSpec((tm, tn), lambda i, j: (i, j))] * len(dtypes)
        out_shape = [jax.ShapeDtypeStruct((t, f), dt) for dt in dtypes]
    outs = pl.pallas_call(
        functools.partial(_norm_matmul_kernel, groups=nj if split else 1),
        grid=(t // tm, nj),
        in_specs=[pl.BlockSpec((tm, d), lambda i, j: (i, 0)),
                  pl.BlockSpec((1, d), lambda i, j: (0, 0)),
                  pl.BlockSpec((d, tn), lambda i, j: (0, j))],
        out_specs=out_specs,
        out_shape=out_shape,
        scratch_shapes=[pltpu.VMEM((tm, d), BF16)],
        compiler_params=_params("parallel", "arbitrary"),
        name="norm_matmul",
    )(x, gain.reshape(1, d), w.astype(BF16))
    if split:
        return [outs[k * nj:(k + 1) * nj] for k in range(len(dtypes))]
    return outs


def _branch_proj_kernel(x_ref, g_ref, w_ref, *rest, keep_f32, scale):
    *o_refs, xn_ref, y_ref = rest
    chunks, tm, _ = y_ref.shape
    tn = chunks * LANES
    j = pl.program_id(1)

    @pl.when(j == 0)
    def _():
        xn_ref[...] = _rms(x_ref[...], g_ref[...]).astype(BF16)

    y = jnp.dot(xn_ref[...], w_ref[...], preferred_element_type=F32)
    if scale != 1.0:
        y = y * scale
    for b, dil in enumerate(BRANCH_DILATIONS):
        @pl.when(j == b)
        def _():
            if keep_f32:
                o_refs[N_BRANCH + b][...] = y
            if dil == 1:
                o_refs[b][...] = y.astype(BF16)
            else:
                for c in range(chunks):
                    y_ref[c] = y[:, c * LANES:(c + 1) * LANES]
                for r in range(dil):
                    for c in range(chunks):
                        lo = r * tn + c * LANES
                        o_refs[b][:, lo:lo + LANES] = y_ref[c, pl.ds(r, tm // dil, stride=dil), :].astype(BF16)


def _branch_proj(x, gain, w, keep_f32, scale=1.0):
    t, d = x.shape
    tn = w.shape[1] // N_BRANCH
    tm = min(ROW_TILE, t)
    row = lambda i, j: (i, 0)
    out_specs = [pl.BlockSpec((tm // dil, dil * tn), row) for dil in BRANCH_DILATIONS]
    out_shape = [jax.ShapeDtypeStruct((t // dil, dil * tn), BF16) for dil in BRANCH_DILATIONS]
    if keep_f32:
        out_specs += [pl.BlockSpec((tm, tn), row)] * N_BRANCH
        out_shape += [jax.ShapeDtypeStruct((t, tn), F32)] * N_BRANCH
    outs = pl.pallas_call(
        functools.partial(_branch_proj_kernel, keep_f32=keep_f32, scale=scale),
        grid=(t // tm, N_BRANCH),
        in_specs=[pl.BlockSpec((tm, d), row),
                  pl.BlockSpec((1, d), lambda i, j: (0, 0)),
                  pl.BlockSpec((d, tn), lambda i, j: (0, j))],
        out_specs=out_specs,
        out_shape=out_shape,
        scratch_shapes=[pltpu.VMEM((tm, d), BF16), pltpu.VMEM((tn // LANES, tm, LANES), F32)],
        compiler_params=_params("parallel", "arbitrary"),
        name="branch_proj",
    )(x, gain.reshape(1, d), w.astype(BF16))
    return outs[:N_BRANCH], outs[N_BRANCH:]


def _inproj_kernel(x_ref, g_ref, w_ref, wg_ref, o_ref, og_ref, xn_ref):
    @pl.when(pl.program_id(1) == 0)
    def _():
        xn = _rms(x_ref[...], g_ref[...]).astype(BF16)
        xn_ref[...] = xn
        og_ref[...] = jnp.dot(xn, wg_ref[...], preferred_element_type=F32)

    o_ref[...] = jnp.dot(xn_ref[...], w_ref[...], preferred_element_type=F32)


def _inproj(x, gain, w_in):
    t, d = x.shape
    tm = min(ROW_TILE, t)
    tn = 1024
    w_main = w_in[:, :QKVZ_W_A].astype(BF16)
    w_gate = jnp.pad(w_in[:, QKVZ_W_A:], ((0, 0), (0, LANES - 2 * HEADS_A))).astype(BF16)
    return pl.pallas_call(
        _inproj_kernel,
        grid=(t // tm, QKVZ_W_A // tn),
        in_specs=[pl.BlockSpec((tm, d), lambda i, j: (i, 0)),
                  pl.BlockSpec((1, d), lambda i, j: (0, 0)),
                  pl.BlockSpec((d, tn), lambda i, j: (0, j)),
                  pl.BlockSpec((d, LANES), lambda i, j: (0, 0))],
        out_specs=[pl.BlockSpec((tm, tn), lambda i, j: (i, j)),
                   pl.BlockSpec((tm, LANES), lambda i, j: (i, 0))],
        out_shape=[jax.ShapeDtypeStruct((t, QKVZ_W_A), F32),
                   jax.ShapeDtypeStruct((t, LANES), F32)],
        scratch_shapes=[pltpu.VMEM((tm, d), BF16)],
        compiler_params=_params("parallel", "arbitrary"),
        name="gdn_inproj",
    )(x, gain.reshape(1, d), w_main, w_gate)


def _inproj_conv_kernel(x_ref, g_ref, w_ref, wg_ref, cw_ref, c0_ref, o_ref, og_ref, tail_ref, xn_ref, halo_ref,
                        *, tiles_per_seq):
    i = pl.program_id(0)
    j = pl.program_id(1)
    halo = SUBLANES
    qkv_tiles = QKV_W_A // KEY_W_A

    @pl.when(j == 0)
    def _():
        xn = _rms(x_ref[...], g_ref[...]).astype(BF16)
        xn_ref[...] = xn
        og_ref[...] = jnp.dot(xn, wg_ref[...], preferred_element_type=F32)

    @pl.when(j < qkv_tiles)
    def _():
        y = jnp.dot(xn_ref[...], w_ref[...], preferred_element_type=F32)
        tm = y.shape[0]
        prev = jnp.where(i % tiles_per_seq == 0, c0_ref[...], halo_ref[j])
        ext = jnp.concatenate([prev, y], axis=0)
        conv = y * cw_ref[CONV_W - 1:CONV_W, :]
        for back in range(1, CONV_W):
            conv = conv + pltpu.roll(ext, back, axis=0)[halo:] * cw_ref[CONV_W - 1 - back:CONV_W - back, :]
        halo_ref[j] = y[tm - halo:]
        tail_ref[...] = y[tm - halo:]
        act = conv * _sigmoid(conv)
        col_head = lax.broadcasted_iota(jnp.int32, (y.shape[1], LANES), 0) // DK_A
        lane = lax.broadcasted_iota(jnp.int32, (y.shape[1], LANES), 1)
        sums = _dot(act * act, (col_head == lane).astype(BF16))
        unit = lax.rsqrt(sums + EPS) * jnp.where(j == 0, DK_A ** -0.5, 1.0)
        factor = jnp.where(j < 2, unit, 1.0)
        for h in range(HEADS_A):
            sl = slice(h * DK_A, (h + 1) * DK_A)
            o_ref[:, sl] = (act[:, sl] * factor[:, h:h + 1]).astype(o_ref.dtype)

    @pl.when(j == qkv_tiles)
    def _():
        y = jnp.dot(xn_ref[...], w_ref[...], preferred_element_type=F32)
        o_ref[...] = (y * _sigmoid(y)).astype(o_ref.dtype)


def _inproj_conv(x, gain, w_in, w_conv, conv0, n, seq_len):
    t, d = x.shape
    tm = min(ROW_TILE, seq_len)
    tn = KEY_W_A
    halo = SUBLANES
    tiles_per_seq = seq_len // tm
    last_qkv = QKV_W_A // tn - 1
    w_main = w_in[:, :QKVZ_W_A].astype(BF16)
    w_gate = jnp.pad(w_in[:, QKVZ_W_A:], ((0, 0), (0, LANES - 2 * HEADS_A))).astype(BF16)
    c0 = jnp.pad(conv0, ((0, 0), (halo - (CONV_W - 1), 0), (0, 0)))
    conv_col = lambda i, j: (0, jnp.minimum(j, last_qkv))
    seq_col = lambda i, j: (i // tiles_per_seq, 0, jnp.minimum(j, last_qkv))
    return pl.pallas_call(
        functools.partial(_inproj_conv_kernel, tiles_per_seq=tiles_per_seq),
        grid=(t // tm, QKVZ_W_A // tn),
        in_specs=[pl.BlockSpec((tm, d), lambda i, j: (i, 0)),
                  pl.BlockSpec((1, d), lambda i, j: (0, 0)),
                  pl.BlockSpec((d, tn), lambda i, j: (0, j)),
                  pl.BlockSpec((d, LANES), lambda i, j: (0, 0)),
                  pl.BlockSpec((CONV_W, tn), conv_col),
                  pl.BlockSpec((None, halo, tn), seq_col)],
        out_specs=[pl.BlockSpec((tm, tn), lambda i, j: (i, j)),
                   pl.BlockSpec((tm, LANES), lambda i, j: (i, 0)),
                   pl.BlockSpec((None, halo, tn), lambda i, j: (i, 0, jnp.minimum(j, last_qkv)))],
        out_shape=[jax.ShapeDtypeStruct((t, QKVZ_W_A), BF16),
                   jax.ShapeDtypeStruct((t, LANES), F32),
                   jax.ShapeDtypeStruct((t // tm, halo, QKV_W_A), F32)],
        scratch_shapes=[pltpu.VMEM((tm, d), BF16), pltpu.VMEM((QKV_W_A // tn, halo, tn), F32)],
        compiler_params=_params("arbitrary", "arbitrary"),
        name="gdn_inproj_conv",
    )(x, gain.reshape(1, d), w_main, w_gate, w_conv, c0)


def _mlp_kernel(*refs, n_mix, final_norm):
    x_ref = refs[0]
    mix_refs = refs[1:1 + n_mix]
    wo_ref, g_ref, wup_ref, wdn_ref, gf_ref, y_ref, xn_ref = refs[1 + n_mix:1 + n_mix + 7]
    nat_refs = refs[1 + n_mix + 7:]
    f = pl.program_id(1)

    @pl.when(f == 0)
    def _():
        if n_mix == 1:
            mixed = mix_refs[0][...]
        else:
            tm = x_ref.shape[0]
            chunks = Q_W_BRANCH // LANES
            for b, dil in enumerate(BRANCH_DILATIONS):
                o_blk, lse_blk = mix_refs[b], mix_refs[N_BRANCH + b]
                o_nat, lse_nat = nat_refs[b], nat_refs[N_BRANCH + b]
                rows = tm // dil
                for r in range(dil):
                    dst = pl.ds(r, rows, stride=dil) if dil > 1 else pl.ds(0, rows)
                    for c in range(chunks):
                        lo = r * Q_W_BRANCH + c * LANES
                        o_nat[c, dst, :] = o_blk[:, lo:lo + LANES].astype(F32)
                    lse_nat[dst, :] = lse_blk[r]
            lses = [r[...] for r in nat_refs[N_BRANCH:]]
            m = functools.reduce(jnp.maximum, lses)
            ws = [jnp.exp(l - m) for l in lses]
            total = sum(ws)
            low = lax.broadcasted_iota(jnp.int32, (1, 2 * HEAD_DIM_B), 1) < HEAD_DIM_B
            mixed = 0.0
            for w, o_nat in zip(ws, nat_refs[:N_BRANCH]):
                w = w / total
                mixed = mixed + jnp.concatenate(
                    [jnp.where(low, w[:, 2 * p:2 * p + 1], w[:, 2 * p + 1:2 * p + 2]) * o_nat[p]
                     for p in range(HEADS_B // 2)], axis=1)
        x1 = x_ref[...] + _dot(mixed, wo_ref[...])
        y_ref[...] = x1
        xn_ref[...] = _rms(x1, g_ref[...]).astype(BF16)

    u = jnp.maximum(jnp.dot(xn_ref[...], wup_ref[...], preferred_element_type=F32), 0.0)
    y_ref[...] += _dot(u * u, wdn_ref[...])

    if final_norm:
        @pl.when(f == pl.num_programs(1) - 1)
        def _():
            y_ref[...] = _rms(y_ref[...], gf_ref[...])


def _mlp_block(x, mix, w_o, gain, w_up, w_down, layer, gain_final, final_norm):
    t, d = x.shape
    k = w_o.shape[0]
    dff = w_up.shape[2]
    tm = min(ROW_TILE, t)
    tf = MLP_FF_TILE
    n_mix = len(mix)
    row = lambda i, f: (i, 0)
    const = lambda i, f: (0, 0)
    scratch = [pltpu.VMEM((tm, d), BF16)]
    if n_mix == 1:
        mix_specs = [pl.BlockSpec((tm, k), row)]
    else:
        mix_specs = [pl.BlockSpec((tm // dil, dil * Q_W_BRANCH), row) for dil in BRANCH_DILATIONS]
        for dil, lse in zip(BRANCH_DILATIONS, mix[N_BRANCH:]):
            per_seq = lse.shape[2] // (tm // dil)
            mix_specs.append(pl.BlockSpec((None, dil, tm // dil, LANES),
                                          lambda i, f, per_seq=per_seq: (i // per_seq, 0, i % per_seq, 0)))
        scratch += ([pltpu.VMEM((Q_W_BRANCH // LANES, tm, LANES), F32)] * N_BRANCH
                    + [pltpu.VMEM((tm, LANES), F32)] * N_BRANCH)
    return pl.pallas_call(
        functools.partial(_mlp_kernel, n_mix=n_mix, final_norm=final_norm),
        grid=(t // tm, dff // tf),
        in_specs=([pl.BlockSpec((tm, d), row)] + mix_specs
                  + [pl.BlockSpec((k, d), const),
                     pl.BlockSpec((1, d), const),
                     pl.BlockSpec((None, d, tf), lambda i, f: (layer, 0, f)),
                     pl.BlockSpec((None, tf, d), lambda i, f: (layer, f, 0)),
                     pl.BlockSpec((1, d), const)]),
        out_specs=pl.BlockSpec((tm, d), row),
        out_shape=jax.ShapeDtypeStruct((t, d), F32),
        scratch_shapes=scratch,
        compiler_params=_params("parallel", "arbitrary"),
        name="mix_out_mlp",
    )(x, *mix, w_o.astype(BF16), gain.reshape(1, d), w_up.astype(BF16), w_down.astype(BF16),
      gain_final.reshape(1, d))


def _inverse_masks(c):
    row = lax.broadcasted_iota(jnp.int32, (c, c), 0)
    col = lax.broadcasted_iota(jnp.int32, (c, c), 1)
    masks = []
    shift = 0
    while (1 << shift) < c:
        same_pair = (row >> (shift + 1)) == (col >> (shift + 1))
        other_half = (row >> shift) != (col >> shift)
        masks.append((row > col) & same_pair & other_half)
        shift += 1
    return masks


def _unit_lower_inverses(mats, c):
    masks = _inverse_masks(c)
    row = lax.broadcasted_iota(jnp.int32, (c, c), 0)
    col = lax.broadcasted_iota(jnp.int32, (c, c), 1)
    eye = (row == col).astype(F32)
    invs = [eye - jnp.where(masks[0], a, 0.0) for a in mats]
    for mask in masks[1:]:
        xs = [_dot(inv, jnp.where(mask, a, 0.0)) for inv, a in zip(invs, mats)]
        invs = [inv - _dot(x, inv) for inv, x in zip(invs, xs)]
    return invs


def _gdn_kernel(q_ref, k_ref, v_ref, z_ref, ba_ref, bat_ref, s0_ref, alr_ref, dtr_ref, alc_ref, dtc_ref, gn_ref,
                o_ref, s_ref, *, chunk, chunks_per_step):
    c = chunk

    @pl.when(pl.program_id(1) == 0)
    def _():
        s_ref[...] = s0_ref[...]

    row = lax.broadcasted_iota(jnp.int32, (c, c), 0)
    col = lax.broadcasted_iota(jnp.int32, (c, c), 1)
    causal = row >= col
    strict = row > col
    lower_ones = causal.astype(F32)
    upper_ones = (row <= col).astype(F32)
    heads = range(HEADS_A)
    pairs = [(ck, h) for ck in range(chunks_per_step) for h in heads]
    rows_of = [slice(ck * c, (ck + 1) * c) for ck in range(chunks_per_step)]
    cols_of = [slice(h * DK_A, (h + 1) * DK_A) for h in heads]

    beta_cols, cum_cols, cum_rows = [], [], []
    for ck in range(chunks_per_step):
        ba = ba_ref[rows_of[ck], :]
        bat = bat_ref[ck]
        beta_cols.append(_sigmoid(ba))
        g_cols = -jnp.exp(alr_ref[...]) * _softplus(ba + dtr_ref[...])
        g_rows = -jnp.exp(alc_ref[...]) * _softplus(bat + dtc_ref[...])
        cum_cols.append(_dot_f32(lower_ones, g_cols))
        cum_rows.append(_dot_f32(g_rows, upper_ones))

    qs = [q_ref[rows_of[ck], cols_of[h]] for ck, h in pairs]
    ks = [k_ref[rows_of[ck], cols_of[h]] for ck, h in pairs]
    kfs = [k.astype(F32) for k in ks]
    betas = [beta_cols[ck][:, h:h + 1] for ck, h in pairs]
    gcs = [cum_cols[ck][:, HEADS_A + h:HEADS_A + h + 1] for ck, h in pairs]
    grs = [cum_rows[ck][HEADS_A + h:HEADS_A + h + 1, :] for ck, h in pairs]
    decays = [jnp.where(causal, jnp.exp(jnp.where(causal, gc - gr, 0.0)), 0.0) for gc, gr in zip(gcs, grs)]
    kbs = [k * b for k, b in zip(kfs, betas)]
    kq = [_dot_nt(jnp.concatenate([kb.astype(BF16), q], axis=0), k) for kb, q, k in zip(kbs, qs, ks)]
    mats = [jnp.where(strict, x[:c] * d, 0.0) for x, d in zip(kq, decays)]
    attns = [x[c:] * d for x, d in zip(kq, decays)]
    t_invs = _unit_lower_inverses(mats, c)
    e_gcs = [jnp.exp(gc) for gc in gcs]
    uws = [_dot(t, jnp.concatenate([v_ref[rows_of[ck], cols_of[h]].astype(F32) * b, kb * e], axis=1))
           for t, (ck, h), b, kb, e in zip(t_invs, pairs, betas, kbs, e_gcs)]
    q_decs = [q.astype(F32) * e for q, e in zip(qs, e_gcs)]

    states = [s_ref[h] for h in heads]
    for ck in range(chunks_per_step):
        at = slice(ck * HEADS_A, (ck + 1) * HEADS_A)
        wq_s = [_dot(jnp.concatenate([uw[:, DV_A:], qd], axis=0), s)
                for uw, qd, s in zip(uws[at], q_decs[at], states)]
        v_news = [uw[:, :DV_A] - x[:c] for uw, x in zip(uws[at], wq_s)]
        outs = [x[c:] + _dot(attn, vn) for x, attn, vn in zip(wq_s, attns[at], v_news)]
        g_lasts = [gc[c - 1:c, :] for gc in gcs[at]]
        states = [s * jnp.exp(gl) + _dot_tn(kf * jnp.exp(gl - gc), vn)
                  for s, gl, kf, gc, vn in zip(states, g_lasts, kfs[at], gcs[at], v_news)]
        for h in heads:
            gate = z_ref[rows_of[ck], cols_of[h]].astype(F32)
            o_ref[rows_of[ck], cols_of[h]] = (_rms(outs[h], gn_ref[...]) * gate).astype(o_ref.dtype)
    for h in heads:
        s_ref[h] = states[h]


def _gdn(act, ba, n, seq_len, chunk, s0, a_log, dt_bias, g_norm):
    nc = seq_len // chunk
    per_step = GDN_CHUNKS_PER_STEP
    steps = nc // per_step
    rows = per_step * chunk
    t = n * seq_len
    bat = ba[:, :2 * HEADS_A].reshape(n * nc, chunk, 2 * HEADS_A).transpose(0, 2, 1)
    pad_row = lambda p: jnp.pad(p.reshape(1, HEADS_A), ((0, 0), (HEADS_A, LANES - 2 * HEADS_A)))
    pad_col = lambda p: jnp.pad(p.reshape(HEADS_A, 1), ((HEADS_A, 0), (0, 0)))
    blk = lambda i, j: (i * steps + j, 0)
    const = lambda i, j: (0, 0)
    col_blk = lambda col: pl.BlockSpec((rows, KEY_W_A), lambda i, j: (i * steps + j, col))
    return pl.pallas_call(
        functools.partial(_gdn_kernel, chunk=chunk, chunks_per_step=per_step),
        grid=(n, steps),
        in_specs=[col_blk(0), col_blk(1), col_blk(2), col_blk(3),
                  pl.BlockSpec((rows, LANES), blk),
                  pl.BlockSpec((per_step, 2 * HEADS_A, chunk), lambda i, j: (i * steps + j, 0, 0)),
                  pl.BlockSpec((None, HEADS_A, DK_A, DV_A), lambda i, j: (i, 0, 0, 0)),
                  pl.BlockSpec((1, LANES), const),
                  pl.BlockSpec((1, LANES), const),
                  pl.BlockSpec((2 * HEADS_A, 1), const),
                  pl.BlockSpec((2 * HEADS_A, 1), const),
                  pl.BlockSpec((1, DV_A), const)],
        out_specs=[pl.BlockSpec((rows, VAL_W_A), blk),
                   pl.BlockSpec((None, HEADS_A, DK_A, DV_A), lambda i, j: (i, 0, 0, 0))],
        out_shape=[jax.ShapeDtypeStruct((t, VAL_W_A), BF16),
                   jax.ShapeDtypeStruct((n, HEADS_A, DK_A, DV_A), F32)],
        compiler_params=_params("parallel", "arbitrary"),
        name="gdn_delta_rule",
    )(act, act, act, act, ba, bat, s0, pad_row(a_log), pad_row(dt_bias), pad_col(a_log), pad_col(dt_bias),
      g_norm.reshape(1, DV_A))


def _gdn_step_kernel(qkv_ref, z_ref, ba_ref, st_ref, cw_ref, s0_ref, alr_ref, dtr_ref, gn_ref, *rest):
    nb = qkv_ref.shape[0]
    if len(rest) == 3:
        prev_ref, o_ref, both_ref = rest
        both_ref[0] = prev_ref[...]
        s_ref = both_ref.at[1]
    else:
        o_ref, s_ref = rest
    conv = qkv_ref[...] * cw_ref[CONV_W - 1:CONV_W, :]
    for j in range(CONV_W - 1):
        conv = conv + st_ref[j] * cw_ref[j:j + 1, :]
    act = conv * _sigmoid(conv)
    ba = ba_ref[...]
    beta_cols = _sigmoid(ba)
    decay_cols = jnp.exp(-jnp.exp(alr_ref[...]) * _softplus(ba + dtr_ref[...]))
    fill = jnp.zeros((DK_A - nb, DK_A), F32)
    for h in range(HEADS_A):
        lo = h * DK_A
        q = act[:, lo:lo + DK_A]
        k = act[:, KEY_W_A + lo:KEY_W_A + lo + DK_A]
        v = act[:, 2 * KEY_W_A + lo:2 * KEY_W_A + lo + DV_A]
        q = q * lax.rsqrt(jnp.sum(q * q, axis=-1, keepdims=True) + EPS) * (DK_A ** -0.5)
        k = k * lax.rsqrt(jnp.sum(k * k, axis=-1, keepdims=True) + EPS)
        q_t = jnp.concatenate([q, fill], axis=0).T
        k_t = jnp.concatenate([k, fill], axis=0).T
        z = z_ref[:, lo:lo + DV_A]
        gate = z * _sigmoid(z)
        seqs = range(nb)
        k_cols = [k_t[:, i:i + 1] for i in seqs]
        decays = [decay_cols[i:i + 1, HEADS_A + h:HEADS_A + h + 1] for i in seqs]
        k_s = [jnp.sum(k_cols[i] * s0_ref[i, h], axis=0, keepdims=True) for i in seqs]
        v_new = [beta_cols[i:i + 1, h:h + 1] * (v[i:i + 1, :] - decays[i] * k_s[i]) for i in seqs]
        s_new = [decays[i] * s0_ref[i, h] + k_cols[i] * v_new[i] for i in seqs]
        for i in seqs:
            s_ref[i, h] = s_new[i]
        outs = [jnp.sum(q_t[:, i:i + 1] * s_new[i], axis=0, keepdims=True) for i in seqs]
        for i in seqs:
            o_ref[i:i + 1, lo:lo + DV_A] = _rms(outs[i], gn_ref[...]) * gate[i:i + 1, :]


def _gdn_step(qkvz, ba, s0_layers, conv0_layers, layer, w_conv, a_log, dt_bias, g_norm, s_prev=None):
    n = qkvz.shape[0]
    nb = SUBLANES
    state_blk = (nb, HEADS_A, DK_A, DV_A)
    state_idx = lambda i: (i, 0, 0, 0)
    if s_prev is None:
        extra_in, extra_specs = [], []
        state_spec = pl.BlockSpec(state_blk, state_idx)
        state_shape = jax.ShapeDtypeStruct((n,) + state_blk[1:], F32)
    else:
        extra_in, extra_specs = [s_prev], [pl.BlockSpec(state_blk, state_idx)]
        state_spec = pl.BlockSpec((2,) + state_blk, lambda i: (0, i, 0, 0, 0))
        state_shape = jax.ShapeDtypeStruct((2, n) + state_blk[1:], F32)
    pad_row = lambda p: jnp.pad(p.reshape(1, HEADS_A), ((0, 0), (HEADS_A, LANES - 2 * HEADS_A)))
    blk = lambda i: (i, 0)
    const = lambda i: (0, 0)
    return pl.pallas_call(
        _gdn_step_kernel,
        grid=(n // nb,),
        in_specs=[pl.BlockSpec((nb, QKV_W_A), blk),
                  pl.BlockSpec((nb, VAL_W_A), lambda i: (i, QKV_W_A // VAL_W_A)),
                  pl.BlockSpec((nb, LANES), blk),
                  pl.BlockSpec((None, CONV_W - 1, nb, QKV_W_A), lambda i: (layer, 0, i, 0)),
                  pl.BlockSpec((CONV_W, QKV_W_A), const),
                  pl.BlockSpec((None, nb, HEADS_A, DK_A, DV_A), lambda i: (layer, i, 0, 0, 0)),
                  pl.BlockSpec((1, LANES), const),
                  pl.BlockSpec((1, LANES), const),
                  pl.BlockSpec((1, DV_A), const)] + extra_specs,
        out_specs=[pl.BlockSpec((nb, VAL_W_A), blk), state_spec],
        out_shape=[jax.ShapeDtypeStruct((n, VAL_W_A), F32), state_shape],
        compiler_params=_params("parallel"),
        name="gdn_step",
    )(qkvz, qkvz, ba, conv0_layers.transpose(0, 2, 1, 3), w_conv, s0_layers, pad_row(a_log), pad_row(dt_bias),
      g_norm.reshape(1, DV_A), *extra_in)


def _attn_prompt_kernel(q_ref, kvc_ref, kvp_ref, o_ref, lse_ref, *, dilation, sub_blocks):
    blk = ATT_BLOCK
    hd = HEAD_DIM_B
    a = pl.program_id(2)
    qi = lax.broadcasted_iota(jnp.int32, (blk, 2 * blk), 0)
    kj = lax.broadcasted_iota(jnp.int32, (blk, 2 * blk), 1)
    steps = blk + qi - kj
    in_window = (steps >= 0) & (steps <= blk)
    started = in_window & ((kj >= blk) | (a > 0))
    dist = (steps * dilation).astype(F32)
    bias_first = [jnp.where(started, -slope * dist, -jnp.inf) for slope in ALIBI_SLOPES]
    bias_later = [jnp.where(in_window, -slope * dist, -jnp.inf) for slope in ALIBI_SLOPES]
    low = lax.broadcasted_iota(jnp.int32, (1, 2 * hd), 1) < hd
    kv_all = jnp.concatenate([kvp_ref[...], kvc_ref[...]], axis=0)
    padded = []
    for kvh in range(KV_HEADS_B):
        own = low if kvh == 0 else ~low
        both = []
        for x in (kv_all[:, :2 * hd], kv_all[:, 2 * hd:]):
            kept = jnp.where(own, x, jnp.zeros_like(x))
            moved = pltpu.roll(kept, hd, axis=1)
            both += [kept, moved] if kvh == 0 else [moved, kept]
        padded.append(both)
    head_lane = lax.broadcasted_iota(jnp.int32, (1, LANES), 1)
    group = min(sub_blocks, ATT_SUB_BLOCKS_PER_STAGE)
    for j0 in range(0, sub_blocks, group):
        tiles = [(j, h) for j in range(j0, j0 + group) for h in range(HEADS_B)]
        rows = {j: slice(j * blk, (j + 1) * blk) for j in range(j0, j0 + group)}
        kv_of = {(j, h): [x[j * blk:(j + 2) * blk] for x in padded[h // GQA_B]] for j, h in tiles}
        ss = [_dot_nt(q_ref[rows[j], (h // 2) * 2 * hd:(h // 2 + 1) * 2 * hd], kv_of[j, h][h % 2])
              + (bias_first if j == 0 else bias_later)[h] for j, h in tiles]
        ms = [jnp.max(s, axis=-1, keepdims=True) for s in ss]
        ps = [jnp.exp(s - m) for s, m in zip(ss, ms)]
        ls = [jnp.sum(p, axis=-1, keepdims=True) for p in ps]
        for t in range(0, len(tiles), 2):
            j, h = tiles[t]
            o = _dot(ps[t], kv_of[j, h][2]) + _dot(ps[t + 1], kv_of[j, h][3])
            o_ref[rows[j], h * hd:(h + 2) * hd] = (o / jnp.where(low, ls[t], ls[t + 1])).astype(o_ref.dtype)
        for j in range(j0, j0 + group):
            lse_all = jnp.zeros((blk, LANES), F32)
            for t, (jt, h) in enumerate(tiles):
                if jt == j:
                    lse_all = jnp.where(head_lane == h, ms[t] + jnp.log(ls[t]), lse_all)
            lse_ref[rows[j], :] = lse_all


def _attn_prompt_branch(q, kv, n, seq_len, dilation):
    ls = seq_len // dilation
    sub_blocks = min(4, ls // ATT_BLOCK)
    qb = sub_blocks * ATT_BLOCK
    qv = q.reshape(n, ls, dilation * Q_W_BRANCH)
    kvv = kv.reshape(n, ls, dilation * KV_W_BRANCH)
    cur = lambda i, r, a: (i, a, r)
    prev = lambda i, r, a: (i, jnp.maximum(a * sub_blocks - 1, 0), r)
    o, lse = pl.pallas_call(
        functools.partial(_attn_prompt_kernel, dilation=dilation, sub_blocks=sub_blocks),
        grid=(n, dilation, ls // qb),
        in_specs=[pl.BlockSpec((None, qb, Q_W_BRANCH), cur),
                  pl.BlockSpec((None, qb, KV_W_BRANCH), cur),
                  pl.BlockSpec((None, ATT_BLOCK, KV_W_BRANCH), prev)],
        out_specs=[pl.BlockSpec((None, qb, Q_W_BRANCH), cur),
                   pl.BlockSpec((None, None, qb, LANES), lambda i, r, a: (i, r, a, 0))],
        out_shape=[jax.ShapeDtypeStruct(qv.shape, BF16),
                   jax.ShapeDtypeStruct((n, dilation, ls, LANES), F32)],
        compiler_params=_params("parallel", "parallel", "arbitrary"),
        name=f"attn_prompt_d{dilation}",
    )(qv, kvv, kvv)
    return o.reshape(n * ls, dilation * Q_W_BRANCH), lse


def _cache_update_kernel(c_ref, kvn_ref, o_ref, g_ref, *, window, dilation):
    nb = c_ref.shape[0]
    nk = ATT_BLOCK
    last = lax.broadcasted_iota(jnp.int32, (1, window), 1) == window - 1
    new_t = jnp.concatenate([kvn_ref[...], jnp.zeros((LANES - nb, KV_W_BRANCH), F32)], axis=0).T
    if dilation > 1:
        src = lax.broadcasted_iota(jnp.int32, (window, nk), 0)
        dst = lax.broadcasted_iota(jnp.int32, (window, nk), 1)
        pick = (src == dst * dilation).astype(BF16)
    for i in range(nb):
        x = c_ref[i].reshape(KV_W_BRANCH, window)
        shifted = jnp.where(last, new_t[:, i:i + 1], pltpu.roll(x, window - 1, axis=1))
        o_ref[i] = shifted.reshape(o_ref.shape[1:])
        seen = x.astype(BF16) if dilation == 1 else _dot(x, pick).astype(BF16)
        g_ref[i] = seen.reshape(g_ref.shape[1:])


def _cache_update(cache_t, kv_new, window, dilation):
    n = cache_t.shape[0]
    nb = max(1, min(SUBLANES, 4096 // window))
    blk = (nb, 2, KV_HEADS_B, HEAD_DIM_B, window)
    seen_blk = (nb, 2, KV_HEADS_B * HEAD_DIM_B, ATT_BLOCK)
    return pl.pallas_call(
        functools.partial(_cache_update_kernel, window=window, dilation=dilation),
        grid=(n // nb,),
        in_specs=[pl.BlockSpec(blk, lambda i: (i, 0, 0, 0, 0)),
                  pl.BlockSpec((None, nb, KV_W_BRANCH), lambda i: (i, 0, 0))],
        out_specs=[pl.BlockSpec(blk, lambda i: (i, 0, 0, 0, 0)),
                   pl.BlockSpec(seen_blk, lambda i: (i, 0, 0, 0))],
        out_shape=[jax.ShapeDtypeStruct(cache_t.shape, F32),
                   jax.ShapeDtypeStruct((n,) + seen_blk[1:], BF16)],
        compiler_params=_params("parallel"),
        name=f"cache_update_w{window}",
    )(cache_t, kv_new.reshape(n // nb, nb, KV_W_BRANCH))


def _attn_sample_kernel(q_ref, kvn_ref, g0_ref, g1_ref, g2_ref, o_ref):
    nk = ATT_BLOCK
    nb = q_ref.shape[0]
    kv_w = KV_HEADS_B * HEAD_DIM_B
    head = lax.broadcasted_iota(jnp.int32, (HEADS_B, 1), 0)
    slopes = jnp.exp2(-8.0 * (head + 1).astype(F32) / HEADS_B)
    key = lax.broadcasted_iota(jnp.int32, (1, nk), 1)
    lane = lax.broadcasted_iota(jnp.int32, (HEADS_B, kv_w), 1)
    own_half = (lane // HEAD_DIM_B) == (head // GQA_B)
    scale = HEAD_DIM_B ** -0.5
    g_refs = (g0_ref, g1_ref, g2_ref)
    pairs = [(i, b) for i in range(nb) for b in range(N_BRANCH)]
    dists = [((nk - key) * d).astype(F32) for d in BRANCH_DILATIONS]
    qs = [q_ref[i, b * HEADS_B:(b + 1) * HEADS_B, :] * scale for i, b in pairs]
    s_old = [_dot(q, g_refs[b][i, 0]) - slopes * dists[b] for q, (i, b) in zip(qs, pairs)]
    s_new = [jnp.sum(q * kvn_ref[b, i:i + 1, :kv_w], axis=-1, keepdims=True) for q, (i, b) in zip(qs, pairs)]
    ms = [jnp.maximum(jnp.max(so, axis=-1, keepdims=True), sn) for so, sn in zip(s_old, s_new)]
    p_old = [jnp.exp(so - m) for so, m in zip(s_old, ms)]
    p_new = [jnp.exp(sn - m) for sn, m in zip(s_new, ms)]
    ls = [jnp.sum(po, axis=-1, keepdims=True) + pn for po, pn in zip(p_old, p_new)]
    outs = [(_dot_nt(po, g_refs[b][i, 1]) + pn * kvn_ref[b, i:i + 1, kv_w:]) / l
            for po, pn, l, (i, b) in zip(p_old, p_new, ls, pairs)]
    lses = [m + jnp.log(l) for m, l in zip(ms, ls)]
    for i in range(nb):
        sl = slice(i * N_BRANCH, (i + 1) * N_BRANCH)
        m = functools.reduce(jnp.maximum, lses[sl])
        ws = [jnp.exp(l - m) for l in lses[sl]]
        merged = sum(w * o for w, o in zip(ws, outs[sl])) / sum(ws)
        o_ref[i] = jnp.where(own_half, merged, 0.0)


def _spread_heads(w, axis):
    w = jnp.moveaxis(w, axis, -1)
    lead = w.shape[:-1]
    w = w.reshape(lead + (-1, KV_HEADS_B, GQA_B, HEAD_DIM_B))
    halves = [jnp.pad(w[..., g, :, :], [(0, 0)] * (len(lead) + 2) + [(g * HEAD_DIM_B, (KV_HEADS_B - 1 - g) * HEAD_DIM_B)])
              for g in range(KV_HEADS_B)]
    out = jnp.stack(halves, axis=-3)
    return jnp.moveaxis(out.reshape(lead + (-1,)), -1, axis)


def _attn_sample(q, kv_new, seen):
    n = q.shape[0]
    nb = SUBLANES
    rows = N_BRANCH * HEADS_B
    seen_blk = (nb, 2, KV_HEADS_B * HEAD_DIM_B, ATT_BLOCK)
    out = pl.pallas_call(
        _attn_sample_kernel,
        grid=(n // nb,),
        in_specs=[pl.BlockSpec((nb, rows, LANES), lambda i: (i, 0, 0)),
                  pl.BlockSpec((N_BRANCH, nb, KV_W_BRANCH), lambda i: (0, i, 0))]
                 + [pl.BlockSpec(seen_blk, lambda i: (i, 0, 0, 0))] * N_BRANCH,
        out_specs=pl.BlockSpec((nb, HEADS_B, LANES), lambda i: (i, 0, 0)),
        out_shape=jax.ShapeDtypeStruct((n, HEADS_B, LANES), F32),
        compiler_params=_params("parallel"),
        name="attn_sample",
    )(q.reshape(n, rows, LANES), kv_new, *seen)
    return out.reshape(n, HEADS_B * LANES)


def _trunk(x, rec0, conv0, kv_bufs, w):
    n, seq_len, d = x.shape
    fresh = kv_bufs is None
    assert seq_len == 1 or (fresh and seq_len % (ATT_BLOCK * max(BRANCH_DILATIONS)) == 0)
    x = x.reshape(n * seq_len, d)
    rec_new, conv_new = [], []
    for l in range(N_LAYERS_A):
        if fresh:
            act, ba, tail = _inproj_conv(x, w['norm_mix'][l], w['w_in_a'][l], w['conv_a'][l], conv0[l], n, seq_len)
            last_tile = tail.reshape(n, -1, SUBLANES, QKV_W_A)[:, -1]
            conv_new.append(last_tile[:, SUBLANES - (CONV_W - 1):])
            o, s_new = _gdn(act, ba, n, seq_len, GDN_CHUNK, rec0[l], w['a_log'][l], w['dt_bias'][l],
                            w['norm_o_a'][l])
        else:
            qkvz, ba = _inproj(x, w['norm_mix'][l], w['w_in_a'][l])
            conv_new.append(jnp.concatenate([conv0[l][:, 1:], qkvz[:, None, :QKV_W_A]], axis=1))
            s_prev = rec_new[0] if l == N_LAYERS_A - 1 == 1 else None
            o, s_new = _gdn_step(qkvz, ba, rec0, conv0, l, w['conv_a'][l], w['a_log'][l], w['dt_bias'][l],
                                 w['norm_o_a'][l], s_prev)
        rec_new.append(s_new)
        x = _mlp_block(x, [o], w['w_out_a'][l], w['norm_mlp'][l], w['w_up'], w['w_down'], l,
                       w['norm_final'], False)

    if fresh:
        kv_lo, kv = _branch_proj(x, w['norm_kv'], w['w_kv'], True)
    else:
        kv, = _norm_matmul(x, w['norm_kv'], w['w_kv'], KV_W_BRANCH, True, (F32,))
    bufs_new, seen = [], []
    for b in range(N_BRANCH):
        if fresh:
            keep = min(BRANCH_WINDOWS[b], seq_len)
            last = kv[b].reshape(n, seq_len, KV_W_BRANCH)[:, seq_len - keep:]
            bufs_new.append(last.reshape(n, keep, 2, KV_HEADS_B, HEAD_DIM_B))
        else:
            assert kv_bufs[b].shape[1] == BRANCH_WINDOWS[b]
            shifted, seen_b = _cache_update(jnp.transpose(kv_bufs[b], (0, 2, 3, 4, 1)), kv[b],
                                            BRANCH_WINDOWS[b], BRANCH_DILATIONS[b])
            bufs_new.append(jnp.transpose(shifted, (0, 4, 1, 2, 3)))
            seen.append(seen_b)

    for l in range(N_LAYERS_A, N_LAYERS_A + N_LAYERS_B):
        lb = l - N_LAYERS_A
        if fresh:
            q, _ = _branch_proj(x, w['norm_mix'][l], w['w_q_b'][lb], False, HEAD_DIM_B ** -0.5)
            parts = [_attn_prompt_branch(q[b], kv_lo[b], n, seq_len, BRANCH_DILATIONS[b]) for b in range(N_BRANCH)]
            mix = [p[0] for p in parts] + [p[1] for p in parts]
            w_o = w['w_o_b'][lb]
        else:
            q, = _norm_matmul(x, w['norm_mix'][l], _spread_heads(w['w_q_b'][lb], 1), 1024, False, (F32,))
            mix = [_attn_sample(q, jnp.stack(kv), seen)]
            w_o = _spread_heads(w['w_o_b'][lb], 0)
        x = _mlp_block(x, mix, w_o, w['norm_mlp'][l], w['w_up'], w['w_down'], l,
                       w['norm_final'], l == N_LAYERS_A + N_LAYERS_B - 1)
    rec_out = rec_new[-1] if rec_new[-1].ndim == rec0.ndim else jnp.stack(rec_new)
    return (x.reshape(n, seq_len, d), rec_out, jnp.stack(conv_new), *bufs_new)


def kernel(x_prompt, x_sample, state_a_rec, state_a_conv, cache_b0_kv, cache_b1_kv, cache_b2_kv, norm_mix, norm_mlp, w_in_a, conv_a, a_log, dt_bias, norm_o_a, w_out_a, norm_kv, w_kv, w_q_b, w_o_b, w_up, w_down, norm_final):
    w = {'norm_mix': norm_mix, 'norm_mlp': norm_mlp, 'w_in_a': w_in_a, 'conv_a': conv_a, 'a_log': a_log,
         'dt_bias': dt_bias, 'norm_o_a': norm_o_a, 'w_out_a': w_out_a, 'norm_kv': norm_kv, 'w_kv': w_kv,
         'w_q_b': w_q_b, 'w_o_b': w_o_b, 'w_up': w_up, 'w_down': w_down, 'norm_final': norm_final}
    n_p = x_prompt.shape[0]
    p_rec0 = jnp.zeros((N_LAYERS_A, n_p, HEADS_A, DK_A, DV_A), state_a_rec.dtype)
    p_conv0 = jnp.zeros((N_LAYERS_A, n_p, CONV_W - 1, QKV_W_A), state_a_conv.dtype)
    prompt = _trunk(x_prompt, p_rec0, p_conv0, None, w)
    sample = _trunk(x_sample, state_a_rec, state_a_conv, [cache_b0_kv, cache_b1_kv, cache_b2_kv], w)
    return (prompt[0], sample[0], *prompt[1:], *sample[1:])
```
